```python
import math
import jax, jax.numpy as jnp
from jax import lax
import numpy as np

D_MODEL = 1024
BATCH = 2
SEQ = 16384
DEPTH = 2
DEC_BATCH = 8
DEC_SEQ = 32
PAST_LEN = 1024

CHUNK = 64
WINDOW = 128
WIN_CHUNKS = WINDOW // CHUNK
BAND = WINDOW + CHUNK
HEAD_DIM = 64
ATT_DIM = 3 * D_MODEL // 4
N_HEADS = ATT_DIM // HEAD_DIM
N_KV = 4
GROUP = N_HEADS // N_KV
KV_DIM = N_KV * HEAD_DIM
ATTN_SCALE = HEAD_DIM ** -0.5
ALIBI_MAX = 8.0
NEG_INF = -1e30
POOL_DIM = D_MODEL - ATT_DIM
POOL_WINDOWS = (2, 4, 8, 16)
N_POOL_GROUPS = len(POOL_WINDOWS)
POOL_CH = POOL_DIM // N_POOL_GROUPS
POOL_PAD = max(POOL_WINDOWS) - 1
PROJ_DIM = ATT_DIM + 2 * KV_DIM + POOL_DIM
N_EXPERTS = 32
TOP_K = 4
D_FF = D_MODEL
SWIGLU_LIMIT = 7.0
SWIGLU_ALPHA = 1.702
MOE_BLOCK = 128
LN_EPS = 1e-5
DEEPNORM_ALPHA = (2.0 * DEPTH) ** 0.25
DEEPNORM_BETA = (8.0 * DEPTH) ** -0.25

kernel_name = "hybrid_swa_sink_pool_moe_streaming_step"


def _layer_norm(x, g=None, b=None):
    xf = x.astype(jnp.float32)
    mu = jnp.mean(xf, axis=-1, keepdims=True)
    xc = xf - mu
    var = jnp.mean(xc * xc, axis=-1, keepdims=True)
    y = xc * lax.rsqrt(var + LN_EPS)
    if g is not None:
        y = y * g.astype(jnp.float32) + b.astype(jnp.float32)
    return y.astype(x.dtype)


def _adaln(c, w_ada, b_ada):
    m = jax.nn.silu(c.astype(jnp.float32)) @ w_ada.astype(jnp.float32) + b_ada.astype(jnp.float32)
    return m.reshape(c.shape[0], 6, D_MODEL).astype(c.dtype)


def _modulate(x, shift, scale):
    return _layer_norm(x) * (1 + scale[:, None, :]) + shift[:, None, :]


def _alibi_slopes():
    return jnp.exp2(-ALIBI_MAX * (jnp.arange(N_HEADS, dtype=jnp.float32) + 1.0) / N_HEADS)


def _in_proj(h, w_in, b_in):
    B, T, _ = h.shape
    p = jnp.einsum('btd,de->bte', h, w_in) + b_in
    q = p[..., :ATT_DIM].reshape(B, T, N_KV, GROUP, HEAD_DIM)
    k = p[..., ATT_DIM:ATT_DIM + KV_DIM].reshape(B, T, N_KV, HEAD_DIM)
    v = p[..., ATT_DIM + KV_DIM:ATT_DIM + 2 * KV_DIM].reshape(B, T, N_KV, HEAD_DIM)
    u = p[..., ATT_DIM + 2 * KV_DIM:]
    return q, k, v, u


def _softmax_with_sink(s, sinks):
    sk = jnp.broadcast_to(sinks.astype(jnp.float32)[:, :, None, None], s.shape[:-1] + (1,))
    p = jax.nn.softmax(jnp.concatenate([s, sk], axis=-1), axis=-1)
    return p[..., :-1]


def _attn_prompt(q, k, v, sinks, slopes):
    B, S = q.shape[:2]
    n_c = S // CHUNK
    qb = q.reshape(B, n_c, CHUNK, N_KV, GROUP, HEAD_DIM)
    pad = jnp.zeros((B, WINDOW, N_KV, HEAD_DIM), k.dtype)
    kp = jnp.concatenate([pad, k], axis=1)
    vp = jnp.concatenate([pad, v], axis=1)

    def band(xp):
        parts = [xp[:, j * CHUNK:j * CHUNK + S].reshape(B, n_c, CHUNK, N_KV, HEAD_DIM)
                 for j in range(WIN_CHUNKS + 1)]
        return jnp.stack(parts, axis=2).reshape(B, n_c, BAND, N_KV, HEAD_DIM)

    kb, vb = band(kp), band(vp)
    s = jnp.einsum('bcqkgd,bcjkd->bckgqj', qb, kb).astype(jnp.float32) * ATTN_SCALE
    qi = jnp.arange(CHUNK)
    kj = jnp.arange(BAND)
    dist = jnp.abs(qi[:, None] - kj[None, :] + WINDOW).astype(jnp.float32)
    bias = (-slopes[:, None, None] * dist).reshape(N_KV, GROUP, CHUNK, BAND)
    valid = (jnp.arange(n_c)[:, None] * CHUNK + kj[None, :]) >= WINDOW
    s = jnp.where(valid[None, :, None, None, None, :], s + bias, NEG_INF)
    p = _softmax_with_sink(s, sinks.reshape(N_KV, GROUP))
    o = jnp.einsum('bckgqj,bcjkd->bcqkgd', p.astype(vb.dtype), vb)
    return o.reshape(B, S, ATT_DIM)


def _attn_sample(q, k, v, k_cache, v_cache, sinks, slopes):
    B, T = q.shape[:2]
    L = k_cache.shape[1]
    kk = jnp.concatenate([k_cache.astype(k.dtype), k], axis=1)
    vv = jnp.concatenate([v_cache.astype(v.dtype), v], axis=1)
    qpos = PAST_LEN + jnp.arange(T)
    kpos = PAST_LEN - L + jnp.arange(L + T)
    dc = qpos[:, None] // CHUNK - kpos[None, :] // CHUNK
    vis = (dc >= 0) & (dc <= WIN_CHUNKS) & (kpos[None, :] >= 0)
    dist = jnp.abs(qpos[:, None] - kpos[None, :]).astype(jnp.float32)
    bias = (-slopes[:, None, None] * dist).reshape(N_KV, GROUP, T, L + T)
    s = jnp.einsum('btkgd,bjkd->bkgtj', q, kk).astype(jnp.float32) * ATTN_SCALE
    s = jnp.where(vis, s + bias, NEG_INF)
    p = _softmax_with_sink(s, sinks.reshape(N_KV, GROUP))
    o = jnp.einsum('bkgtj,bjkd->btkgd', p.astype(vv.dtype), vv)
    return o.reshape(B, T, ATT_DIM), kk[:, -L:], vv[:, -L:]


def _pool_mix(u, prefix, pos0, w_pool, pool_scale):
    B, T, _ = u.shape
    up = jnp.concatenate([prefix.astype(u.dtype), u], axis=1)
    upf = up.astype(jnp.float32)
    cs = jnp.concatenate([jnp.zeros((B, 1, POOL_DIM), jnp.float32), jnp.cumsum(upf, axis=1)], axis=1)
    pos = pos0 + jnp.arange(T)
    outs = []
    for g, w in enumerate(POOL_WINDOWS):
        sl = slice(g * POOL_CH, (g + 1) * POOL_CH)
        wsum = cs[:, POOL_PAD + 1:POOL_PAD + 1 + T, sl] - cs[:, POOL_PAD + 1 - w:POOL_PAD + 1 - w + T, sl]
        cnt = jnp.minimum(w, pos + 1).astype(jnp.float32)
        outs.append(wsum / cnt[None, :, None] - upf[:, POOL_PAD:, sl])
    pooled = jnp.stack(outs, axis=2).astype(u.dtype)
    mixed = jnp.einsum('btgc,gcd->btgd', pooled, w_pool).reshape(B, T, POOL_DIM)
    return mixed * pool_scale, up[:, -POOL_PAD:]


def _moe(h, w_router, b_router, w_gu, b_gu, w_down, b_down):
    B, T, D = h.shape
    n_tok = B * T
    hf = h.reshape(n_tok, D)
    logits = hf.astype(jnp.float32) @ w_router.astype(jnp.float32) + b_router.astype(jnp.float32)
    top_val, top_idx = lax.top_k(logits, TOP_K)
    gates = jax.nn.softmax(top_val, axis=-1)
    n_asg = n_tok * TOP_K
    flat_e = top_idx.reshape(-1).astype(jnp.int32)
    flat_tok = jnp.repeat(jnp.arange(n_tok, dtype=jnp.int32), TOP_K)
    order = jnp.argsort(flat_e)
    sorted_e = flat_e[order]
    counts = jnp.bincount(flat_e, length=N_EXPERTS).astype(jnp.int32)
    starts = jnp.cumsum(counts) - counts
    padded = (counts + MOE_BLOCK - 1) // MOE_BLOCK * MOE_BLOCK
    pad_ends = jnp.cumsum(padded)
    pad_starts = pad_ends - padded
    dest = pad_starts[sorted_e] + jnp.arange(n_asg, dtype=jnp.int32) - starts[sorted_e]
    n_blocks = (n_asg + N_EXPERTS * (MOE_BLOCK - 1) + MOE_BLOCK - 1) // MOE_BLOCK
    n_rows = n_blocks * MOE_BLOCK
    slot_tok = jnp.full((n_rows,), n_tok, jnp.int32).at[dest].set(flat_tok[order])
    dest_asg = jnp.zeros((n_asg,), jnp.int32).at[order].set(dest)
    block_start = jnp.arange(n_blocks, dtype=jnp.int32) * MOE_BLOCK
    block_e = jnp.minimum(jnp.searchsorted(pad_ends, block_start, side='right'), N_EXPERTS - 1)
    h_pad = jnp.concatenate([hf, jnp.zeros((1, D), hf.dtype)], axis=0)
    xs = h_pad[slot_tok].reshape(n_blocks, MOE_BLOCK, D)

    def expert_block(args):
        xb, e = args
        gu = xb @ w_gu[e] + b_gu[e]
        gate, up = gu[:, :D_FF], gu[:, D_FF:]
        gate = jnp.minimum(gate, SWIGLU_LIMIT)
        up = jnp.clip(up, -SWIGLU_LIMIT, SWIGLU_LIMIT)
        a = gate * jax.nn.sigmoid(SWIGLU_ALPHA * gate) * (up + 1)
        return a @ w_down[e] + b_down[e]

    ys = lax.map(expert_block, (xs, block_e)).reshape(n_rows, D)
    y = ys[dest_asg].reshape(n_tok, TOP_K, D)
    out = jnp.einsum('tk,tkd->td', gates.astype(y.dtype), y)
    return out.reshape(B, T, D)


def _layer(x, c, mixer, w_ada, b_ada, w_in, b_in, w_out, b_out, ln1_g, ln1_b, ln2_g, ln2_b,
           w_router, b_router, w_gu, b_gu, w_down, b_down):
    mod = _adaln(c, w_ada, b_ada)
    h = _modulate(x, mod[:, 0], mod[:, 1])
    q, k, v, u = _in_proj(h, w_in, b_in)
    att, pool, states = mixer(q, k, v, u)
    mix = jnp.einsum('bte,ed->btd', jnp.concatenate([att, pool], axis=-1), w_out) + b_out
    x = _layer_norm(DEEPNORM_ALPHA * x + mod[:, 2, None, :] * mix, ln1_g, ln1_b)
    h = _modulate(x, mod[:, 3], mod[:, 4])
    f = _moe(h, w_router, b_router, w_gu, b_gu, w_down, b_down)
    x = _layer_norm(DEEPNORM_ALPHA * x + mod[:, 5, None, :] * f, ln2_g, ln2_b)
    return x, states


def setup_inputs(seed: int = 0) -> dict:
    key = jax.random.key(seed)
    ks = jax.random.split(key, 26)
    nrm = jax.random.normal
    f32 = jnp.float32
    L = min(WINDOW, PAST_LEN)
    return {
        "x_prompt": nrm(ks[0], (BATCH, SEQ, D_MODEL), f32),
        "x_sample": nrm(ks[1], (DEC_BATCH, DEC_SEQ, D_MODEL), f32),
        "c_prompt": nrm(ks[2], (BATCH, D_MODEL), f32),
        "c_sample": nrm(ks[3], (DEC_BATCH, D_MODEL), f32),
        "cache_k": nrm(ks[4], (DEPTH, DEC_BATCH, L, N_KV, HEAD_DIM), f32),
        "cache_v": nrm(ks[5], (DEPTH, DEC_BATCH, L, N_KV, HEAD_DIM), f32),
        "state_pool": nrm(ks[6], (DEPTH, DEC_BATCH, POOL_PAD, POOL_DIM), f32),
        "w_ada": nrm(ks[7], (DEPTH, D_MODEL, 6 * D_MODEL), f32) * (0.5 * D_MODEL ** -0.5),
        "b_ada": nrm(ks[8], (DEPTH, 6 * D_MODEL), f32) * 0.02,
        "w_in": nrm(ks[9], (DEPTH, D_MODEL, PROJ_DIM), f32) * D_MODEL ** -0.5,
        "b_in": nrm(ks[10], (DEPTH, PROJ_DIM), f32) * 0.02,
        "sinks": nrm(ks[11], (DEPTH, N_HEADS), f32) * 0.5,
        "w_pool": nrm(ks[12], (DEPTH, N_POOL_GROUPS, POOL_CH, POOL_CH), f32) * POOL_CH ** -0.5,
        "pool_scale": 1.0 + 0.1 * nrm(ks[13], (DEPTH, POOL_DIM), f32),
        "w_out": nrm(ks[14], (DEPTH, D_MODEL, D_MODEL), f32) * (DEEPNORM_BETA * D_MODEL ** -0.5),
        "b_out": nrm(ks[15], (DEPTH, D_MODEL), f32) * 0.02,
        "ln1_g": 1.0 + 0.05 * nrm(ks[16], (DEPTH, D_MODEL), f32),
        "ln1_b": 0.02 * nrm(ks[17], (DEPTH, D_MODEL), f32),
        "ln2_g": 1.0 + 0.05 * nrm(ks[18], (DEPTH, D_MODEL), f32),
        "ln2_b": 0.02 * nrm(ks[19], (DEPTH, D_MODEL), f32),
        "w_router": nrm(ks[20], (DEPTH, D_MODEL, N_EXPERTS), f32) * D_MODEL ** -0.5,
        "b_router": nrm(ks[21], (DEPTH, N_EXPERTS), f32) * 0.01,
        "w_gu": nrm(ks[22], (DEPTH, N_EXPERTS, D_MODEL, 2 * D_FF), f32) * D_MODEL ** -0.5,
        "b_gu": nrm(ks[23], (DEPTH, N_EXPERTS, 2 * D_FF), f32) * 0.01,
        "w_down": nrm(ks[24], (DEPTH, N_EXPERTS, D_FF, D_MODEL), f32) * (DEEPNORM_BETA * D_FF ** -0.5),
        "b_down": nrm(ks[25], (DEPTH, N_EXPERTS, D_MODEL), f32) * 0.01,
    }


def reference(x_prompt, x_sample, c_prompt, c_sample, cache_k, cache_v, state_pool,
              w_ada, b_ada, w_in, b_in, sinks, w_pool, pool_scale, w_out, b_out,
              ln1_g, ln1_b, ln2_g, ln2_b, w_router, b_router, w_gu, b_gu, w_down, b_down):
    slopes = _alibi_slopes()
    yp, ys = x_prompt, x_sample
    kp_l, vp_l, pp_l, ks_l, vs_l, ps_l = [], [], [], [], [], []
    for l in range(DEPTH):
        def mixer_prompt(q, k, v, u, l=l):
            att = _attn_prompt(q, k, v, sinks[l], slopes)
            prefix = jnp.zeros((u.shape[0], POOL_PAD, POOL_DIM), u.dtype)
            pool, pst = _pool_mix(u, prefix, 0, w_pool[l], pool_scale[l])
            return att, pool, (k[:, -WINDOW:], v[:, -WINDOW:], pst)

        def mixer_sample(q, k, v, u, l=l):
            att, kn, vn = _attn_sample(q, k, v, cache_k[l], cache_v[l], sinks[l], slopes)
            pool, pst = _pool_mix(u, state_pool[l], PAST_LEN, w_pool[l], pool_scale[l])
            return att, pool, (kn, vn, pst)

        weights = (w_ada[l], b_ada[l], w_in[l], b_in[l], w_out[l], b_out[l], ln1_g[l], ln1_b[l],
                   ln2_g[l], ln2_b[l], w_router[l], b_router[l], w_gu[l], b_gu[l], w_down[l], b_down[l])
        yp, (kp, vp, pp) = _layer(yp, c_prompt, mixer_prompt, *weights)
        ys, (kn, vn, pn) = _layer(ys, c_sample, mixer_sample, *weights)
        kp_l.append(kp); vp_l.append(vp); pp_l.append(pp)
        ks_l.append(kn); vs_l.append(vn); ps_l.append(pn)
    new_k_prompt = jnp.stack(kp_l)
    new_v_prompt = jnp.stack(vp_l)
    new_pool_prompt = jnp.stack(pp_l)
    new_k_sample = jnp.stack(ks_l)
    new_v_sample = jnp.stack(vs_l)
    new_pool_sample = jnp.stack(ps_l)
    return (yp, ys, new_k_prompt, new_v_prompt, new_pool_prompt, new_k_sample, new_v_sample, new_pool_sample)
```

```python
import functools

import jax
import jax.numpy as jnp
from jax import lax
from jax.experimental import pallas as pl
from jax.experimental.pallas import tpu as pltpu

F32 = jnp.float32
BF16 = jnp.bfloat16

D_MODEL = 1024
DEPTH = 2
CHUNK = 64
WINDOW = 128
WIN_CHUNKS = WINDOW // CHUNK
HEAD_DIM = 64
ATT_DIM = 768
N_HEADS = 12
N_KV = 4
GROUP = N_HEADS // N_KV
KV_DIM = N_KV * HEAD_DIM
ATTN_SCALE = HEAD_DIM ** -0.5
ALIBI_MAX = 8.0
NEG_INF = -1e30
POOL_DIM = D_MODEL - ATT_DIM
POOL_WINDOWS = (2, 4, 8, 16)
POOL_CH = POOL_DIM // len(POOL_WINDOWS)
POOL_PAD = max(POOL_WINDOWS) - 1
N_EXPERTS = 32
TOP_K = 4
D_FF = D_MODEL
SWIGLU_LIMIT = 7.0
SWIGLU_ALPHA = 1.702
LN_EPS = 1e-5
DEEPNORM_ALPHA = (2.0 * DEPTH) ** 0.25
PAST_LEN = 1024

LANES = 128
SUBLANES = 8
VMEM_LIMIT = 52 * 1024 * 1024

TOK_TILE = 512
Q_TILE = 128
POOL_HALO = 16
KD_DIM = N_KV * LANES
PROJ_COLS = ATT_DIM + 2 * KD_DIM + POOL_DIM
MOE_ROWS = 256
DISPATCH_TILE = 512
COMBINE_TILE = 256


def _params(sem, vmem=VMEM_LIMIT):
    return pltpu.CompilerParams(dimension_semantics=sem, vmem_limit_bytes=vmem)


def _layer_norm(x):
    mu = jnp.mean(x, axis=-1, keepdims=True)
    xc = x - mu
    var = jnp.mean(xc * xc, axis=-1, keepdims=True)
    return xc * lax.rsqrt(var + LN_EPS)


def _adaln_kernel(c_ref, w_ref, b_ref, o_ref):
    c = c_ref[...]
    s = c * jax.nn.sigmoid(c)
    o_ref[...] = jnp.dot(s, w_ref[...], preferred_element_type=F32,
                         precision=lax.Precision.HIGHEST) + b_ref[...]


def _adaln(c_all, w_ada, b_ada):
    nb = c_all.shape[0]
    ncol = 6 * D_MODEL
    tn = D_MODEL
    return pl.pallas_call(
        _adaln_kernel,
        grid=(DEPTH, ncol // tn),
        in_specs=[
            pl.BlockSpec((nb, D_MODEL), lambda l, j: (0, 0)),
            pl.BlockSpec((None, D_MODEL, tn), lambda l, j: (l, 0, j)),
            pl.BlockSpec((None, 1, tn), lambda l, j: (l, 0, j)),
        ],
        out_specs=pl.BlockSpec((None, nb, tn), lambda l, j: (l, 0, j)),
        out_shape=jax.ShapeDtypeStruct((DEPTH, nb, ncol), F32),
        compiler_params=_params(("parallel", "parallel")),
        name="adaln",
    )(c_all, w_ada, b_ada.reshape(DEPTH, 1, ncol))


def _inproj_kernel(x_ref, mod_ref, w_ref, b_ref, q_ref, kd_ref, vd_ref, u_ref):
    h = _layer_norm(x_ref[...]) * (1.0 + mod_ref[1:2, :]) + mod_ref[0:1, :]
    p = jnp.dot(h.astype(BF16), w_ref[...], preferred_element_type=F32) + b_ref[...]
    q_ref[...] = p[:, :ATT_DIM].astype(BF16)
    kd_ref[...] = p[:, ATT_DIM:ATT_DIM + KD_DIM].astype(BF16)
    vd_ref[...] = p[:, ATT_DIM + KD_DIM:ATT_DIM + 2 * KD_DIM].astype(BF16)
    u_ref[...] = p[:, ATT_DIM + 2 * KD_DIM:]


def _inproj(x, mod, w, b, tile, mod_row0):
    nb, s, _ = x.shape
    nt = s // tile

    def tok(c):
        return pl.BlockSpec((None, tile, c), lambda bi, i: (bi, i, 0))

    return pl.pallas_call(
        _inproj_kernel,
        grid=(nb, nt),
        in_specs=[
            tok(D_MODEL),
            pl.BlockSpec((None, 6, D_MODEL), lambda bi, i: (bi + mod_row0, 0, 0)),
            pl.BlockSpec((D_MODEL, PROJ_COLS), lambda bi, i: (0, 0)),
            pl.BlockSpec((1, PROJ_COLS), lambda bi, i: (0, 0)),
        ],
        out_specs=[tok(ATT_DIM), tok(KD_DIM), tok(KD_DIM), tok(POOL_DIM)],
        out_shape=[
            jax.ShapeDtypeStruct((nb, s, ATT_DIM), BF16),
            jax.ShapeDtypeStruct((nb, s, KD_DIM), BF16),
            jax.ShapeDtypeStruct((nb, s, KD_DIM), BF16),
            jax.ShapeDtypeStruct((nb, s, POOL_DIM), F32),
        ],
        compiler_params=_params(("parallel", "parallel")),
        name="inproj",
    )(x, mod, w, b)


def _mixer_kernel(sinks_ref, q_ref, kd_ref, kdh_ref, vd_ref, vdh_ref, u_ref, uh_ref,
                  bias_ref, wpool_ref, pscale_ref, o_ref, kall_ref, vall_ref,
                  *, tq, history_starts_empty, pos0):
    i = pl.program_id(1)
    t = q_ref.shape[0]
    hk = kdh_ref.shape[0]
    nk = hk + tq
    kall_ref[0:hk, :] = kdh_ref[...]
    kall_ref[hk:, :] = kd_ref[...]
    vall_ref[0:hk, :] = vdh_ref[...]
    vall_ref[hk:, :] = vd_ref[...]

    low_half = lax.broadcasted_iota(jnp.int32, (tq, LANES), 1) < HEAD_DIM
    if history_starts_empty:
        col = lax.broadcasted_iota(jnp.int32, (1, nk), 1)
        no_history = jnp.where((col < hk) & (i == 0), NEG_INF, 0.0).astype(F32)

    for s in range(t // tq):
        rows = slice(s * tq, (s + 1) * tq)
        keys = slice(s * tq, s * tq + nk)
        for pair in range(N_HEADS // 2):
            q2 = q_ref[rows, pair * LANES:(pair + 1) * LANES]
            halves = []
            for half in range(2):
                head = 2 * pair + half
                kv = head // GROUP
                qm = jnp.where(low_half if half == 0 else ~low_half, q2, jnp.zeros_like(q2))
                kd = kall_ref[keys, kv * LANES:(kv + 1) * LANES]
                sc = lax.dot_general(qm, kd, (((1,), (1,)), ((), ())),
                                     preferred_element_type=F32)
                sc = sc + bias_ref[head]
                if history_starts_empty and s == 0:
                    sc = sc + no_history
                sink = sinks_ref[head]
                m = jnp.maximum(jnp.max(sc, axis=1, keepdims=True), sink)
                p = jnp.exp(sc - m)
                den = jnp.sum(p, axis=1, keepdims=True) + jnp.exp(sink - m)
                vd = vall_ref[keys, kv * LANES:(kv + 1) * LANES]
                o2 = jnp.dot(p.astype(BF16), vd, preferred_element_type=F32)
                halves.append(o2 / den)
            o_ref[rows, pair * LANES:(pair + 1) * LANES] = jnp.where(
                low_half, halves[0], halves[1]).astype(BF16)

    uh = uh_ref[...]
    if history_starts_empty:
        uh = jnp.where(i == 0, 0.0, uh)
    u = u_ref[...]
    ue = jnp.concatenate([uh, u], axis=0)
    s2 = ue + pltpu.roll(ue, 1, 0)
    s4 = s2 + pltpu.roll(s2, 2, 0)
    s8 = s4 + pltpu.roll(s4, 4, 0)
    s16 = s8 + pltpu.roll(s8, 8, 0)
    lane = lax.broadcasted_iota(jnp.int32, (1, POOL_DIM), 1)
    g0, g1, g2 = lane < POOL_CH, lane < 2 * POOL_CH, lane < 3 * POOL_CH
    wsum = jnp.where(g0, s2[POOL_HALO:], jnp.where(g1, s4[POOL_HALO:],
                     jnp.where(g2, s8[POOL_HALO:], s16[POOL_HALO:])))
    width = jnp.where(g0, 2.0, jnp.where(g1, 4.0, jnp.where(g2, 8.0, 16.0))).astype(F32)
    pos = (pos0 + i * t + lax.broadcasted_iota(jnp.int32, (t, 1), 0)).astype(F32)
    cnt = jnp.minimum(width, pos + 1.0)
    pooled = wsum / cnt - u
    mixed = jnp.dot(pooled.astype(BF16), wpool_ref[...], preferred_element_type=F32)
    o_ref[:, ATT_DIM:] = (mixed * pscale_ref[...]).astype(BF16)


def _mixer(sinks, q, kd, vd, u, k_hist, v_hist, u_hist, bias, wpool, pscale, *,
           tile, tq, history_starts_empty, pos0):
    nb, s, _ = q.shape
    nt = s // tile
    hk = WINDOW
    own_history = k_hist is None
    if own_history:
        k_hist, v_hist, u_hist = kd, vd, u
        kh_map = lambda bi, i: (bi, jnp.maximum(i * (tile // hk) - 1, 0), 0)
        uh_map = lambda bi, i: (bi, jnp.maximum(i * (tile // POOL_HALO) - 1, 0), 0)
    else:
        kh_map = lambda bi, i: (bi, 0, 0)
        uh_map = lambda bi, i: (bi, 0, 0)

    def tok(c):
        return pl.BlockSpec((None, tile, c), lambda bi, i: (bi, i, 0))

    kern = functools.partial(_mixer_kernel, tq=tq,
                             history_starts_empty=history_starts_empty, pos0=pos0)
    return pl.pallas_call(
        kern,
        grid=(nb, nt),
        in_specs=[
            pl.BlockSpec(memory_space=pltpu.SMEM),
            tok(ATT_DIM),
            tok(KD_DIM),
            pl.BlockSpec((None, hk, KD_DIM), kh_map),
            tok(KD_DIM),
            pl.BlockSpec((None, hk, KD_DIM), kh_map),
            tok(POOL_DIM),
            pl.BlockSpec((None, POOL_HALO, POOL_DIM), uh_map),
            pl.BlockSpec(bias.shape, lambda bi, i: (0, 0, 0)),
            pl.BlockSpec((POOL_DIM, POOL_DIM), lambda bi, i: (0, 0)),
            pl.BlockSpec((1, POOL_DIM), lambda bi, i: (0, 0)),
        ],
        out_specs=tok(D_MODEL),
        out_shape=jax.ShapeDtypeStruct((nb, s, D_MODEL), BF16),
        scratch_shapes=[pltpu.VMEM((hk + tile, KD_DIM), BF16),
                        pltpu.VMEM((hk + tile, KD_DIM), BF16)],
        compiler_params=_params(("parallel", "parallel")),
        name="mixer",
    )(sinks, q, kd, k_hist, vd, v_hist, u, u_hist, bias, wpool, pscale)


def _post_kernel(mix_ref, x_ref, mod_ref, wout_ref, bout_ref, g1_ref, b1_ref, wr_ref, br_ref,
                 cin_ref, x1_ref, h2_ref, ri_ref, rg_ref, cnt_ref):
    first = (pl.program_id(0) == 0) & (pl.program_id(1) == 0)

    @pl.when(first)
    def _():
        cnt_ref[...] = cin_ref[...]

    t = x_ref.shape[0]
    mix = jnp.dot(mix_ref[...], wout_ref[...], preferred_element_type=F32) + bout_ref[...]
    z = DEEPNORM_ALPHA * x_ref[...] + mod_ref[2:3, :] * mix
    x1 = _layer_norm(z) * g1_ref[...] + b1_ref[...]
    x1_ref[...] = x1
    h2 = _layer_norm(x1) * (1.0 + mod_ref[4:5, :]) + mod_ref[3:4, :]
    h2_ref[...] = h2

    logits = jnp.dot(h2, wr_ref[...], preferred_element_type=F32,
                     precision=lax.Precision.HIGHEST) + br_ref[...]
    lane = lax.broadcasted_iota(jnp.int32, (t, LANES), 1)
    lane_f = lane.astype(F32)
    vals, idxs, hots = [], [], []
    cur = logits
    for _ in range(TOP_K):
        mk = jnp.max(cur, axis=1, keepdims=True)
        ik = jnp.min(jnp.where(cur == mk, lane_f, float(LANES)), axis=1, keepdims=True)
        hot = lane_f == ik
        cur = jnp.where(hot, -jnp.inf, cur)
        vals.append(mk)
        idxs.append(ik)
        hots.append(hot)
    exps = [jnp.exp(v - vals[0]) for v in vals]
    den = exps[0] + exps[1] + exps[2] + exps[3]
    gates = [e / den for e in exps]

    onehot = jnp.zeros((t, LANES), F32)
    for hot in hots:
        onehot = onehot + hot.astype(F32)
    row = lax.broadcasted_iota(jnp.int32, (t, t), 0)
    colt = lax.broadcasted_iota(jnp.int32, (t, t), 1)
    earlier = jnp.where(row > colt, 1.0, 0.0).astype(BF16)
    before = jnp.dot(earlier, onehot.astype(BF16), preferred_element_type=F32) + cnt_ref[0:1, :]
    ranks = [jnp.sum(jnp.where(hot, before, 0.0), axis=1, keepdims=True) for hot in hots]
    cnt_ref[0:1, :] = cnt_ref[0:1, :] + jnp.sum(onehot, axis=0, keepdims=True)

    ri = jnp.zeros((t, LANES), F32)
    rg = jnp.zeros((t, LANES), F32)
    for k in range(TOP_K):
        ri = jnp.where(lane == k, idxs[k], ri)
        ri = jnp.where(lane == TOP_K + k, ranks[k], ri)
        rg = jnp.where(lane == k, gates[k], rg)
    ri_ref[...] = ri.astype(jnp.int32)
    rg_ref[...] = rg


def _post(mix, x, mod, wout, bout, g1, b1, wr, br, counts_in, tile, mod_row0):
    nb, s, _ = x.shape
    nt = s // tile

    def tok(c):
        return pl.BlockSpec((None, tile, c), lambda bi, i: (bi, i, 0))

    def whole(shape):
        return pl.BlockSpec(shape, lambda bi, i: tuple(0 for _ in shape))

    return pl.pallas_call(
        _post_kernel,
        grid=(nb, nt),
        in_specs=[
            tok(D_MODEL), tok(D_MODEL),
            pl.BlockSpec((None, 6, D_MODEL), lambda bi, i: (bi + mod_row0, 0, 0)),
            whole((D_MODEL, D_MODEL)), whole((1, D_MODEL)),
            whole((1, D_MODEL)), whole((1, D_MODEL)),
            whole((D_MODEL, LANES)), whole((1, LANES)),
            whole((SUBLANES, LANES)),
        ],
        out_specs=[tok(D_MODEL), tok(D_MODEL), tok(LANES), tok(LANES),
                   whole((SUBLANES, LANES))],
        out_shape=[
            jax.ShapeDtypeStruct((nb, s, D_MODEL), F32),
            jax.ShapeDtypeStruct((nb, s, D_MODEL), F32),
            jax.ShapeDtypeStruct((nb, s, LANES), jnp.int32),
            jax.ShapeDtypeStruct((nb, s, LANES), F32),
            jax.ShapeDtypeStruct((SUBLANES, LANES), F32),
        ],
        compiler_params=_params(("arbitrary", "arbitrary")),
        name="post",
    )(mix, x, mod, wout, bout, g1, b1, wr, br, counts_in)


def _dispatch_kernel(dest_ref, h_ref, xs_in_ref, xs_ref, sem):
    del xs_in_ref
    t = h_ref.shape[0]

    def row_copy(tok, slot):
        return pltpu.make_async_copy(h_ref.at[pl.ds(tok, 1), :], xs_ref.at[pl.ds(slot, 1), :], sem)

    def body(tok, carry):
        for k in range(TOP_K):
            row_copy(tok, dest_ref[tok * TOP_K + k]).start()
        return carry

    lax.fori_loop(0, t, body, 0)
    for _ in range(TOP_K):
        pltpu.make_async_copy(h_ref, xs_ref.at[pl.ds(0, t), :], sem).wait()


def _dispatch(dest, h, xs, tile):
    n = h.shape[0]
    return pl.pallas_call(
        _dispatch_kernel,
        grid=(n // tile,),
        in_specs=[
            pl.BlockSpec((tile * TOP_K,), lambda i: (i,), memory_space=pltpu.SMEM),
            pl.BlockSpec((tile, D_MODEL), lambda i: (i, 0)),
            pl.BlockSpec(memory_space=pl.ANY),
        ],
        out_specs=pl.BlockSpec(memory_space=pl.ANY),
        out_shape=jax.ShapeDtypeStruct(xs.shape, xs.dtype),
        scratch_shapes=[pltpu.SemaphoreType.DMA],
        input_output_aliases={2: 0},
        compiler_params=_params(("arbitrary",)),
        name="dispatch",
    )(dest, h, xs)


def _moe_kernel(be_ref, nv_ref, xs_ref, wgu_ref, bgu_ref, wdn_ref, bdn_ref, ys_ref,
                wgu_bf, wdn_bf):
    i = pl.program_id(0)
    e = be_ref[i]
    prev = be_ref[jnp.maximum(i - 1, 0)]
    rows = 128

    @pl.when((i == 0) | (e != prev))
    def _():
        def cast(r, carry):
            sl = pl.ds(pl.multiple_of(r * rows, rows), rows)
            wgu_bf[sl, :] = wgu_ref[sl, :].astype(BF16)
            wdn_bf[sl, :] = wdn_ref[sl, :].astype(BF16)
            return carry
        lax.fori_loop(0, D_MODEL // rows, cast, 0)

    @pl.when(i < nv_ref[0])
    def _():
        x = xs_ref[...].astype(BF16)
        gu = jnp.dot(x, wgu_bf[...], preferred_element_type=F32) + bgu_ref[...]
        gate = jnp.minimum(gu[:, :D_FF], SWIGLU_LIMIT)
        up = jnp.clip(gu[:, D_FF:], -SWIGLU_LIMIT, SWIGLU_LIMIT)
        a = gate * jax.nn.sigmoid(SWIGLU_ALPHA * gate) * (up + 1.0)
        ys_ref[...] = jnp.dot(a.astype(BF16), wdn_bf[...],
                              preferred_element_type=F32) + bdn_ref[...]

    @pl.when(i >= nv_ref[0])
    def _():
        ys_ref[...] = jnp.zeros_like(ys_ref)


def _moe(block_e, n_valid, xs, w_gu, b_gu, w_down, b_down):
    n_rows = xs.shape[0]
    n_blocks = n_rows // MOE_ROWS
    grid_spec = pltpu.PrefetchScalarGridSpec(
        num_scalar_prefetch=2,
        grid=(n_blocks,),
        in_specs=[
            pl.BlockSpec((MOE_ROWS, D_MODEL), lambda i, be, nv: (i, 0)),
            pl.BlockSpec((None, D_MODEL, 2 * D_FF), lambda i, be, nv: (be[i], 0, 0)),
            pl.BlockSpec((None, 1, 2 * D_FF), lambda i, be, nv: (be[i], 0, 0)),
            pl.BlockSpec((None, D_FF, D_MODEL), lambda i, be, nv: (be[i], 0, 0)),
            pl.BlockSpec((None, 1, D_MODEL), lambda i, be, nv: (be[i], 0, 0)),
        ],
        out_specs=pl.BlockSpec((MOE_ROWS, D_MODEL), lambda i, be, nv: (i, 0)),
        scratch_shapes=[pltpu.VMEM((D_MODEL, 2 * D_FF), BF16),
                        pltpu.VMEM((D_FF, D_MODEL), BF16)],
    )
    return pl.pallas_call(
        _moe_kernel,
        grid_spec=grid_spec,
        out_shape=jax.ShapeDtypeStruct((n_rows, D_MODEL), F32),
        compiler_params=_params(("arbitrary",)),
        name="moe",
    )(block_e, n_valid, xs, w_gu, b_gu.reshape(N_EXPERTS, 1, 2 * D_FF),
      w_down, b_down.reshape(N_EXPERTS, 1, D_MODEL))


def _combine_kernel(dest_ref, rg_ref, x1_ref, gf_ref, g2_ref, b2_ref, ys_ref, o_ref, buf, sem):
    t = x1_ref.shape[0]

    def body(tok, carry):
        for k in range(TOP_K):
            slot = dest_ref[tok * TOP_K + k]
            pltpu.make_async_copy(ys_ref.at[pl.ds(slot, 1), :],
                                  buf.at[k, pl.ds(tok, 1), :], sem).start()
        return carry

    lax.fori_loop(0, t, body, 0)
    for k in range(TOP_K):
        pltpu.make_async_copy(ys_ref.at[pl.ds(0, t), :], buf.at[k], sem).wait()

    rg = rg_ref[...]
    f = rg[:, 0:1] * buf[0]
    for k in range(1, TOP_K):
        f = f + rg[:, k:k + 1] * buf[k]
    z = DEEPNORM_ALPHA * x1_ref[...] + gf_ref[...] * f
    o_ref[...] = _layer_norm(z) * g2_ref[...] + b2_ref[...]


def _combine(dest, rg, x1, gf, gf_spec, g2, b2, ys, tile):
    n = x1.shape[0]
    return pl.pallas_call(
        _combine_kernel,
        grid=(n // tile,),
        in_specs=[
            pl.BlockSpec((tile * TOP_K,), lambda i: (i,), memory_space=pltpu.SMEM),
            pl.BlockSpec((tile, LANES), lambda i: (i, 0)),
            pl.BlockSpec((tile, D_MODEL), lambda i: (i, 0)),
            gf_spec,
            pl.BlockSpec((1, D_MODEL), lambda i: (0, 0)),
            pl.BlockSpec((1, D_MODEL), lambda i: (0, 0)),
            pl.BlockSpec(memory_space=pl.ANY),
        ],
        out_specs=pl.BlockSpec((tile, D_MODEL), lambda i: (i, 0)),
        out_shape=jax.ShapeDtypeStruct((n, D_MODEL), F32),
        scratch_shapes=[pltpu.VMEM((TOP_K, tile, D_MODEL), F32), pltpu.SemaphoreType.DMA],
        compiler_params=_params(("arbitrary",)),
        name="combine",
    )(dest, rg, x1, gf, g2, b2, ys)


def _alibi_slopes():
    return jnp.exp2(-ALIBI_MAX * (jnp.arange(N_HEADS, dtype=F32) + 1.0) / N_HEADS)


def _prompt_bias(slopes):
    r = jnp.arange(Q_TILE)
    j = jnp.arange(WINDOW + Q_TILE)
    dc = (WINDOW + r)[:, None] // CHUNK - j[None, :] // CHUNK
    vis = (dc >= 0) & (dc <= WIN_CHUNKS)
    dist = jnp.abs(r[:, None] + WINDOW - j[None, :]).astype(F32)
    return jnp.where(vis[None], -slopes[:, None, None] * dist[None], NEG_INF)


def _sample_bias(slopes, t, hist):
    qpos = PAST_LEN + jnp.arange(t)
    kpos = PAST_LEN - hist + jnp.arange(hist + t)
    dc = qpos[:, None] // CHUNK - kpos[None, :] // CHUNK
    vis = (dc >= 0) & (dc <= WIN_CHUNKS) & (kpos[None, :] >= 0)
    dist = jnp.abs(qpos[:, None] - kpos[None, :]).astype(F32)
    return jnp.where(vis[None], -slopes[:, None, None] * dist[None], NEG_INF)


def _dup_heads(w):
    lead = w.shape[:-1]
    w4 = w.reshape(lead + (N_KV, HEAD_DIM))
    return jnp.concatenate([w4, w4], axis=-1).reshape(lead + (KD_DIM,))


def _proj_weights(w_in, b_in):
    q0, k0, v0, u0 = 0, ATT_DIM, ATT_DIM + KV_DIM, ATT_DIM + 2 * KV_DIM
    w = jnp.concatenate([w_in[:, q0:k0] * ATTN_SCALE, _dup_heads(w_in[:, k0:v0]),
                         _dup_heads(w_in[:, v0:u0]), w_in[:, u0:]], axis=1)
    b = jnp.concatenate([b_in[q0:k0] * ATTN_SCALE, _dup_heads(b_in[k0:v0]),
                         _dup_heads(b_in[v0:u0]), b_in[u0:]], axis=0)
    return w.astype(BF16), b.reshape(1, PROJ_COLS)


def _pool_weight(w_pool):
    n = len(POOL_WINDOWS)
    eye = jnp.eye(n, dtype=w_pool.dtype)
    return jnp.einsum('gcd,gh->gchd', w_pool, eye).reshape(POOL_DIM, POOL_DIM).astype(BF16)


def _undup(kd, nb, rows):
    return kd[:, -rows:].reshape(nb, rows, N_KV, LANES)[..., :HEAD_DIM].astype(F32)


def kernel(x_prompt, x_sample, c_prompt, c_sample, cache_k, cache_v, state_pool, w_ada, b_ada,
           w_in, b_in, sinks, w_pool, pool_scale, w_out, b_out, ln1_g, ln1_b, ln2_g, ln2_b,
           w_router, b_router, w_gu, b_gu, w_down, b_down):
    nbp, seq, _ = x_prompt.shape
    nbs, tdec, _ = x_sample.shape
    hist = cache_k.shape[2]
    n_p, n_s = nbp * seq, nbs * tdec
    n_asg = (n_p + n_s) * TOP_K
    n_blocks = -(-(n_asg + N_EXPERTS * (MOE_ROWS - 1)) // MOE_ROWS)
    n_rows = n_blocks * MOE_ROWS
    assert hist == WINDOW and tdec >= POOL_PAD and seq % TOK_TILE == 0
    assert n_p % DISPATCH_TILE == 0 and n_p % COMBINE_TILE == 0

    nb_all = nbp + nbs
    nb_pad = -(-nb_all // SUBLANES) * SUBLANES
    c_all = jnp.concatenate([c_prompt, c_sample, jnp.zeros((nb_pad - nb_all, D_MODEL), F32)], 0)
    mod_all = _adaln(c_all, w_ada, b_ada).reshape(DEPTH, nb_pad, 6, D_MODEL)

    slopes = _alibi_slopes()
    bias_p = _prompt_bias(slopes)
    bias_s = _sample_bias(slopes, tdec, hist)
    row = lambda v: v.reshape(1, -1)

    yp, ys_tok = x_prompt, x_sample
    outs = {k: [] for k in ("kp", "vp", "pp", "ks", "vs", "ps")}
    for l in range(DEPTH):
        mod = mod_all[l]
        w_proj, b_proj = _proj_weights(w_in[l], b_in[l])
        wpool = _pool_weight(w_pool[l])
        pscale = row(pool_scale[l])
        wout = w_out[l].astype(BF16)
        wr = jnp.pad(w_router[l], ((0, 0), (0, LANES - N_EXPERTS)))
        br = jnp.concatenate([b_router[l], jnp.full((LANES - N_EXPERTS,), NEG_INF, F32)]).reshape(1, LANES)

        q, kd, vd, u = _inproj(yp, mod, w_proj, b_proj, TOK_TILE, 0)
        mix_p = _mixer(sinks[l], q, kd, vd, u, None, None, None, bias_p, wpool, pscale,
                       tile=TOK_TILE, tq=Q_TILE, history_starts_empty=True, pos0=0)
        outs["kp"].append(_undup(kd, nbp, WINDOW))
        outs["vp"].append(_undup(vd, nbp, WINDOW))
        outs["pp"].append(u[:, -POOL_PAD:])

        qs, kds, vds, us = _inproj(ys_tok, mod, w_proj, b_proj, tdec, nbp)
        ck = _dup_heads(cache_k[l].reshape(nbs, hist, KV_DIM)).astype(BF16)
        cv = _dup_heads(cache_v[l].reshape(nbs, hist, KV_DIM)).astype(BF16)
        sp = jnp.pad(state_pool[l], ((0, 0), (POOL_HALO - POOL_PAD, 0), (0, 0)))
        mix_s = _mixer(sinks[l], qs, kds, vds, us, ck, cv, sp, bias_s, wpool, pscale,
                       tile=tdec, tq=tdec, history_starts_empty=False, pos0=PAST_LEN)
        k_new = _undup(kds, nbs, tdec)
        v_new = _undup(vds, nbs, tdec)
        outs["ks"].append(jnp.concatenate([cache_k[l], k_new], axis=1)[:, -hist:])
        outs["vs"].append(jnp.concatenate([cache_v[l], v_new], axis=1)[:, -hist:])
        outs["ps"].append(us[:, -POOL_PAD:])

        post_w = (wout, row(b_out[l]), row(ln1_g[l]), row(ln1_b[l]), wr, br)
        zero_counts = jnp.zeros((SUBLANES, LANES), F32)
        x1p, h2p, rip, rgp, cnt_p = _post(mix_p, yp, mod, *post_w, zero_counts, TOK_TILE, 0)
        x1s, h2s, ris, rgs, cnt = _post(mix_s, ys_tok, mod, *post_w, cnt_p, tdec, nbp)

        counts = cnt[0, :N_EXPERTS].astype(jnp.int32)
        padded = (counts + MOE_ROWS - 1) // MOE_ROWS * MOE_ROWS
        pad_ends = jnp.cumsum(padded)
        pad_starts = pad_ends - padded
        block_start = jnp.arange(n_blocks, dtype=jnp.int32) * MOE_ROWS
        block_e = jnp.minimum(jnp.searchsorted(pad_ends, block_start, side='right'),
                              N_EXPERTS - 1).astype(jnp.int32)
        n_valid = (pad_ends[-1:] // MOE_ROWS).astype(jnp.int32)

        def slots(ri, n):
            ri = ri.reshape(n, LANES)
            return (pad_starts[ri[:, :TOP_K]] + ri[:, TOP_K:2 * TOP_K]).reshape(n * TOP_K)

        dest_p, dest_s = slots(rip, n_p), slots(ris, n_s)

        xs = jnp.zeros((n_rows, D_MODEL), F32)
        xs = _dispatch(dest_p, h2p.reshape(n_p, D_MODEL), xs, DISPATCH_TILE)
        xs = _dispatch(dest_s, h2s.reshape(n_s, D_MODEL), xs, n_s)
        ye = _moe(block_e, n_valid, xs, w_gu[l], b_gu[l], w_down[l], b_down[l])

        g2, b2 = row(ln2_g[l]), row(ln2_b[l])
        gf_p = mod[:nbp, 5].reshape(nbp, 1, D_MODEL)
        tiles_per_seq = seq // COMBINE_TILE
        gf_p_spec = pl.BlockSpec((None, 1, D_MODEL), lambda i: (i // tiles_per_seq, 0, 0))
        yp = _combine(dest_p, rgp.reshape(n_p, LANES), x1p.reshape(n_p, D_MODEL), gf_p, gf_p_spec,
                      g2, b2, ye, COMBINE_TILE).reshape(nbp, seq, D_MODEL)
        gf_s = jnp.repeat(mod[nbp:nb_all, 5], tdec, axis=0)
        gf_s_spec = pl.BlockSpec((n_s, D_MODEL), lambda i: (0, 0))
        ys_tok = _combine(dest_s, rgs.reshape(n_s, LANES), x1s.reshape(n_s, D_MODEL), gf_s,
                          gf_s_spec, g2, b2, ye, n_s).reshape(nbs, tdec, D_MODEL)

    st = lambda k: jnp.stack(outs[k])
    return (yp, ys_tok, st("kp"), st("vp"), st("pp"), st("ks"), st("vs"), st("ps"))
```

```python
import functools

import jax
import jax.numpy as jnp
from jax import lax
from jax.experimental import pallas as pl
from jax.experimental.pallas import tpu as pltpu

F32 = jnp.float32
BF16 = jnp.bfloat16

D_MODEL = 1024
DEPTH = 2
CHUNK = 64
WINDOW = 128
WIN_CHUNKS = WINDOW // CHUNK
HEAD_DIM = 64
ATT_DIM = 768
N_HEADS = 12
N_KV = 4
GROUP = N_HEADS // N_KV
KV_DIM = N_KV * HEAD_DIM
ATTN_SCALE = HEAD_DIM ** -0.5
ALIBI_MAX = 8.0
NEG_INF = -1e30
POOL_DIM = D_MODEL - ATT_DIM
POOL_WINDOWS = (2, 4, 8, 16)
POOL_CH = POOL_DIM // len(POOL_WINDOWS)
POOL_PAD = max(POOL_WINDOWS) - 1
N_EXPERTS = 32
TOP_K = 4
D_FF = D_MODEL
SWIGLU_LIMIT = 7.0
SWIGLU_ALPHA = 1.702
LN_EPS = 1e-5
DEEPNORM_ALPHA = (2.0 * DEPTH) ** 0.25
PAST_LEN = 1024

LANES = 128
SUBLANES = 8
VMEM_LIMIT = 52 * 1024 * 1024

TOK_TILE = 512
Q_TILE = 128
POOL_HALO = 16
KD_DIM = N_KV * LANES
PROJ_COLS = ATT_DIM + 2 * KD_DIM + POOL_DIM
MOE_ROWS = 256
DISPATCH_TILE = 512
COMBINE_TILE = 256


def _params(sem, vmem=VMEM_LIMIT):
    return pltpu.CompilerParams(dimension_semantics=sem, vmem_limit_bytes=vmem)


def _layer_norm(x):
    mu = jnp.mean(x, axis=-1, keepdims=True)
    xc = x - mu
    var = jnp.mean(xc * xc, axis=-1, keepdims=True)
    return xc * lax.rsqrt(var + LN_EPS)


def _adaln_kernel(c_ref, w_ref, b_ref, o_ref):
    c = c_ref[...]
    s = c * jax.nn.sigmoid(c)
    o_ref[...] = jnp.dot(s, w_ref[...], preferred_element_type=F32,
                         precision=lax.Precision.HIGHEST) + b_ref[...]


def _adaln(c_all, w_ada, b_ada):
    nb = c_all.shape[0]
    ncol = 6 * D_MODEL
    tn = D_MODEL
    return pl.pallas_call(
        _adaln_kernel,
        grid=(DEPTH, ncol // tn),
        in_specs=[
            pl.BlockSpec((nb, D_MODEL), lambda l, j: (0, 0)),
            pl.BlockSpec((None, D_MODEL, tn), lambda l, j: (l, 0, j)),
            pl.BlockSpec((None, 1, tn), lambda l, j: (l, 0, j)),
        ],
        out_specs=pl.BlockSpec((None, nb, tn), lambda l, j: (l, 0, j)),
        out_shape=jax.ShapeDtypeStruct((DEPTH, nb, ncol), F32),
        compiler_params=_params(("parallel", "parallel")),
        name="adaln",
    )(c_all, w_ada, b_ada.reshape(DEPTH, 1, ncol))


def _inproj_kernel(x_ref, mod_ref, w_ref, b_ref, q_ref, kd_ref, vd_ref, u_ref):
    h = _layer_norm(x_ref[...]) * (1.0 + mod_ref[1:2, :]) + mod_ref[0:1, :]
    p = jnp.dot(h.astype(BF16), w_ref[...], preferred_element_type=F32) + b_ref[...]
    q_ref[...] = p[:, :ATT_DIM].astype(BF16)
    kd_ref[...] = p[:, ATT_DIM:ATT_DIM + KD_DIM].astype(BF16)
    vd_ref[...] = p[:, ATT_DIM + KD_DIM:ATT_DIM + 2 * KD_DIM].astype(BF16)
    u_ref[...] = p[:, ATT_DIM + 2 * KD_DIM:]


def _inproj(x, mod, w, b, tile, mod_row0):
    nb, s, _ = x.shape
    nt = s // tile

    def tok(c):
        return pl.BlockSpec((None, tile, c), lambda bi, i: (bi, i, 0))

    return pl.pallas_call(
        _inproj_kernel,
        grid=(nb, nt),
        in_specs=[
            tok(D_MODEL),
            pl.BlockSpec((None, 6, D_MODEL), lambda bi, i: (bi + mod_row0, 0, 0)),
            pl.BlockSpec((D_MODEL, PROJ_COLS), lambda bi, i: (0, 0)),
            pl.BlockSpec((1, PROJ_COLS), lambda bi, i: (0, 0)),
        ],
        out_specs=[tok(ATT_DIM), tok(KD_DIM), tok(KD_DIM), tok(POOL_DIM)],
        out_shape=[
            jax.ShapeDtypeStruct((nb, s, ATT_DIM), BF16),
            jax.ShapeDtypeStruct((nb, s, KD_DIM), BF16),
            jax.ShapeDtypeStruct((nb, s, KD_DIM), BF16),
            jax.ShapeDtypeStruct((nb, s, POOL_DIM), F32),
        ],
        compiler_params=_params(("parallel", "parallel")),
        name="inproj",
    )(x, mod, w, b)


def _mixer_kernel(sinks_ref, q_ref, kd_ref, kdh_ref, vd_ref, vdh_ref, u_ref, uh_ref,
                  bias_ref, wpool_ref, pscale_ref, o_ref, kall_ref, vall_ref,
                  *, tq, history_starts_empty, pos0):
    i = pl.program_id(1)
    t = q_ref.shape[0]
    hk = kdh_ref.shape[0]
    nk = hk + tq
    kall_ref[0:hk, :] = kdh_ref[...]
    kall_ref[hk:, :] = kd_ref[...]
    vall_ref[0:hk, :] = vdh_ref[...]
    vall_ref[hk:, :] = vd_ref[...]

    low_half = lax.broadcasted_iota(jnp.int32, (tq, LANES), 1) < HEAD_DIM
    if history_starts_empty:
        col = lax.broadcasted_iota(jnp.int32, (1, nk), 1)
        no_history = jnp.where((col < hk) & (i == 0), NEG_INF, 0.0).astype(F32)

    for s in range(t // tq):
        rows = slice(s * tq, (s + 1) * tq)
        keys = slice(s * tq, s * tq + nk)
        for pair in range(N_HEADS // 2):
            q2 = q_ref[rows, pair * LANES:(pair + 1) * LANES]
            halves = []
            for half in range(2):
                head = 2 * pair + half
                kv = head // GROUP
                qm = jnp.where(low_half if half == 0 else ~low_half, q2, jnp.zeros_like(q2))
                kd = kall_ref[keys, kv * LANES:(kv + 1) * LANES]
                sc = lax.dot_general(qm, kd, (((1,), (1,)), ((), ())),
                                     preferred_element_type=F32)
                sc = sc + bias_ref[head]
                if history_starts_empty and s == 0:
                    sc = sc + no_history
                sink = sinks_ref[head]
                m = jnp.maximum(jnp.max(sc, axis=1, keepdims=True), sink)
                p = jnp.exp(sc - m)
                den = jnp.sum(p, axis=1, keepdims=True) + jnp.exp(sink - m)
                vd = vall_ref[keys, kv * LANES:(kv + 1) * LANES]
                o2 = jnp.dot(p.astype(BF16), vd, preferred_element_type=F32)
                halves.append(o2 / den)
            o_ref[rows, pair * LANES:(pair + 1) * LANES] = jnp.where(
                low_half, halves[0], halves[1]).astype(BF16)

    uh = uh_ref[...]
    if history_starts_empty:
        uh = jnp.where(i == 0, 0.0, uh)
    u = u_ref[...]
    ue = jnp.concatenate([uh, u], axis=0)
    s2 = ue + pltpu.roll(ue, 1, 0)
    s4 = s2 + pltpu.roll(s2, 2, 0)
    s8 = s4 + pltpu.roll(s4, 4, 0)
    s16 = s8 + pltpu.roll(s8, 8, 0)
    lane = lax.broadcasted_iota(jnp.int32, (1, POOL_DIM), 1)
    g0, g1, g2 = lane < POOL_CH, lane < 2 * POOL_CH, lane < 3 * POOL_CH
    wsum = jnp.where(g0, s2[POOL_HALO:], jnp.where(g1, s4[POOL_HALO:],
                     jnp.where(g2, s8[POOL_HALO:], s16[POOL_HALO:])))
    width = jnp.where(g0, 2.0, jnp.where(g1, 4.0, jnp.where(g2, 8.0, 16.0))).astype(F32)
    pos = (pos0 + i * t + lax.broadcasted_iota(jnp.int32, (t, 1), 0)).astype(F32)
    cnt = jnp.minimum(width, pos + 1.0)
    pooled = wsum / cnt - u
    mixed = jnp.dot(pooled.astype(BF16), wpool_ref[...], preferred_element_type=F32)
    o_ref[:, ATT_DIM:] = (mixed * pscale_ref[...]).astype(BF16)


def _mixer(sinks, q, kd, vd, u, k_hist, v_hist, u_hist, bias, wpool, pscale, *,
           tile, tq, history_starts_empty, pos0):
    nb, s, _ = q.shape
    nt = s // tile
    hk = WINDOW
    own_history = k_hist is None
    if own_history:
        k_hist, v_hist, u_hist = kd, vd, u
        kh_map = lambda bi, i: (bi, jnp.maximum(i * (tile // hk) - 1, 0), 0)
        uh_map = lambda bi, i: (bi, jnp.maximum(i * (tile // POOL_HALO) - 1, 0), 0)
    else:
        kh_map = lambda bi, i: (bi, 0, 0)
        uh_map = lambda bi, i: (bi, 0, 0)

    def tok(c):
        return pl.BlockSpec((None, tile, c), lambda bi, i: (bi, i, 0))

    kern = functools.partial(_mixer_kernel, tq=tq,
                             history_starts_empty=history_starts_empty, pos0=pos0)
    return pl.pallas_call(
        kern,
        grid=(nb, nt),
        in_specs=[
            pl.BlockSpec(memory_space=pltpu.SMEM),
            tok(ATT_DIM),
            tok(KD_DIM),
            pl.BlockSpec((None, hk, KD_DIM), kh_map),
            tok(KD_DIM),
            pl.BlockSpec((None, hk, KD_DIM), kh_map),
            tok(POOL_DIM),
            pl.BlockSpec((None, POOL_HALO, POOL_DIM), uh_map),
            pl.BlockSpec(bias.shape, lambda bi, i: (0, 0, 0)),
            pl.BlockSpec((POOL_DIM, POOL_DIM), lambda bi, i: (0, 0)),
            pl.BlockSpec((1, POOL_DIM), lambda bi, i: (0, 0)),
        ],
        out_specs=tok(D_MODEL),
        out_shape=jax.ShapeDtypeStruct((nb, s, D_MODEL), BF16),
        scratch_shapes=[pltpu.VMEM((hk + tile, KD_DIM), BF16),
                        pltpu.VMEM((hk + tile, KD_DIM), BF16)],
        compiler_params=_params(("parallel", "parallel")),
        name="mixer",
    )(sinks, q, kd, k_hist, vd, v_hist, u, u_hist, bias, wpool, pscale)


def _post_kernel(mix_ref, x_ref, mod_ref, wout_ref, bout_ref, g1_ref, b1_ref, wr_ref, br_ref,
                 cin_ref, x1_ref, h2_ref, ri_ref, rg_ref, cnt_ref):
    first = (pl.program_id(0) == 0) & (pl.program_id(1) == 0)

    @pl.when(first)
    def _():
        cnt_ref[...] = cin_ref[...]

    t = x_ref.shape[0]
    mix = jnp.dot(mix_ref[...], wout_ref[...], preferred_element_type=F32) + bout_ref[...]
    z = DEEPNORM_ALPHA * x_ref[...] + mod_ref[2:3, :] * mix
    x1 = _layer_norm(z) * g1_ref[...] + b1_ref[...]
    x1_ref[...] = x1
    h2 = _layer_norm(x1) * (1.0 + mod_ref[4:5, :]) + mod_ref[3:4, :]
    h2_ref[...] = h2

    h2_hi = h2.astype(BF16)
    h2_lo = (h2 - h2_hi.astype(F32)).astype(BF16)
    d_hi = jnp.dot(h2_hi, wr_ref[...], preferred_element_type=F32)
    d_lo = jnp.dot(h2_lo, wr_ref[:, :LANES], preferred_element_type=F32)
    logits = d_hi[:, :LANES] + d_hi[:, LANES:] + d_lo + br_ref[...]
    lane = lax.broadcasted_iota(jnp.int32, (t, LANES), 1)
    lane_f = lane.astype(F32)
    vals, idxs, hots = [], [], []
    cur = logits
    for _ in range(TOP_K):
        mk = jnp.max(cur, axis=1, keepdims=True)
        ik = jnp.min(jnp.where(cur == mk, lane_f, float(LANES)), axis=1, keepdims=True)
        hot = lane_f == ik
        cur = jnp.where(hot, -jnp.inf, cur)
        vals.append(mk)
        idxs.append(ik)
        hots.append(hot)
    exps = [jnp.exp(v - vals[0]) for v in vals]
    den = exps[0] + exps[1] + exps[2] + exps[3]
    gates = [e / den for e in exps]

    onehot = jnp.zeros((t, LANES), F32)
    for hot in hots:
        onehot = onehot + hot.astype(F32)
    row = lax.broadcasted_iota(jnp.int32, (t, t), 0)
    colt = lax.broadcasted_iota(jnp.int32, (t, t), 1)
    earlier = jnp.where(row > colt, 1.0, 0.0).astype(BF16)
    before = jnp.dot(earlier, onehot.astype(BF16), preferred_element_type=F32) + cnt_ref[0:1, :]
    ranks = [jnp.sum(jnp.where(hot, before, 0.0), axis=1, keepdims=True) for hot in hots]
    cnt_ref[0:1, :] = cnt_ref[0:1, :] + jnp.sum(onehot, axis=0, keepdims=True)

    ri = jnp.zeros((t, LANES), F32)
    rg = jnp.zeros((t, LANES), F32)
    for k in range(TOP_K):
        ri = jnp.where(lane == k, idxs[k], ri)
        ri = jnp.where(lane == TOP_K + k, ranks[k], ri)
        rg = jnp.where(lane == k, gates[k], rg)
    ri_ref[...] = ri.astype(jnp.int32)
    rg_ref[...] = rg


def _post(mix, x, mod, wout, bout, g1, b1, wr, br, counts_in, tile, mod_row0):
    nb, s, _ = x.shape
    nt = s // tile

    def tok(c):
        return pl.BlockSpec((None, tile, c), lambda bi, i: (bi, i, 0))

    def whole(shape):
        return pl.BlockSpec(shape, lambda bi, i: tuple(0 for _ in shape))

    return pl.pallas_call(
        _post_kernel,
        grid=(nb, nt),
        in_specs=[
            tok(D_MODEL), tok(D_MODEL),
            pl.BlockSpec((None, 6, D_MODEL), lambda bi, i: (bi + mod_row0, 0, 0)),
            whole((D_MODEL, D_MODEL)), whole((1, D_MODEL)),
            whole((1, D_MODEL)), whole((1, D_MODEL)),
            whole((D_MODEL, 2 * LANES)), whole((1, LANES)),
            whole((SUBLANES, LANES)),
        ],
        out_specs=[tok(D_MODEL), tok(D_MODEL), tok(LANES), tok(LANES),
                   whole((SUBLANES, LANES))],
        out_shape=[
            jax.ShapeDtypeStruct((nb, s, D_MODEL), F32),
            jax.ShapeDtypeStruct((nb, s, D_MODEL), F32),
            jax.ShapeDtypeStruct((nb, s, LANES), jnp.int32),
            jax.ShapeDtypeStruct((nb, s, LANES), F32),
            jax.ShapeDtypeStruct((SUBLANES, LANES), F32),
        ],
        compiler_params=_params(("arbitrary", "arbitrary")),
        name="post",
    )(mix, x, mod, wout, bout, g1, b1, wr, br, counts_in)


def _dispatch_kernel(dest_ref, h_ref, xs_in_ref, xs_ref, sem):
    del xs_in_ref
    t = h_ref.shape[0]

    def row_copy(tok, slot):
        return pltpu.make_async_copy(h_ref.at[pl.ds(tok, 1), :], xs_ref.at[pl.ds(slot, 1), :], sem)

    def body(tok, carry):
        for k in range(TOP_K):
            row_copy(tok, dest_ref[tok * TOP_K + k]).start()
        return carry

    lax.fori_loop(0, t, body, 0)
    for _ in range(TOP_K):
        pltpu.make_async_copy(h_ref, xs_ref.at[pl.ds(0, t), :], sem).wait()


def _dispatch(dest, h, xs, tile):
    n = h.shape[0]
    return pl.pallas_call(
        _dispatch_kernel,
        grid=(n // tile,),
        in_specs=[
            pl.BlockSpec((tile * TOP_K,), lambda i: (i,), memory_space=pltpu.SMEM),
            pl.BlockSpec((tile, D_MODEL), lambda i: (i, 0)),
            pl.BlockSpec(memory_space=pl.ANY),
        ],
        out_specs=pl.BlockSpec(memory_space=pl.ANY),
        out_shape=jax.ShapeDtypeStruct(xs.shape, xs.dtype),
        scratch_shapes=[pltpu.SemaphoreType.DMA],
        input_output_aliases={2: 0},
        compiler_params=_params(("arbitrary",)),
        name="dispatch",
    )(dest, h, xs)


def _moe_kernel(be_ref, nv_ref, xs_ref, wgu_ref, bgu_ref, wdn_ref, bdn_ref, ys_ref,
                wgu_bf, wdn_bf):
    i = pl.program_id(0)
    e = be_ref[i]
    prev = be_ref[jnp.maximum(i - 1, 0)]
    rows = 128

    @pl.when((i == 0) | (e != prev))
    def _():
        def cast(r, carry):
            sl = pl.ds(pl.multiple_of(r * rows, rows), rows)
            wgu_bf[sl, :] = wgu_ref[sl, :].astype(BF16)
            wdn_bf[sl, :] = wdn_ref[sl, :].astype(BF16)
            return carry
        lax.fori_loop(0, D_MODEL // rows, cast, 0)

    @pl.when(i < nv_ref[0])
    def _():
        x = xs_ref[...].astype(BF16)
        gu = jnp.dot(x, wgu_bf[...], preferred_element_type=F32) + bgu_ref[...]
        gate = jnp.minimum(gu[:, :D_FF], SWIGLU_LIMIT)
        up = jnp.clip(gu[:, D_FF:], -SWIGLU_LIMIT, SWIGLU_LIMIT)
        a = gate * jax.nn.sigmoid(SWIGLU_ALPHA * gate) * (up + 1.0)
        ys_ref[...] = jnp.dot(a.astype(BF16), wdn_bf[...],
                              preferred_element_type=F32) + bdn_ref[...]

    @pl.when(i >= nv_ref[0])
    def _():
        ys_ref[...] = jnp.zeros_like(ys_ref)


def _moe(block_e, n_valid, xs, w_gu, b_gu, w_down, b_down, layer):
    n_rows = xs.shape[0]
    n_blocks = n_rows // MOE_ROWS
    grid_spec = pltpu.PrefetchScalarGridSpec(
        num_scalar_prefetch=2,
        grid=(n_blocks,),
        in_specs=[
            pl.BlockSpec((MOE_ROWS, D_MODEL), lambda i, be, nv: (i, 0)),
            pl.BlockSpec((None, None, D_MODEL, 2 * D_FF), lambda i, be, nv: (layer, be[i], 0, 0)),
            pl.BlockSpec((None, None, 1, 2 * D_FF), lambda i, be, nv: (layer, be[i], 0, 0)),
            pl.BlockSpec((None, None, D_FF, D_MODEL), lambda i, be, nv: (layer, be[i], 0, 0)),
            pl.BlockSpec((None, None, 1, D_MODEL), lambda i, be, nv: (layer, be[i], 0, 0)),
        ],
        out_specs=pl.BlockSpec((MOE_ROWS, D_MODEL), lambda i, be, nv: (i, 0)),
        scratch_shapes=[pltpu.VMEM((D_MODEL, 2 * D_FF), BF16),
                        pltpu.VMEM((D_FF, D_MODEL), BF16)],
    )
    return pl.pallas_call(
        _moe_kernel,
        grid_spec=grid_spec,
        out_shape=jax.ShapeDtypeStruct((n_rows, D_MODEL), F32),
        compiler_params=_params(("arbitrary",)),
        name="moe",
    )(block_e, n_valid, xs, w_gu, b_gu.reshape(DEPTH, N_EXPERTS, 1, 2 * D_FF),
      w_down, b_down.reshape(DEPTH, N_EXPERTS, 1, D_MODEL))


def _combine_kernel(dest_ref, rg_ref, x1_ref, gf_ref, g2_ref, b2_ref, ys_ref, o_ref, buf, sem):
    t = x1_ref.shape[0]

    def body(tok, carry):
        for k in range(TOP_K):
            slot = dest_ref[tok * TOP_K + k]
            pltpu.make_async_copy(ys_ref.at[pl.ds(slot, 1), :],
                                  buf.at[k, pl.ds(tok, 1), :], sem).start()
        return carry

    lax.fori_loop(0, t, body, 0)
    for k in range(TOP_K):
        pltpu.make_async_copy(ys_ref.at[pl.ds(0, t), :], buf.at[k], sem).wait()

    rg = rg_ref[...]
    f = rg[:, 0:1] * buf[0]
    for k in range(1, TOP_K):
        f = f + rg[:, k:k + 1] * buf[k]
    z = DEEPNORM_ALPHA * x1_ref[...] + gf_ref[...] * f
    o_ref[...] = _layer_norm(z) * g2_ref[...] + b2_ref[...]


def _combine(dest, rg, x1, gf, gf_spec, g2, b2, ys, tile):
    n = x1.shape[0]
    return pl.pallas_call(
        _combine_kernel,
        grid=(n // tile,),
        in_specs=[
            pl.BlockSpec((tile * TOP_K,), lambda i: (i,), memory_space=pltpu.SMEM),
            pl.BlockSpec((tile, LANES), lambda i: (i, 0)),
            pl.BlockSpec((tile, D_MODEL), lambda i: (i, 0)),
            gf_spec,
            pl.BlockSpec((1, D_MODEL), lambda i: (0, 0)),
            pl.BlockSpec((1, D_MODEL), lambda i: (0, 0)),
            pl.BlockSpec(memory_space=pl.ANY),
        ],
        out_specs=pl.BlockSpec((tile, D_MODEL), lambda i: (i, 0)),
        out_shape=jax.ShapeDtypeStruct((n, D_MODEL), F32),
        scratch_shapes=[pltpu.VMEM((TOP_K, tile, D_MODEL), F32), pltpu.SemaphoreType.DMA],
        compiler_params=_params(("arbitrary",)),
        name="combine",
    )(dest, rg, x1, gf, g2, b2, ys)


def _alibi_slopes():
    return jnp.exp2(-ALIBI_MAX * (jnp.arange(N_HEADS, dtype=F32) + 1.0) / N_HEADS)


def _prompt_bias(slopes):
    r = jnp.arange(Q_TILE)
    j = jnp.arange(WINDOW + Q_TILE)
    dc = (WINDOW + r)[:, None] // CHUNK - j[None, :] // CHUNK
    vis = (dc >= 0) & (dc <= WIN_CHUNKS)
    dist = jnp.abs(r[:, None] + WINDOW - j[None, :]).astype(F32)
    return jnp.where(vis[None], -slopes[:, None, None] * dist[None], NEG_INF)


def _sample_bias(slopes, t, hist):
    qpos = PAST_LEN + jnp.arange(t)
    kpos = PAST_LEN - hist + jnp.arange(hist + t)
    dc = qpos[:, None] // CHUNK - kpos[None, :] // CHUNK
    vis = (dc >= 0) & (dc <= WIN_CHUNKS) & (kpos[None, :] >= 0)
    dist = jnp.abs(qpos[:, None] - kpos[None, :]).astype(F32)
    return jnp.where(vis[None], -slopes[:, None, None] * dist[None], NEG_INF)


def _dup_heads(w):
    lead = w.shape[:-1]
    w4 = w.reshape(lead + (N_KV, HEAD_DIM))
    return jnp.concatenate([w4, w4], axis=-1).reshape(lead + (KD_DIM,))


def _proj_weights(w_in, b_in):
    q0, k0, v0, u0 = 0, ATT_DIM, ATT_DIM + KV_DIM, ATT_DIM + 2 * KV_DIM
    w = jnp.concatenate([w_in[:, q0:k0] * ATTN_SCALE, _dup_heads(w_in[:, k0:v0]),
                         _dup_heads(w_in[:, v0:u0]), w_in[:, u0:]], axis=1)
    b = jnp.concatenate([b_in[q0:k0] * ATTN_SCALE, _dup_heads(b_in[k0:v0]),
                         _dup_heads(b_in[v0:u0]), b_in[u0:]], axis=0)
    return w.astype(BF16), b.reshape(1, PROJ_COLS)


def _pool_weight(w_pool):
    n = len(POOL_WINDOWS)
    eye = jnp.eye(n, dtype=w_pool.dtype)
    return jnp.einsum('gcd,gh->gchd', w_pool, eye).reshape(POOL_DIM, POOL_DIM).astype(BF16)


def _undup(kd, nb, rows):
    return kd[:, -rows:].reshape(nb, rows, N_KV, LANES)[..., :HEAD_DIM].astype(F32)


def kernel(x_prompt, x_sample, c_prompt, c_sample, cache_k, cache_v, state_pool, w_ada, b_ada,
           w_in, b_in, sinks, w_pool, pool_scale, w_out, b_out, ln1_g, ln1_b, ln2_g, ln2_b,
           w_router, b_router, w_gu, b_gu, w_down, b_down):
    nbp, seq, _ = x_prompt.shape
    nbs, tdec, _ = x_sample.shape
    hist = cache_k.shape[2]
    n_p, n_s = nbp * seq, nbs * tdec
    n_asg = (n_p + n_s) * TOP_K
    n_blocks = -(-(n_asg + N_EXPERTS * (MOE_ROWS - 1)) // MOE_ROWS)
    n_rows = n_blocks * MOE_ROWS
    assert hist == WINDOW and tdec >= POOL_PAD and seq % TOK_TILE == 0
    assert n_p % DISPATCH_TILE == 0 and n_p % COMBINE_TILE == 0

    nb_all = nbp + nbs
    nb_pad = -(-nb_all // SUBLANES) * SUBLANES
    c_all = jnp.concatenate([c_prompt, c_sample, jnp.zeros((nb_pad - nb_all, D_MODEL), F32)], 0)
    mod_all = _adaln(c_all, w_ada, b_ada).reshape(DEPTH, nb_pad, 6, D_MODEL)

    slopes = _alibi_slopes()
    bias_p = _prompt_bias(slopes)
    bias_s = _sample_bias(slopes, tdec, hist)
    row = lambda v: v.reshape(1, -1)

    yp, ys_tok = x_prompt, x_sample
    outs = {k: [] for k in ("kp", "vp", "pp", "ks", "vs", "ps")}
    for l in range(DEPTH):
        mod = mod_all[l]
        w_proj, b_proj = _proj_weights(w_in[l], b_in[l])
        wpool = _pool_weight(w_pool[l])
        pscale = row(pool_scale[l])
        wout = w_out[l].astype(BF16)
        wr_f32 = jnp.pad(w_router[l], ((0, 0), (0, LANES - N_EXPERTS)))
        wr_hi = wr_f32.astype(BF16)
        wr = jnp.concatenate([wr_hi, (wr_f32 - wr_hi.astype(F32)).astype(BF16)], axis=1)
        br = jnp.concatenate([b_router[l], jnp.full((LANES - N_EXPERTS,), NEG_INF, F32)]).reshape(1, LANES)

        q, kd, vd, u = _inproj(yp, mod, w_proj, b_proj, TOK_TILE, 0)
        mix_p = _mixer(sinks[l], q, kd, vd, u, None, None, None, bias_p, wpool, pscale,
                       tile=TOK_TILE, tq=Q_TILE, history_starts_empty=True, pos0=0)
        outs["kp"].append(_undup(kd, nbp, WINDOW))
        outs["vp"].append(_undup(vd, nbp, WINDOW))
        outs["pp"].append(u[:, -POOL_PAD:])

        qs, kds, vds, us = _inproj(ys_tok, mod, w_proj, b_proj, tdec, nbp)
        ck = _dup_heads(cache_k[l].reshape(nbs, hist, KV_DIM)).astype(BF16)
        cv = _dup_heads(cache_v[l].reshape(nbs, hist, KV_DIM)).astype(BF16)
        sp = jnp.pad(state_pool[l], ((0, 0), (POOL_HALO - POOL_PAD, 0), (0, 0)))
        mix_s = _mixer(sinks[l], qs, kds, vds, us, ck, cv, sp, bias_s, wpool, pscale,
                       tile=tdec, tq=tdec, history_starts_empty=False, pos0=PAST_LEN)
        k_new = _undup(kds, nbs, tdec)
        v_new = _undup(vds, nbs, tdec)
        outs["ks"].append(jnp.concatenate([cache_k[l], k_new], axis=1)[:, -hist:])
        outs["vs"].append(jnp.concatenate([cache_v[l], v_new], axis=1)[:, -hist:])
        outs["ps"].append(us[:, -POOL_PAD:])

        post_w = (wout, row(b_out[l]), row(ln1_g[l]), row(ln1_b[l]), wr, br)
        zero_counts = jnp.zeros((SUBLANES, LANES), F32)
        x1p, h2p, rip, rgp, cnt_p = _post(mix_p, yp, mod, *post_w, zero_counts, TOK_TILE, 0)
        x1s, h2s, ris, rgs, cnt = _post(mix_s, ys_tok, mod, *post_w, cnt_p, tdec, nbp)

        counts = cnt[0, :N_EXPERTS].astype(jnp.int32)
        padded = (counts + MOE_ROWS - 1) // MOE_ROWS * MOE_ROWS
        pad_ends = jnp.cumsum(padded)
        pad_starts = pad_ends - padded
        block_start = jnp.arange(n_blocks, dtype=jnp.int32) * MOE_ROWS
        block_e = jnp.minimum(jnp.sum(pad_ends[None, :] <= block_start[:, None], axis=1),
                              N_EXPERTS - 1).astype(jnp.int32)
        n_valid = (pad_ends[-1:] // MOE_ROWS).astype(jnp.int32)

        def slots(ri, n):
            ri = ri.reshape(n, LANES)
            is_e = ri[:, :TOP_K, None] == jnp.arange(N_EXPERTS, dtype=jnp.int32)
            start = jnp.sum(jnp.where(is_e, pad_starts, 0), axis=-1)
            return (start + ri[:, TOP_K:2 * TOP_K]).reshape(n * TOP_K)

        dest_p, dest_s = slots(rip, n_p), slots(ris, n_s)

        xs = jnp.zeros((n_rows, D_MODEL), F32)
        xs = _dispatch(dest_p, h2p.reshape(n_p, D_MODEL), xs, DISPATCH_TILE)
        xs = _dispatch(dest_s, h2s.reshape(n_s, D_MODEL), xs, n_s)
        ye = _moe(block_e, n_valid, xs, w_gu, b_gu, w_down, b_down, l)

        g2, b2 = row(ln2_g[l]), row(ln2_b[l])
        gf_p = mod[:nbp, 5].reshape(nbp, 1, D_MODEL)
        tiles_per_seq = seq // COMBINE_TILE
        gf_p_spec = pl.BlockSpec((None, 1, D_MODEL), lambda i: (i // tiles_per_seq, 0, 0))
        yp = _combine(dest_p, rgp.reshape(n_p, LANES), x1p.reshape(n_p, D_MODEL), gf_p, gf_p_spec,
                      g2, b2, ye, COMBINE_TILE).reshape(nbp, seq, D_MODEL)
        gf_s = jnp.repeat(mod[nbp:nb_all, 5], tdec, axis=0)
        gf_s_spec = pl.BlockSpec((n_s, D_MODEL), lambda i: (0, 0))
        ys_tok = _combine(dest_s, rgs.reshape(n_s, LANES), x1s.reshape(n_s, D_MODEL), gf_s,
                          gf_s_spec, g2, b2, ye, n_s).reshape(nbs, tdec, D_MODEL)

    st = lambda k: jnp.stack(outs[k])
    return (yp, ys_tok, st("kp"), st("vp"), st("pp"), st("ks"), st("vs"), st("ps"))
```

```python
import functools

import jax
import jax.numpy as jnp
from jax import lax
from jax.experimental import pallas as pl
from jax.experimental.pallas import tpu as pltpu

F32 = jnp.float32
BF16 = jnp.bfloat16

D_MODEL = 1024
DEPTH = 2
CHUNK = 64
WINDOW = 128
WIN_CHUNKS = WINDOW // CHUNK
HEAD_DIM = 64
ATT_DIM = 768
N_HEADS = 12
N_KV = 4
GROUP = N_HEADS // N_KV
KV_DIM = N_KV * HEAD_DIM
ATTN_SCALE = HEAD_DIM ** -0.5
ALIBI_MAX = 8.0
NEG_INF = -1e30
POOL_DIM = D_MODEL - ATT_DIM
POOL_WINDOWS = (2, 4, 8, 16)
POOL_CH = POOL_DIM // len(POOL_WINDOWS)
POOL_PAD = max(POOL_WINDOWS) - 1
N_EXPERTS = 32
TOP_K = 4
D_FF = D_MODEL
SWIGLU_LIMIT = 7.0
SWIGLU_ALPHA = 1.702
LN_EPS = 1e-5
DEEPNORM_ALPHA = (2.0 * DEPTH) ** 0.25
PAST_LEN = 1024

LANES = 128
SUBLANES = 8
VMEM_LIMIT = 52 * 1024 * 1024

TOK_TILE = 512
Q_TILE = 128
POOL_HALO = 16
KD_DIM = N_KV * LANES
PROJ_COLS = ATT_DIM + 2 * KD_DIM + POOL_DIM
MOE_ROWS = 256
ROW_TILE = D_MODEL // LANES
assert ROW_TILE == SUBLANES


def _params(sem, vmem=VMEM_LIMIT):
    return pltpu.CompilerParams(dimension_semantics=sem, vmem_limit_bytes=vmem)


def _layer_norm(x):
    mu = jnp.mean(x, axis=-1, keepdims=True)
    xc = x - mu
    var = jnp.mean(xc * xc, axis=-1, keepdims=True)
    return xc * lax.rsqrt(var + LN_EPS)


def _adaln_kernel(c_ref, w_ref, b_ref, o_ref):
    c = c_ref[...]
    s = c * jax.nn.sigmoid(c)
    o_ref[...] = jnp.dot(s, w_ref[...], preferred_element_type=F32,
                         precision=lax.Precision.HIGHEST) + b_ref[...]


def _adaln(c_all, w_ada, b_ada):
    nb = c_all.shape[0]
    ncol = 6 * D_MODEL
    tn = D_MODEL
    return pl.pallas_call(
        _adaln_kernel,
        grid=(DEPTH, ncol // tn),
        in_specs=[
            pl.BlockSpec((nb, D_MODEL), lambda l, j: (0, 0)),
            pl.BlockSpec((None, D_MODEL, tn), lambda l, j: (l, 0, j)),
            pl.BlockSpec((None, 1, tn), lambda l, j: (l, 0, j)),
        ],
        out_specs=pl.BlockSpec((None, nb, tn), lambda l, j: (l, 0, j)),
        out_shape=jax.ShapeDtypeStruct((DEPTH, nb, ncol), F32),
        compiler_params=_params(("parallel", "parallel")),
        name="adaln",
    )(c_all, w_ada, b_ada.reshape(DEPTH, 1, ncol))


def _inproj_kernel(x_ref, mod_ref, w_ref, b_ref, q_ref, kd_ref, vd_ref, u_ref):
    h = _layer_norm(x_ref[...]) * (1.0 + mod_ref[1:2, :]) + mod_ref[0:1, :]
    p = jnp.dot(h.astype(BF16), w_ref[...], preferred_element_type=F32) + b_ref[...]
    q_ref[...] = p[:, :ATT_DIM].astype(BF16)
    kd_ref[...] = p[:, ATT_DIM:ATT_DIM + KD_DIM].astype(BF16)
    vd_ref[...] = p[:, ATT_DIM + KD_DIM:ATT_DIM + 2 * KD_DIM].astype(BF16)
    u_ref[...] = p[:, ATT_DIM + 2 * KD_DIM:]


def _inproj(x, mod, w, b, tile, mod_row0):
    nb, s, _ = x.shape
    nt = s // tile

    def tok(c):
        return pl.BlockSpec((None, tile, c), lambda bi, i: (bi, i, 0))

    return pl.pallas_call(
        _inproj_kernel,
        grid=(nb, nt),
        in_specs=[
            tok(D_MODEL),
            pl.BlockSpec((None, 6, D_MODEL), lambda bi, i: (bi + mod_row0, 0, 0)),
            pl.BlockSpec((D_MODEL, PROJ_COLS), lambda bi, i: (0, 0)),
            pl.BlockSpec((1, PROJ_COLS), lambda bi, i: (0, 0)),
        ],
        out_specs=[tok(ATT_DIM), tok(KD_DIM), tok(KD_DIM), tok(POOL_DIM)],
        out_shape=[
            jax.ShapeDtypeStruct((nb, s, ATT_DIM), BF16),
            jax.ShapeDtypeStruct((nb, s, KD_DIM), BF16),
            jax.ShapeDtypeStruct((nb, s, KD_DIM), BF16),
            jax.ShapeDtypeStruct((nb, s, POOL_DIM), F32),
        ],
        compiler_params=_params(("parallel", "parallel")),
        name="inproj",
    )(x, mod, w, b)


def _mixer_kernel(sinks_ref, q_ref, kd_ref, kdh_ref, vd_ref, vdh_ref, u_ref, uh_ref,
                  bias_ref, wpool_ref, pscale_ref, o_ref, kall_ref, vall_ref,
                  *, tq, history_starts_empty, pos0):
    i = pl.program_id(1)
    t = q_ref.shape[0]
    hk = kdh_ref.shape[0]
    nk = hk + tq
    kall_ref[0:hk, :] = kdh_ref[...]
    kall_ref[hk:, :] = kd_ref[...]
    vall_ref[0:hk, :] = vdh_ref[...]
    vall_ref[hk:, :] = vd_ref[...]

    low_half = lax.broadcasted_iota(jnp.int32, (tq, LANES), 1) < HEAD_DIM
    if history_starts_empty:
        col = lax.broadcasted_iota(jnp.int32, (1, nk), 1)
        no_history = jnp.where((col < hk) & (i == 0), NEG_INF, 0.0).astype(F32)

    for s in range(t // tq):
        rows = slice(s * tq, (s + 1) * tq)
        keys = slice(s * tq, s * tq + nk)
        for pair in range(N_HEADS // 2):
            q2 = q_ref[rows, pair * LANES:(pair + 1) * LANES]
            halves = []
            for half in range(2):
                head = 2 * pair + half
                kv = head // GROUP
                qm = jnp.where(low_half if half == 0 else ~low_half, q2, jnp.zeros_like(q2))
                kd = kall_ref[keys, kv * LANES:(kv + 1) * LANES]
                sc = lax.dot_general(qm, kd, (((1,), (1,)), ((), ())),
                                     preferred_element_type=F32)
                sc = sc + bias_ref[head]
                if history_starts_empty and s == 0:
                    sc = sc + no_history
                sink = sinks_ref[head]
                m = jnp.maximum(jnp.max(sc, axis=1, keepdims=True), sink)
                p = jnp.exp(sc - m)
                den = jnp.sum(p, axis=1, keepdims=True) + jnp.exp(sink - m)
                vd = vall_ref[keys, kv * LANES:(kv + 1) * LANES]
                o2 = jnp.dot(p.astype(BF16), vd, preferred_element_type=F32)
                halves.append(o2 / den)
            o_ref[rows, pair * LANES:(pair + 1) * LANES] = jnp.where(
                low_half, halves[0], halves[1]).astype(BF16)

    uh = uh_ref[...]
    if history_starts_empty:
        uh = jnp.where(i == 0, 0.0, uh)
    u = u_ref[...]
    ue = jnp.concatenate([uh, u], axis=0)
    s2 = ue + pltpu.roll(ue, 1, 0)
    s4 = s2 + pltpu.roll(s2, 2, 0)
    s8 = s4 + pltpu.roll(s4, 4, 0)
    s16 = s8 + pltpu.roll(s8, 8, 0)
    lane = lax.broadcasted_iota(jnp.int32, (1, POOL_DIM), 1)
    g0, g1, g2 = lane < POOL_CH, lane < 2 * POOL_CH, lane < 3 * POOL_CH
    wsum = jnp.where(g0, s2[POOL_HALO:], jnp.where(g1, s4[POOL_HALO:],
                     jnp.where(g2, s8[POOL_HALO:], s16[POOL_HALO:])))
    width = jnp.where(g0, 2.0, jnp.where(g1, 4.0, jnp.where(g2, 8.0, 16.0))).astype(F32)
    pos = (pos0 + i * t + lax.broadcasted_iota(jnp.int32, (t, 1), 0)).astype(F32)
    cnt = jnp.minimum(width, pos + 1.0)
    pooled = wsum / cnt - u
    mixed = jnp.dot(pooled.astype(BF16), wpool_ref[...], preferred_element_type=F32)
    o_ref[:, ATT_DIM:] = (mixed * pscale_ref[...]).astype(BF16)


def _mixer(sinks, q, kd, vd, u, k_hist, v_hist, u_hist, bias, wpool, pscale, *,
           tile, tq, history_starts_empty, pos0):
    nb, s, _ = q.shape
    nt = s // tile
    hk = WINDOW
    own_history = k_hist is None
    if own_history:
        k_hist, v_hist, u_hist = kd, vd, u
        kh_map = lambda bi, i: (bi, jnp.maximum(i * (tile // hk) - 1, 0), 0)
        uh_map = lambda bi, i: (bi, jnp.maximum(i * (tile // POOL_HALO) - 1, 0), 0)
    else:
        kh_map = lambda bi, i: (bi, 0, 0)
        uh_map = lambda bi, i: (bi, 0, 0)

    def tok(c):
        return pl.BlockSpec((None, tile, c), lambda bi, i: (bi, i, 0))

    kern = functools.partial(_mixer_kernel, tq=tq,
                             history_starts_empty=history_starts_empty, pos0=pos0)
    return pl.pallas_call(
        kern,
        grid=(nb, nt),
        in_specs=[
            pl.BlockSpec(memory_space=pltpu.SMEM),
            tok(ATT_DIM),
            tok(KD_DIM),
            pl.BlockSpec((None, hk, KD_DIM), kh_map),
            tok(KD_DIM),
            pl.BlockSpec((None, hk, KD_DIM), kh_map),
            tok(POOL_DIM),
            pl.BlockSpec((None, POOL_HALO, POOL_DIM), uh_map),
            pl.BlockSpec(bias.shape, lambda bi, i: (0, 0, 0)),
            pl.BlockSpec((POOL_DIM, POOL_DIM), lambda bi, i: (0, 0)),
            pl.BlockSpec((1, POOL_DIM), lambda bi, i: (0, 0)),
        ],
        out_specs=tok(D_MODEL),
        out_shape=jax.ShapeDtypeStruct((nb, s, D_MODEL), BF16),
        scratch_shapes=[pltpu.VMEM((hk + tile, KD_DIM), BF16),
                        pltpu.VMEM((hk + tile, KD_DIM), BF16)],
        compiler_params=_params(("parallel", "parallel")),
        name="mixer",
    )(sinks, q, kd, k_hist, vd, v_hist, u, u_hist, bias, wpool, pscale)


def _to_row_tiles(ref, x):
    t = x.shape[0]
    for s in range(ROW_TILE):
        ref[pl.ds(s, t, stride=ROW_TILE), :] = x[:, s * LANES:(s + 1) * LANES]


def _from_row_tiles(ref, t):
    return jnp.concatenate([ref[pl.ds(s, t, stride=ROW_TILE), :] for s in range(ROW_TILE)], axis=1)


def _post_kernel(mix_ref, x_ref, mod_ref, wout_ref, bout_ref, g1_ref, b1_ref, wr_ref, br_ref,
                 cin_ref, x1_ref, h2t_ref, rl_ref, rg_ref, meta_ref, cnt_ref):
    first = (pl.program_id(0) == 0) & (pl.program_id(1) == 0)

    @pl.when(first)
    def _():
        cnt_ref[...] = cin_ref[...]

    t = x_ref.shape[0]
    mix = jnp.dot(mix_ref[...], wout_ref[...], preferred_element_type=F32) + bout_ref[...]
    z = DEEPNORM_ALPHA * x_ref[...] + mod_ref[2:3, :] * mix
    x1 = _layer_norm(z) * g1_ref[...] + b1_ref[...]
    x1_ref[...] = x1
    h2 = _layer_norm(x1) * (1.0 + mod_ref[4:5, :]) + mod_ref[3:4, :]
    _to_row_tiles(h2t_ref, h2)

    h2_hi = h2.astype(BF16)
    h2_lo = (h2 - h2_hi.astype(F32)).astype(BF16)
    d_hi = jnp.dot(h2_hi, wr_ref[...], preferred_element_type=F32)
    d_lo = jnp.dot(h2_lo, wr_ref[:, :LANES], preferred_element_type=F32)
    logits = d_hi[:, :LANES] + d_hi[:, LANES:] + d_lo + br_ref[...]
    lane = lax.broadcasted_iota(jnp.int32, (t, LANES), 1)
    lane_f = lane.astype(F32)
    vals, idxs, hots = [], [], []
    cur = logits
    for _ in range(TOP_K):
        mk = jnp.max(cur, axis=1, keepdims=True)
        ik = jnp.min(jnp.where(cur == mk, lane_f, float(LANES)), axis=1, keepdims=True)
        hot = lane_f == ik
        cur = jnp.where(hot, -jnp.inf, cur)
        vals.append(mk)
        idxs.append(ik)
        hots.append(hot)
    exps = [jnp.exp(v - vals[0]) for v in vals]
    den = exps[0] + exps[1] + exps[2] + exps[3]
    gates = [e / den for e in exps]

    onehot = jnp.zeros((t, LANES), F32)
    for hot in hots:
        onehot = onehot + hot.astype(F32)
    row = lax.broadcasted_iota(jnp.int32, (t, t), 0)
    colt = lax.broadcasted_iota(jnp.int32, (t, t), 1)
    earlier = jnp.where(row > colt, 1.0, 0.0).astype(BF16)
    rank = jnp.dot(earlier, onehot.astype(BF16), preferred_element_type=F32)
    tile_cnt = jnp.broadcast_to(jnp.sum(onehot, axis=0, keepdims=True), (SUBLANES, LANES))
    lane8 = lax.broadcasted_iota(jnp.int32, (SUBLANES, LANES), 1)
    incl = tile_cnt
    for j in range(LANES.bit_length() - 1):
        incl = incl + jnp.where(lane8 >= (1 << j), pltpu.roll(incl, 1 << j, 1), 0.0)
    tile_off = incl - tile_cnt
    local = rank + tile_off[0:1, :]
    slots = [jnp.sum(jnp.where(hot, local, 0.0), axis=1, keepdims=True) for hot in hots]

    rl = jnp.zeros((t, LANES), F32)
    rg = jnp.zeros((t, LANES), F32)
    for k in range(TOP_K):
        rl = jnp.where(lane == k, slots[k], rl)
        rl = jnp.where(lane == TOP_K + k, idxs[k], rl)
        rg = jnp.where(lane == k, gates[k], rg)
    rl_ref[...] = rl.astype(jnp.int32)
    rg_ref[...] = rg
    sub8 = lax.broadcasted_iota(jnp.int32, (SUBLANES, LANES), 0)
    before = jnp.broadcast_to(cnt_ref[0:1, :], (SUBLANES, LANES))
    meta = jnp.where(sub8 == 0, tile_cnt, jnp.where(sub8 == 1, tile_off,
                     jnp.where(sub8 == 2, before, 0.0)))
    meta_ref[...] = meta.astype(jnp.int32)
    cnt_ref[0:1, :] = cnt_ref[0:1, :] + tile_cnt[0:1, :]


def _post(mix, x, mod, wout, bout, g1, b1, wr, br, counts_in, tile, mod_row0):
    nb, s, _ = x.shape
    nt = s // tile

    def tok(c):
        return pl.BlockSpec((None, tile, c), lambda bi, i: (bi, i, 0))

    def whole(shape):
        return pl.BlockSpec(shape, lambda bi, i: tuple(0 for _ in shape))

    return pl.pallas_call(
        _post_kernel,
        grid=(nb, nt),
        in_specs=[
            tok(D_MODEL), tok(D_MODEL),
            pl.BlockSpec((None, 6, D_MODEL), lambda bi, i: (bi + mod_row0, 0, 0)),
            whole((D_MODEL, D_MODEL)), whole((1, D_MODEL)),
            whole((1, D_MODEL)), whole((1, D_MODEL)),
            whole((D_MODEL, 2 * LANES)), whole((1, LANES)),
            whole((SUBLANES, LANES)),
        ],
        out_specs=[tok(D_MODEL),
                   pl.BlockSpec((None, tile * ROW_TILE, LANES), lambda bi, i: (bi, i, 0)),
                   tok(LANES), tok(LANES),
                   pl.BlockSpec((None, None, SUBLANES, LANES), lambda bi, i: (bi, i, 0, 0)),
                   whole((SUBLANES, LANES))],
        out_shape=[
            jax.ShapeDtypeStruct((nb, s, D_MODEL), F32),
            jax.ShapeDtypeStruct((nb, s * ROW_TILE, LANES), F32),
            jax.ShapeDtypeStruct((nb, s, LANES), jnp.int32),
            jax.ShapeDtypeStruct((nb, s, LANES), F32),
            jax.ShapeDtypeStruct((nb, nt, SUBLANES, LANES), jnp.int32),
            jax.ShapeDtypeStruct((SUBLANES, LANES), F32),
        ],
        compiler_params=_params(("arbitrary", "arbitrary")),
        name="post",
    )(mix, x, mod, wout, bout, g1, b1, wr, br, counts_in)


def _tile_rows(row, n=1):
    return pl.ds(pl.multiple_of(row * ROW_TILE, ROW_TILE), n * ROW_TILE)


def _start_run_copies(cnt_ref, off_ref, dst_ref, tile_idx, t, make_copy):
    def body(e, carry):
        j = tile_idx * N_EXPERTS + e
        c, off, dst = cnt_ref[j], off_ref[j], dst_ref[j]
        for b in range(t.bit_length()):
            size = 1 << b

            @pl.when(((c >> b) & 1) == 1)
            def _():
                done = c & (size - 1)
                make_copy(off + done, dst + done, size).start()
        return carry

    lax.fori_loop(0, N_EXPERTS, body, 0)


def _dispatch_kernel(cnt_ref, off_ref, dst_ref, pe_ref, ls_ref, h_ref, *rest,
                     clear_blocks, ls_blocked):
    xs_ref, stage, zbuf, sem = rest[-4:]
    i = pl.program_id(0)
    t = h_ref.shape[0] // ROW_TILE

    if clear_blocks:
        @pl.when(i == 0)
        def _():
            zbuf[...] = jnp.zeros_like(zbuf)

            def clear_copy(row):
                return pltpu.make_async_copy(zbuf, xs_ref.at[_tile_rows(row, MOE_ROWS), :], sem)

            def clear_segment_end(e, n):
                end = pe_ref[e]
                start = jnp.where(e == 0, 0, pe_ref[jnp.maximum(e - 1, 0)])
                for back in range(1, clear_blocks + 1):
                    row = end - back * MOE_ROWS

                    @pl.when(row >= start)
                    def _():
                        clear_copy(row).start()
                    n = n + (row >= start).astype(jnp.int32)
                return n

            n_started = lax.fori_loop(0, N_EXPERTS, clear_segment_end, 0)
            n_blocks = xs_ref.shape[0] // (MOE_ROWS * ROW_TILE)
            n_valid = pe_ref[N_EXPERTS - 1] // MOE_ROWS

            def clear_past_end(b, carry):
                clear_copy(b * MOE_ROWS).start()
                return carry

            lax.fori_loop(n_valid, n_blocks, clear_past_end, 0)

            def wait_one(_, carry):
                clear_copy(0).wait()
                return carry

            lax.fori_loop(0, n_started + n_blocks - n_valid, wait_one, 0)

    base = 0 if ls_blocked else i * (t * TOP_K)

    def scatter(tok, carry):
        tile = h_ref[_tile_rows(tok), :]
        for k in range(TOP_K):
            stage[_tile_rows(ls_ref[base + tok * TOP_K + k]), :] = tile
        return carry

    lax.fori_loop(0, t, scatter, 0)
    _start_run_copies(
        cnt_ref, off_ref, dst_ref, i, t,
        lambda loc, glob, n: pltpu.make_async_copy(stage.at[_tile_rows(loc, n), :],
                                                   xs_ref.at[_tile_rows(glob, n), :], sem))
    pltpu.make_async_copy(stage, xs_ref.at[_tile_rows(0, t * TOP_K), :], sem).wait()


def _dispatch(cnt, off, dst, pad_ends, ls, h2t, xs, n_rows, tile, later_rows=0):
    n_tiles = h2t.shape[0] // (tile * ROW_TILE)
    first = xs is None
    clear_blocks = -(-(later_rows + MOE_ROWS - 1) // MOE_ROWS) if first else 0
    ls_blocked = (tile * TOP_K) % 1024 == 0
    ls_spec = (pl.BlockSpec((tile * TOP_K,), lambda i, *_: (i,), memory_space=pltpu.SMEM)
               if ls_blocked else pl.BlockSpec(memory_space=pltpu.SMEM))
    in_specs = [ls_spec, pl.BlockSpec((tile * ROW_TILE, LANES), lambda i, *_: (i, 0))]
    args = [cnt, off, dst, pad_ends, ls, h2t]
    if not first:
        in_specs.append(pl.BlockSpec(memory_space=pl.ANY))
        args.append(xs)
    grid_spec = pltpu.PrefetchScalarGridSpec(
        num_scalar_prefetch=4,
        grid=(n_tiles,),
        in_specs=in_specs,
        out_specs=pl.BlockSpec(memory_space=pl.ANY),
        scratch_shapes=[pltpu.VMEM((tile * TOP_K * ROW_TILE, LANES), F32),
                        pltpu.VMEM((MOE_ROWS * ROW_TILE, LANES), F32),
                        pltpu.SemaphoreType.DMA],
    )
    return pl.pallas_call(
        functools.partial(_dispatch_kernel, clear_blocks=clear_blocks, ls_blocked=ls_blocked),
        grid_spec=grid_spec,
        out_shape=jax.ShapeDtypeStruct((n_rows * ROW_TILE, LANES), F32),
        input_output_aliases={} if first else {6: 0},
        compiler_params=_params(("arbitrary",)),
        name="dispatch",
    )(*args)


def _moe_kernel(be_ref, nv_ref, xs_ref, wgu_ref, bgu_ref, wdn_ref, bdn_ref, ys_ref,
                wgu_bf, wdn_bf):
    i = pl.program_id(0)
    e = be_ref[i]
    prev = be_ref[jnp.maximum(i - 1, 0)]
    rows = 128

    @pl.when((i == 0) | (e != prev))
    def _():
        def cast(r, carry):
            sl = pl.ds(pl.multiple_of(r * rows, rows), rows)
            wgu_bf[sl, :] = wgu_ref[sl, :].astype(BF16)
            wdn_bf[sl, :] = wdn_ref[sl, :].astype(BF16)
            return carry
        lax.fori_loop(0, D_MODEL // rows, cast, 0)

    @pl.when(i < nv_ref[0])
    def _():
        x = _from_row_tiles(xs_ref, MOE_ROWS).astype(BF16)
        gu = jnp.dot(x, wgu_bf[...], preferred_element_type=F32) + bgu_ref[...]
        gate = jnp.minimum(gu[:, :D_FF], SWIGLU_LIMIT)
        up = jnp.clip(gu[:, D_FF:], -SWIGLU_LIMIT, SWIGLU_LIMIT)
        a = gate * jax.nn.sigmoid(SWIGLU_ALPHA * gate) * (up + 1.0)
        y = jnp.dot(a.astype(BF16), wdn_bf[...], preferred_element_type=F32) + bdn_ref[...]
        _to_row_tiles(ys_ref, y)

    @pl.when(i >= nv_ref[0])
    def _():
        ys_ref[...] = jnp.zeros_like(ys_ref)


def _moe(block_e, n_valid, xs, w_gu, b_gu, w_down, b_down, layer):
    n_rows = xs.shape[0] // ROW_TILE
    n_blocks = n_rows // MOE_ROWS
    grid_spec = pltpu.PrefetchScalarGridSpec(
        num_scalar_prefetch=2,
        grid=(n_blocks,),
        in_specs=[
            pl.BlockSpec((MOE_ROWS * ROW_TILE, LANES),
                         lambda i, be, nv: (jnp.minimum(i, nv[0] - 1), 0)),
            pl.BlockSpec((None, None, D_MODEL, 2 * D_FF), lambda i, be, nv: (layer, be[i], 0, 0)),
            pl.BlockSpec((None, None, 1, 2 * D_FF), lambda i, be, nv: (layer, be[i], 0, 0)),
            pl.BlockSpec((None, None, D_FF, D_MODEL), lambda i, be, nv: (layer, be[i], 0, 0)),
            pl.BlockSpec((None, None, 1, D_MODEL), lambda i, be, nv: (layer, be[i], 0, 0)),
        ],
        out_specs=pl.BlockSpec((MOE_ROWS * ROW_TILE, LANES), lambda i, be, nv: (i, 0)),
        scratch_shapes=[pltpu.VMEM((D_MODEL, 2 * D_FF), BF16),
                        pltpu.VMEM((D_FF, D_MODEL), BF16)],
    )
    return pl.pallas_call(
        _moe_kernel,
        grid_spec=grid_spec,
        out_shape=jax.ShapeDtypeStruct((n_rows * ROW_TILE, LANES), F32),
        compiler_params=_params(("arbitrary",)),
        name="moe",
    )(block_e, n_valid, xs, w_gu, b_gu.reshape(DEPTH, N_EXPERTS, 1, 2 * D_FF),
      w_down, b_down.reshape(DEPTH, N_EXPERTS, 1, D_MODEL))


def _combine_kernel(cnt_ref, off_ref, dst_ref, ls_ref, gate_ref, x1_ref, gf_ref, g2_ref, b2_ref,
                    ys_ref, o_ref, stage, ft, sem, *, ls_blocked):
    i = pl.program_id(0)
    t = x1_ref.shape[0]
    _start_run_copies(
        cnt_ref, off_ref, dst_ref, i, t,
        lambda loc, glob, n: pltpu.make_async_copy(ys_ref.at[_tile_rows(glob, n), :],
                                                   stage.at[_tile_rows(loc, n), :], sem))
    pltpu.make_async_copy(ys_ref.at[_tile_rows(0, t * TOP_K), :], stage, sem).wait()

    base = 0 if ls_blocked else i * (t * TOP_K)

    def gather(tok, carry):
        acc = None
        for k in range(TOP_K):
            j = base + tok * TOP_K + k
            v = stage[_tile_rows(ls_ref[j]), :] * gate_ref[j]
            acc = v if acc is None else acc + v
        ft[_tile_rows(tok), :] = acc
        return carry

    lax.fori_loop(0, t, gather, 0)
    f = _from_row_tiles(ft, t)
    z = DEEPNORM_ALPHA * x1_ref[...] + gf_ref[...] * f
    o_ref[...] = _layer_norm(z) * g2_ref[...] + b2_ref[...]


def _combine(cnt, off, dst, ls, gates, x1, gf, g2, b2, ys, tile, tiles_per_seq):
    n = x1.shape[0]
    ls_blocked = (tile * TOP_K) % 1024 == 0

    def smem_vec():
        if ls_blocked:
            return pl.BlockSpec((tile * TOP_K,), lambda i, *_: (i,), memory_space=pltpu.SMEM)
        return pl.BlockSpec(memory_space=pltpu.SMEM)

    grid_spec = pltpu.PrefetchScalarGridSpec(
        num_scalar_prefetch=3,
        grid=(n // tile,),
        in_specs=[
            smem_vec(), smem_vec(),
            pl.BlockSpec((tile, D_MODEL), lambda i, *_: (i, 0)),
            pl.BlockSpec((None, 1, D_MODEL), lambda i, *_: (i // tiles_per_seq, 0, 0)),
            pl.BlockSpec((1, D_MODEL), lambda i, *_: (0, 0)),
            pl.BlockSpec((1, D_MODEL), lambda i, *_: (0, 0)),
            pl.BlockSpec(memory_space=pl.ANY),
        ],
        out_specs=pl.BlockSpec((tile, D_MODEL), lambda i, *_: (i, 0)),
        scratch_shapes=[pltpu.VMEM((tile * TOP_K * ROW_TILE, LANES), F32),
                        pltpu.VMEM((tile * ROW_TILE, LANES), F32),
                        pltpu.SemaphoreType.DMA],
    )
    return pl.pallas_call(
        functools.partial(_combine_kernel, ls_blocked=ls_blocked),
        grid_spec=grid_spec,
        out_shape=jax.ShapeDtypeStruct((n, D_MODEL), F32),
        compiler_params=_params(("arbitrary",)),
        name="combine",
    )(cnt, off, dst, ls, gates, x1, gf, g2, b2, ys)


def _alibi_slopes():
    return jnp.exp2(-ALIBI_MAX * (jnp.arange(N_HEADS, dtype=F32) + 1.0) / N_HEADS)


def _prompt_bias(slopes):
    r = jnp.arange(Q_TILE)
    j = jnp.arange(WINDOW + Q_TILE)
    dc = (WINDOW + r)[:, None] // CHUNK - j[None, :] // CHUNK
    vis = (dc >= 0) & (dc <= WIN_CHUNKS)
    dist = jnp.abs(r[:, None] + WINDOW - j[None, :]).astype(F32)
    return jnp.where(vis[None], -slopes[:, None, None] * dist[None], NEG_INF)


def _sample_bias(slopes, t, hist):
    qpos = PAST_LEN + jnp.arange(t)
    kpos = PAST_LEN - hist + jnp.arange(hist + t)
    dc = qpos[:, None] // CHUNK - kpos[None, :] // CHUNK
    vis = (dc >= 0) & (dc <= WIN_CHUNKS) & (kpos[None, :] >= 0)
    dist = jnp.abs(qpos[:, None] - kpos[None, :]).astype(F32)
    return jnp.where(vis[None], -slopes[:, None, None] * dist[None], NEG_INF)


def _dup_heads(w):
    lead = w.shape[:-1]
    w4 = w.reshape(lead + (N_KV, HEAD_DIM))
    return jnp.concatenate([w4, w4], axis=-1).reshape(lead + (KD_DIM,))


def _proj_weights(w_in, b_in):
    q0, k0, v0, u0 = 0, ATT_DIM, ATT_DIM + KV_DIM, ATT_DIM + 2 * KV_DIM
    w = jnp.concatenate([w_in[:, q0:k0] * ATTN_SCALE, _dup_heads(w_in[:, k0:v0]),
                         _dup_heads(w_in[:, v0:u0]), w_in[:, u0:]], axis=1)
    b = jnp.concatenate([b_in[q0:k0] * ATTN_SCALE, _dup_heads(b_in[k0:v0]),
                         _dup_heads(b_in[v0:u0]), b_in[u0:]], axis=0)
    return w.astype(BF16), b.reshape(1, PROJ_COLS)


def _pool_weight(w_pool):
    n = len(POOL_WINDOWS)
    eye = jnp.eye(n, dtype=w_pool.dtype)
    return jnp.einsum('gcd,gh->gchd', w_pool, eye).reshape(POOL_DIM, POOL_DIM).astype(BF16)


def _undup(kd, nb, rows):
    return kd[:, -rows:].reshape(nb, rows, N_KV, LANES)[..., :HEAD_DIM].astype(F32)


def kernel(x_prompt, x_sample, c_prompt, c_sample, cache_k, cache_v, state_pool, w_ada, b_ada,
           w_in, b_in, sinks, w_pool, pool_scale, w_out, b_out, ln1_g, ln1_b, ln2_g, ln2_b,
           w_router, b_router, w_gu, b_gu, w_down, b_down):
    nbp, seq, _ = x_prompt.shape
    nbs, tdec, _ = x_sample.shape
    hist = cache_k.shape[2]
    n_p, n_s = nbp * seq, nbs * tdec
    n_asg = (n_p + n_s) * TOP_K
    n_blocks = -(-(n_asg + N_EXPERTS * (MOE_ROWS - 1)) // MOE_ROWS)
    n_rows = n_blocks * MOE_ROWS
    assert hist == WINDOW and tdec >= POOL_PAD and seq % TOK_TILE == 0

    nb_all = nbp + nbs
    nb_pad = -(-nb_all // SUBLANES) * SUBLANES
    c_all = jnp.concatenate([c_prompt, c_sample, jnp.zeros((nb_pad - nb_all, D_MODEL), F32)], 0)
    mod_all = _adaln(c_all, w_ada, b_ada).reshape(DEPTH, nb_pad, 6, D_MODEL)

    slopes = _alibi_slopes()
    bias_p = _prompt_bias(slopes)
    bias_s = _sample_bias(slopes, tdec, hist)
    row = lambda v: v.reshape(1, -1)

    yp, ys_tok = x_prompt, x_sample
    outs = {k: [] for k in ("kp", "vp", "pp", "ks", "vs", "ps")}
    for l in range(DEPTH):
        mod = mod_all[l]
        w_proj, b_proj = _proj_weights(w_in[l], b_in[l])
        wpool = _pool_weight(w_pool[l])
        pscale = row(pool_scale[l])
        wout = w_out[l].astype(BF16)
        wr_f32 = jnp.pad(w_router[l], ((0, 0), (0, LANES - N_EXPERTS)))
        wr_hi = wr_f32.astype(BF16)
        wr = jnp.concatenate([wr_hi, (wr_f32 - wr_hi.astype(F32)).astype(BF16)], axis=1)
        br = jnp.concatenate([b_router[l], jnp.full((LANES - N_EXPERTS,), NEG_INF, F32)]).reshape(1, LANES)

        q, kd, vd, u = _inproj(yp, mod, w_proj, b_proj, TOK_TILE, 0)
        mix_p = _mixer(sinks[l], q, kd, vd, u, None, None, None, bias_p, wpool, pscale,
                       tile=TOK_TILE, tq=Q_TILE, history_starts_empty=True, pos0=0)
        outs["kp"].append(_undup(kd, nbp, WINDOW))
        outs["vp"].append(_undup(vd, nbp, WINDOW))
        outs["pp"].append(u[:, -POOL_PAD:])

        qs, kds, vds, us = _inproj(ys_tok, mod, w_proj, b_proj, tdec, nbp)
        ck = _dup_heads(cache_k[l].reshape(nbs, hist, KV_DIM)).astype(BF16)
        cv = _dup_heads(cache_v[l].reshape(nbs, hist, KV_DIM)).astype(BF16)
        sp = jnp.pad(state_pool[l], ((0, 0), (POOL_HALO - POOL_PAD, 0), (0, 0)))
        mix_s = _mixer(sinks[l], qs, kds, vds, us, ck, cv, sp, bias_s, wpool, pscale,
                       tile=tdec, tq=tdec, history_starts_empty=False, pos0=PAST_LEN)
        k_new = _undup(kds, nbs, tdec)
        v_new = _undup(vds, nbs, tdec)
        outs["ks"].append(jnp.concatenate([cache_k[l], k_new], axis=1)[:, -hist:])
        outs["vs"].append(jnp.concatenate([cache_v[l], v_new], axis=1)[:, -hist:])
        outs["ps"].append(us[:, -POOL_PAD:])

        post_w = (wout, row(b_out[l]), row(ln1_g[l]), row(ln1_b[l]), wr, br)
        zero_counts = jnp.zeros((SUBLANES, LANES), F32)
        x1p, h2tp, rlp, rgp, meta_p, cnt_p = _post(mix_p, yp, mod, *post_w, zero_counts, TOK_TILE, 0)
        x1s, h2ts, rls, rgs, meta_s, cnt = _post(mix_s, ys_tok, mod, *post_w, cnt_p, tdec, nbp)

        counts = cnt[0, :N_EXPERTS].astype(jnp.int32)
        padded = (counts + MOE_ROWS - 1) // MOE_ROWS * MOE_ROWS
        pad_ends = jnp.cumsum(padded).astype(jnp.int32)
        pad_starts = pad_ends - padded
        block_start = jnp.arange(n_blocks, dtype=jnp.int32) * MOE_ROWS
        block_e = jnp.minimum(jnp.sum(pad_ends[None, :] <= block_start[:, None], axis=1),
                              N_EXPERTS - 1).astype(jnp.int32)
        n_valid = (pad_ends[-1:] // MOE_ROWS).astype(jnp.int32)

        def tables(meta):
            m = meta.reshape(-1, SUBLANES, LANES)[:, :, :N_EXPERTS]
            flat = lambda a: a.reshape(-1).astype(jnp.int32)
            return flat(m[:, 0]), flat(m[:, 1]), flat(pad_starts[None, :] + m[:, 2])

        def per_assignment(r, n):
            return r.reshape(n, LANES)[:, :TOP_K].reshape(n * TOP_K)

        tab_p, tab_s = tables(meta_p), tables(meta_s)
        ls_p, ls_s = per_assignment(rlp, n_p), per_assignment(rls, n_s)

        xs = _dispatch(*tab_p, pad_ends, ls_p, h2tp.reshape(n_p * ROW_TILE, LANES), None,
                       n_rows, TOK_TILE, later_rows=n_s)
        xs = _dispatch(*tab_s, pad_ends, ls_s, h2ts.reshape(n_s * ROW_TILE, LANES), xs,
                       n_rows, tdec)
        ye = _moe(block_e, n_valid, xs, w_gu, b_gu, w_down, b_down, l)

        g2, b2 = row(ln2_g[l]), row(ln2_b[l])
        gf = mod[:, 5].reshape(nb_pad, 1, D_MODEL)
        yp = _combine(*tab_p, ls_p, per_assignment(rgp, n_p), x1p.reshape(n_p, D_MODEL),
                      gf[:nbp], g2, b2, ye, TOK_TILE, seq // TOK_TILE).reshape(nbp, seq, D_MODEL)
        ys_tok = _combine(*tab_s, ls_s, per_assignment(rgs, n_s), x1s.reshape(n_s, D_MODEL),
                          gf[nbp:nb_all], g2, b2, ye, tdec, 1).reshape(nbs, tdec, D_MODEL)

    st = lambda k: jnp.stack(outs[k])
    return (yp, ys_tok, st("kp"), st("vp"), st("pp"), st("ks"), st("vs"), st("ps"))
```

```python
import functools

import jax
import jax.numpy as jnp
from jax import lax
from jax.experimental import pallas as pl
from jax.experimental.pallas import tpu as pltpu

F32 = jnp.float32
BF16 = jnp.bfloat16

D_MODEL = 1024
DEPTH = 2
CHUNK = 64
WINDOW = 128
WIN_CHUNKS = WINDOW // CHUNK
HEAD_DIM = 64
ATT_DIM = 768
N_HEADS = 12
N_KV = 4
GROUP = N_HEADS // N_KV
KV_DIM = N_KV * HEAD_DIM
ATTN_SCALE = HEAD_DIM ** -0.5
ALIBI_MAX = 8.0
NEG_INF = -1e30
POOL_DIM = D_MODEL - ATT_DIM
POOL_WINDOWS = (2, 4, 8, 16)
POOL_CH = POOL_DIM // len(POOL_WINDOWS)
POOL_PAD = max(POOL_WINDOWS) - 1
N_EXPERTS = 32
TOP_K = 4
D_FF = D_MODEL
SWIGLU_LIMIT = 7.0
SWIGLU_ALPHA = 1.702
LN_EPS = 1e-5
DEEPNORM_ALPHA = (2.0 * DEPTH) ** 0.25
PAST_LEN = 1024

LANES = 128
SUBLANES = 8
VMEM_LIMIT = 52 * 1024 * 1024

TOK_TILE = 512
Q_TILE = 128
POOL_HALO = 16
KD_DIM = N_KV * LANES
PROJ_COLS = ATT_DIM + 2 * KD_DIM + POOL_DIM
MOE_ROWS = 512
FF_CHUNK = 1024
TOKEN_UNROLL = 8
ROW_TILE = D_MODEL // LANES
assert ROW_TILE == SUBLANES


def _params(sem, vmem=VMEM_LIMIT):
    return pltpu.CompilerParams(dimension_semantics=sem, vmem_limit_bytes=vmem)


def _layer_norm(x):
    mu = jnp.mean(x, axis=-1, keepdims=True)
    xc = x - mu
    var = jnp.mean(xc * xc, axis=-1, keepdims=True)
    return xc * lax.rsqrt(var + LN_EPS)


def _adaln_kernel(c_ref, w_ref, b_ref, o_ref):
    c = c_ref[...]
    s = c * jax.nn.sigmoid(c)
    o_ref[...] = jnp.dot(s, w_ref[...], preferred_element_type=F32,
                         precision=lax.Precision.HIGHEST) + b_ref[...]


def _adaln(c_all, w_ada, b_ada):
    nb = c_all.shape[0]
    ncol = 6 * D_MODEL
    tn = D_MODEL
    return pl.pallas_call(
        _adaln_kernel,
        grid=(DEPTH, ncol // tn),
        in_specs=[
            pl.BlockSpec((nb, D_MODEL), lambda l, j: (0, 0)),
            pl.BlockSpec((None, D_MODEL, tn), lambda l, j: (l, 0, j)),
            pl.BlockSpec((None, 1, tn), lambda l, j: (l, 0, j)),
        ],
        out_specs=pl.BlockSpec((None, nb, tn), lambda l, j: (l, 0, j)),
        out_shape=jax.ShapeDtypeStruct((DEPTH, nb, ncol), F32),
        compiler_params=_params(("parallel", "parallel")),
        name="adaln",
    )(c_all, w_ada, b_ada.reshape(DEPTH, 1, ncol))


def _inproj_kernel(x_ref, mod_ref, w_ref, b_ref, q_ref, kd_ref, vd_ref, u_ref):
    h = _layer_norm(x_ref[...]) * (1.0 + mod_ref[1:2, :]) + mod_ref[0:1, :]
    p = jnp.dot(h.astype(BF16), w_ref[...], preferred_element_type=F32) + b_ref[...]
    q_ref[...] = p[:, :ATT_DIM].astype(BF16)
    kd_ref[...] = p[:, ATT_DIM:ATT_DIM + KD_DIM].astype(BF16)
    vd_ref[...] = p[:, ATT_DIM + KD_DIM:ATT_DIM + 2 * KD_DIM].astype(BF16)
    u_ref[...] = p[:, ATT_DIM + 2 * KD_DIM:]


def _inproj(x, mod, w, b, tile, mod_row0):
    nb, s, _ = x.shape
    nt = s // tile

    def tok(c):
        return pl.BlockSpec((None, tile, c), lambda bi, i: (bi, i, 0))

    return pl.pallas_call(
        _inproj_kernel,
        grid=(nb, nt),
        in_specs=[
            tok(D_MODEL),
            pl.BlockSpec((None, 6, D_MODEL), lambda bi, i: (bi + mod_row0, 0, 0)),
            pl.BlockSpec((D_MODEL, PROJ_COLS), lambda bi, i: (0, 0)),
            pl.BlockSpec((1, PROJ_COLS), lambda bi, i: (0, 0)),
        ],
        out_specs=[tok(ATT_DIM), tok(KD_DIM), tok(KD_DIM), tok(POOL_DIM)],
        out_shape=[
            jax.ShapeDtypeStruct((nb, s, ATT_DIM), BF16),
            jax.ShapeDtypeStruct((nb, s, KD_DIM), BF16),
            jax.ShapeDtypeStruct((nb, s, KD_DIM), BF16),
            jax.ShapeDtypeStruct((nb, s, POOL_DIM), F32),
        ],
        compiler_params=_params(("parallel", "parallel")),
        name="inproj",
    )(x, mod, w, b)


def _mixer_kernel(sinks_ref, q_ref, kd_ref, kdh_ref, vd_ref, vdh_ref, u_ref, uh_ref,
                  bias_ref, wpool_ref, pscale_ref, o_ref, kall_ref, vall_ref,
                  *, tq, history_starts_empty, pos0):
    i = pl.program_id(1)
    t = q_ref.shape[0]
    hk = kdh_ref.shape[0]
    nk = hk + tq
    kall_ref[0:hk, :] = kdh_ref[...]
    kall_ref[hk:, :] = kd_ref[...]
    vall_ref[0:hk, :] = vdh_ref[...]
    vall_ref[hk:, :] = vd_ref[...]

    low_half = lax.broadcasted_iota(jnp.int32, (tq, LANES), 1) < HEAD_DIM
    if history_starts_empty:
        col = lax.broadcasted_iota(jnp.int32, (1, nk), 1)
        no_history = jnp.where((col < hk) & (i == 0), NEG_INF, 0.0).astype(F32)

    for s in range(t // tq):
        rows = slice(s * tq, (s + 1) * tq)
        keys = slice(s * tq, s * tq + nk)
        for pair in range(N_HEADS // 2):
            q2 = q_ref[rows, pair * LANES:(pair + 1) * LANES]
            halves = []
            for half in range(2):
                head = 2 * pair + half
                kv = head // GROUP
                qm = jnp.where(low_half if half == 0 else ~low_half, q2, jnp.zeros_like(q2))
                kd = kall_ref[keys, kv * LANES:(kv + 1) * LANES]
                sc = lax.dot_general(qm, kd, (((1,), (1,)), ((), ())),
                                     preferred_element_type=F32)
                sc = sc + bias_ref[head]
                if history_starts_empty and s == 0:
                    sc = sc + no_history
                sink = sinks_ref[head]
                m = jnp.maximum(jnp.max(sc, axis=1, keepdims=True), sink)
                p = jnp.exp(sc - m)
                den = jnp.sum(p, axis=1, keepdims=True) + jnp.exp(sink - m)
                vd = vall_ref[keys, kv * LANES:(kv + 1) * LANES]
                o2 = jnp.dot(p.astype(BF16), vd, preferred_element_type=F32)
                halves.append(o2 / den)
            o_ref[rows, pair * LANES:(pair + 1) * LANES] = jnp.where(
                low_half, halves[0], halves[1]).astype(BF16)

    uh = uh_ref[...]
    if history_starts_empty:
        uh = jnp.where(i == 0, 0.0, uh)
    u = u_ref[...]
    ue = jnp.concatenate([uh, u], axis=0)
    s2 = ue + pltpu.roll(ue, 1, 0)
    s4 = s2 + pltpu.roll(s2, 2, 0)
    s8 = s4 + pltpu.roll(s4, 4, 0)
    s16 = s8 + pltpu.roll(s8, 8, 0)
    lane = lax.broadcasted_iota(jnp.int32, (1, POOL_DIM), 1)
    g0, g1, g2 = lane < POOL_CH, lane < 2 * POOL_CH, lane < 3 * POOL_CH
    wsum = jnp.where(g0, s2[POOL_HALO:], jnp.where(g1, s4[POOL_HALO:],
                     jnp.where(g2, s8[POOL_HALO:], s16[POOL_HALO:])))
    width = jnp.where(g0, 2.0, jnp.where(g1, 4.0, jnp.where(g2, 8.0, 16.0))).astype(F32)
    pos = (pos0 + i * t + lax.broadcasted_iota(jnp.int32, (t, 1), 0)).astype(F32)
    cnt = jnp.minimum(width, pos + 1.0)
    pooled = wsum / cnt - u
    mixed = jnp.dot(pooled.astype(BF16), wpool_ref[...], preferred_element_type=F32)
    o_ref[:, ATT_DIM:] = (mixed * pscale_ref[...]).astype(BF16)


def _mixer(sinks, q, kd, vd, u, k_hist, v_hist, u_hist, bias, wpool, pscale, *,
           tile, tq, history_starts_empty, pos0):
    nb, s, _ = q.shape
    nt = s // tile
    hk = WINDOW
    own_history = k_hist is None
    if own_history:
        k_hist, v_hist, u_hist = kd, vd, u
        kh_map = lambda bi, i: (bi, jnp.maximum(i * (tile // hk) - 1, 0), 0)
        uh_map = lambda bi, i: (bi, jnp.maximum(i * (tile // POOL_HALO) - 1, 0), 0)
    else:
        kh_map = lambda bi, i: (bi, 0, 0)
        uh_map = lambda bi, i: (bi, 0, 0)

    def tok(c):
        return pl.BlockSpec((None, tile, c), lambda bi, i: (bi, i, 0))

    kern = functools.partial(_mixer_kernel, tq=tq,
                             history_starts_empty=history_starts_empty, pos0=pos0)
    return pl.pallas_call(
        kern,
        grid=(nb, nt),
        in_specs=[
            pl.BlockSpec(memory_space=pltpu.SMEM),
            tok(ATT_DIM),
            tok(KD_DIM),
            pl.BlockSpec((None, hk, KD_DIM), kh_map),
            tok(KD_DIM),
            pl.BlockSpec((None, hk, KD_DIM), kh_map),
            tok(POOL_DIM),
            pl.BlockSpec((None, POOL_HALO, POOL_DIM), uh_map),
            pl.BlockSpec(bias.shape, lambda bi, i: (0, 0, 0)),
            pl.BlockSpec((POOL_DIM, POOL_DIM), lambda bi, i: (0, 0)),
            pl.BlockSpec((1, POOL_DIM), lambda bi, i: (0, 0)),
        ],
        out_specs=tok(D_MODEL),
        out_shape=jax.ShapeDtypeStruct((nb, s, D_MODEL), BF16),
        scratch_shapes=[pltpu.VMEM((hk + tile, KD_DIM), BF16),
                        pltpu.VMEM((hk + tile, KD_DIM), BF16)],
        compiler_params=_params(("parallel", "parallel")),
        name="mixer",
    )(sinks, q, kd, k_hist, vd, v_hist, u, u_hist, bias, wpool, pscale)


def _to_row_tiles(ref, x):
    t = x.shape[0]
    for s in range(ROW_TILE):
        ref[pl.ds(s, t, stride=ROW_TILE), :] = x[:, s * LANES:(s + 1) * LANES]


def _from_row_tiles(ref, t):
    return jnp.concatenate([ref[pl.ds(s, t, stride=ROW_TILE), :] for s in range(ROW_TILE)], axis=1)


def _post_kernel(mix_ref, x_ref, mod_ref, wout_ref, bout_ref, g1_ref, b1_ref, wr_ref, br_ref,
                 cin_ref, x1_ref, h2t_ref, rl_ref, rg_ref, meta_ref, cnt_ref):
    first = (pl.program_id(0) == 0) & (pl.program_id(1) == 0)

    @pl.when(first)
    def _():
        cnt_ref[...] = cin_ref[...]

    t = x_ref.shape[0]
    mix = jnp.dot(mix_ref[...], wout_ref[...], preferred_element_type=F32) + bout_ref[...]
    z = DEEPNORM_ALPHA * x_ref[...] + mod_ref[2:3, :] * mix
    x1 = _layer_norm(z) * g1_ref[...] + b1_ref[...]
    x1_ref[...] = x1
    h2 = _layer_norm(x1) * (1.0 + mod_ref[4:5, :]) + mod_ref[3:4, :]
    _to_row_tiles(h2t_ref, h2)

    h2_hi = h2.astype(BF16)
    h2_lo = (h2 - h2_hi.astype(F32)).astype(BF16)
    d_hi = jnp.dot(h2_hi, wr_ref[...], preferred_element_type=F32)
    d_lo = jnp.dot(h2_lo, wr_ref[:, :LANES], preferred_element_type=F32)
    logits = d_hi[:, :LANES] + d_hi[:, LANES:] + d_lo + br_ref[...]
    lane = lax.broadcasted_iota(jnp.int32, (t, LANES), 1)
    lane_f = lane.astype(F32)
    vals, idxs, hots = [], [], []
    cur = logits
    for _ in range(TOP_K):
        mk = jnp.max(cur, axis=1, keepdims=True)
        ik = jnp.min(jnp.where(cur == mk, lane_f, float(LANES)), axis=1, keepdims=True)
        hot = lane_f == ik
        cur = jnp.where(hot, -jnp.inf, cur)
        vals.append(mk)
        idxs.append(ik)
        hots.append(hot)
    exps = [jnp.exp(v - vals[0]) for v in vals]
    den = exps[0] + exps[1] + exps[2] + exps[3]
    gates = [e / den for e in exps]

    onehot = jnp.zeros((t, LANES), F32)
    for hot in hots:
        onehot = onehot + hot.astype(F32)
    row = lax.broadcasted_iota(jnp.int32, (t, t), 0)
    colt = lax.broadcasted_iota(jnp.int32, (t, t), 1)
    earlier = jnp.where(row > colt, 1.0, 0.0).astype(BF16)
    rank = jnp.dot(earlier, onehot.astype(BF16), preferred_element_type=F32)
    tile_cnt = jnp.broadcast_to(jnp.sum(onehot, axis=0, keepdims=True), (SUBLANES, LANES))
    lane8 = lax.broadcasted_iota(jnp.int32, (SUBLANES, LANES), 1)
    incl = tile_cnt
    for j in range(LANES.bit_length() - 1):
        incl = incl + jnp.where(lane8 >= (1 << j), pltpu.roll(incl, 1 << j, 1), 0.0)
    tile_off = incl - tile_cnt
    local = rank + tile_off[0:1, :]
    slots = [jnp.sum(jnp.where(hot, local, 0.0), axis=1, keepdims=True) for hot in hots]

    rl = jnp.zeros((t, LANES), F32)
    rg = jnp.zeros((t, LANES), F32)
    for k in range(TOP_K):
        rl = jnp.where(lane == k, slots[k], rl)
        rl = jnp.where(lane == TOP_K + k, idxs[k], rl)
        rg = jnp.where(lane == k, gates[k], rg)
    rl_ref[...] = rl.astype(jnp.int32)
    rg_ref[...] = rg
    sub8 = lax.broadcasted_iota(jnp.int32, (SUBLANES, LANES), 0)
    before = jnp.broadcast_to(cnt_ref[0:1, :], (SUBLANES, LANES))
    meta = jnp.where(sub8 == 0, tile_cnt, jnp.where(sub8 == 1, tile_off,
                     jnp.where(sub8 == 2, before, 0.0)))
    meta_ref[...] = meta.astype(jnp.int32)
    cnt_ref[0:1, :] = cnt_ref[0:1, :] + tile_cnt[0:1, :]


def _post(mix, x, mod, wout, bout, g1, b1, wr, br, counts_in, tile, mod_row0):
    nb, s, _ = x.shape
    nt = s // tile

    def tok(c):
        return pl.BlockSpec((None, tile, c), lambda bi, i: (bi, i, 0))

    def whole(shape):
        return pl.BlockSpec(shape, lambda bi, i: tuple(0 for _ in shape))

    return pl.pallas_call(
        _post_kernel,
        grid=(nb, nt),
        in_specs=[
            tok(D_MODEL), tok(D_MODEL),
            pl.BlockSpec((None, 6, D_MODEL), lambda bi, i: (bi + mod_row0, 0, 0)),
            whole((D_MODEL, D_MODEL)), whole((1, D_MODEL)),
            whole((1, D_MODEL)), whole((1, D_MODEL)),
            whole((D_MODEL, 2 * LANES)), whole((1, LANES)),
            whole((SUBLANES, LANES)),
        ],
        out_specs=[tok(D_MODEL),
                   pl.BlockSpec((None, tile * ROW_TILE, LANES), lambda bi, i: (bi, i, 0)),
                   tok(LANES), tok(LANES),
                   pl.BlockSpec((None, None, SUBLANES, LANES), lambda bi, i: (bi, i, 0, 0)),
                   whole((SUBLANES, LANES))],
        out_shape=[
            jax.ShapeDtypeStruct((nb, s, D_MODEL), F32),
            jax.ShapeDtypeStruct((nb, s * ROW_TILE, LANES), F32),
            jax.ShapeDtypeStruct((nb, s, LANES), jnp.int32),
            jax.ShapeDtypeStruct((nb, s, LANES), F32),
            jax.ShapeDtypeStruct((nb, nt, SUBLANES, LANES), jnp.int32),
            jax.ShapeDtypeStruct((SUBLANES, LANES), F32),
        ],
        compiler_params=_params(("arbitrary", "arbitrary")),
        name="post",
    )(mix, x, mod, wout, bout, g1, b1, wr, br, counts_in)


def _tile_rows(row, n=1):
    return pl.ds(pl.multiple_of(row * ROW_TILE, ROW_TILE), n * ROW_TILE)


def _start_run_copies(cnt_ref, off_ref, dst_ref, tile_idx, t, make_copy):
    def body(e, carry):
        j = tile_idx * N_EXPERTS + e
        c, off, dst = cnt_ref[j], off_ref[j], dst_ref[j]
        for b in range(t.bit_length()):
            size = 1 << b

            @pl.when(((c >> b) & 1) == 1)
            def _():
                done = c & (size - 1)
                make_copy(off + done, dst + done, size).start()
        return carry

    lax.fori_loop(0, N_EXPERTS, body, 0)


def _dispatch_kernel(cnt_ref, off_ref, dst_ref, pe_ref, ls_ref, h_ref, *rest,
                     clear_blocks, ls_blocked):
    xs_ref, stage, zbuf, sems, sem = rest[-5:]
    i = pl.program_id(0)
    t = h_ref.shape[0] // ROW_TILE

    if clear_blocks:
        @pl.when(i == 0)
        def _():
            zbuf[...] = jnp.zeros_like(zbuf)

            def clear_copy(row):
                return pltpu.make_async_copy(zbuf, xs_ref.at[_tile_rows(row, MOE_ROWS), :], sem)

            def clear_segment_end(e, n):
                end = pe_ref[e]
                start = jnp.where(e == 0, 0, pe_ref[jnp.maximum(e - 1, 0)])
                for back in range(1, clear_blocks + 1):
                    row = end - back * MOE_ROWS

                    @pl.when(row >= start)
                    def _():
                        clear_copy(row).start()
                    n = n + (row >= start).astype(jnp.int32)
                return n

            n_started = lax.fori_loop(0, N_EXPERTS, clear_segment_end, 0)
            n_blocks = xs_ref.shape[0] // (MOE_ROWS * ROW_TILE)
            n_valid = pe_ref[N_EXPERTS - 1] // MOE_ROWS

            def clear_past_end(b, carry):
                clear_copy(b * MOE_ROWS).start()
                return carry

            lax.fori_loop(n_valid, n_blocks, clear_past_end, 0)

            def wait_one(_, carry):
                clear_copy(0).wait()
                return carry

            lax.fori_loop(0, n_started + n_blocks - n_valid, wait_one, 0)

    base = 0 if ls_blocked else i * (t * TOP_K)
    slot = i % 2

    def scatter(g, carry):
        for u in range(TOKEN_UNROLL):
            tok = g * TOKEN_UNROLL + u
            tile = h_ref[_tile_rows(tok), :]
            for k in range(TOP_K):
                stage[slot, _tile_rows(ls_ref[base + tok * TOP_K + k]), :] = tile
        return carry

    lax.fori_loop(0, t // TOKEN_UNROLL, scatter, 0)
    _start_run_copies(
        cnt_ref, off_ref, dst_ref, i, t,
        lambda loc, glob, n: pltpu.make_async_copy(stage.at[slot, _tile_rows(loc, n), :],
                                                   xs_ref.at[_tile_rows(glob, n), :],
                                                   sems.at[slot]))

    def wait_stage(s):
        pltpu.make_async_copy(stage.at[s], xs_ref.at[_tile_rows(0, t * TOP_K), :], sems.at[s]).wait()

    @pl.when(i > 0)
    def _():
        wait_stage(1 - slot)

    @pl.when(i == pl.num_programs(0) - 1)
    def _():
        wait_stage(slot)


def _dispatch(cnt, off, dst, pad_ends, ls, h2t, xs, n_rows, tile, later_rows=0):
    n_tiles = h2t.shape[0] // (tile * ROW_TILE)
    first = xs is None
    clear_blocks = -(-(later_rows + MOE_ROWS - 1) // MOE_ROWS) if first else 0
    ls_blocked = (tile * TOP_K) % 1024 == 0
    ls_spec = (pl.BlockSpec((tile * TOP_K,), lambda i, *_: (i,), memory_space=pltpu.SMEM)
               if ls_blocked else pl.BlockSpec(memory_space=pltpu.SMEM))
    in_specs = [ls_spec, pl.BlockSpec((tile * ROW_TILE, LANES), lambda i, *_: (i, 0))]
    args = [cnt, off, dst, pad_ends, ls, h2t]
    if not first:
        in_specs.append(pl.BlockSpec(memory_space=pl.ANY))
        args.append(xs)
    grid_spec = pltpu.PrefetchScalarGridSpec(
        num_scalar_prefetch=4,
        grid=(n_tiles,),
        in_specs=in_specs,
        out_specs=pl.BlockSpec(memory_space=pl.ANY),
        scratch_shapes=[pltpu.VMEM((2, tile * TOP_K * ROW_TILE, LANES), F32),
                        pltpu.VMEM((MOE_ROWS * ROW_TILE, LANES), F32),
                        pltpu.SemaphoreType.DMA((2,)),
                        pltpu.SemaphoreType.DMA],
    )
    return pl.pallas_call(
        functools.partial(_dispatch_kernel, clear_blocks=clear_blocks, ls_blocked=ls_blocked),
        grid_spec=grid_spec,
        out_shape=jax.ShapeDtypeStruct((n_rows * ROW_TILE, LANES), F32),
        input_output_aliases={} if first else {6: 0},
        compiler_params=_params(("arbitrary",)),
        name="dispatch",
    )(*args)


def _moe_kernel(be_ref, nv_ref, xs_ref, wgu_ref, bgu_ref, wdn_ref, bdn_ref, ys_ref,
                wgu_bf, wdn_bf):
    i = pl.program_id(0)
    e = be_ref[i]
    prev = be_ref[jnp.maximum(i - 1, 0)]
    rows = 128

    @pl.when((i == 0) | (e != prev))
    def _():
        def cast(r, carry):
            sl = pl.ds(pl.multiple_of(r * rows, rows), rows)
            wgu_bf[sl, :] = wgu_ref[sl, :].astype(BF16)
            wdn_bf[sl, :] = wdn_ref[sl, :].astype(BF16)
            return carry
        lax.fori_loop(0, D_MODEL // rows, cast, 0)

    @pl.when(i < nv_ref[0])
    def _():
        x = _from_row_tiles(xs_ref, MOE_ROWS).astype(BF16)
        y = None
        for c in range(D_FF // FF_CHUNK):
            gcols = slice(c * FF_CHUNK, (c + 1) * FF_CHUNK)
            ucols = slice(D_FF + c * FF_CHUNK, D_FF + (c + 1) * FF_CHUNK)
            gate = jnp.dot(x, wgu_bf[:, gcols], preferred_element_type=F32) + bgu_ref[:, gcols]
            up = jnp.dot(x, wgu_bf[:, ucols], preferred_element_type=F32) + bgu_ref[:, ucols]
            gate = jnp.minimum(gate, SWIGLU_LIMIT)
            up = jnp.clip(up, -SWIGLU_LIMIT, SWIGLU_LIMIT)
            a = gate * jax.nn.sigmoid(SWIGLU_ALPHA * gate) * (up + 1.0)
            part = jnp.dot(a.astype(BF16), wdn_bf[gcols, :], preferred_element_type=F32)
            y = part if y is None else y + part
        _to_row_tiles(ys_ref, y + bdn_ref[...])

    @pl.when(i >= nv_ref[0])
    def _():
        ys_ref[...] = jnp.zeros_like(ys_ref)


def _moe(block_e, n_valid, xs, w_gu, b_gu, w_down, b_down, layer):
    n_rows = xs.shape[0] // ROW_TILE
    n_blocks = n_rows // MOE_ROWS
    grid_spec = pltpu.PrefetchScalarGridSpec(
        num_scalar_prefetch=2,
        grid=(n_blocks,),
        in_specs=[
            pl.BlockSpec((MOE_ROWS * ROW_TILE, LANES),
                         lambda i, be, nv: (jnp.minimum(i, nv[0] - 1), 0)),
            pl.BlockSpec((None, None, D_MODEL, 2 * D_FF), lambda i, be, nv: (layer, be[i], 0, 0)),
            pl.BlockSpec((None, None, 1, 2 * D_FF), lambda i, be, nv: (layer, be[i], 0, 0)),
            pl.BlockSpec((None, None, D_FF, D_MODEL), lambda i, be, nv: (layer, be[i], 0, 0)),
            pl.BlockSpec((None, None, 1, D_MODEL), lambda i, be, nv: (layer, be[i], 0, 0)),
        ],
        out_specs=pl.BlockSpec((MOE_ROWS * ROW_TILE, LANES), lambda i, be, nv: (i, 0)),
        scratch_shapes=[pltpu.VMEM((D_MODEL, 2 * D_FF), BF16),
                        pltpu.VMEM((D_FF, D_MODEL), BF16)],
    )
    return pl.pallas_call(
        _moe_kernel,
        grid_spec=grid_spec,
        out_shape=jax.ShapeDtypeStruct((n_rows * ROW_TILE, LANES), F32),
        compiler_params=_params(("arbitrary",)),
        name="moe",
    )(block_e, n_valid, xs, w_gu, b_gu.reshape(DEPTH, N_EXPERTS, 1, 2 * D_FF),
      w_down, b_down.reshape(DEPTH, N_EXPERTS, 1, D_MODEL))


def _combine_kernel(cnt_ref, off_ref, dst_ref, ls_ref, gate_ref, x1_ref, gf_ref, g2_ref, b2_ref,
                    ys_ref, o_ref, stage, ft, sems, *, ls_blocked):
    i = pl.program_id(0)
    t = x1_ref.shape[0]
    slot = i % 2

    def start_fetch(tile_idx, s):
        _start_run_copies(
            cnt_ref, off_ref, dst_ref, tile_idx, t,
            lambda loc, glob, n: pltpu.make_async_copy(ys_ref.at[_tile_rows(glob, n), :],
                                                       stage.at[s, _tile_rows(loc, n), :],
                                                       sems.at[s]))

    @pl.when(i == 0)
    def _():
        start_fetch(0, 0)

    @pl.when(i + 1 < pl.num_programs(0))
    def _():
        start_fetch(i + 1, 1 - slot)

    pltpu.make_async_copy(ys_ref.at[_tile_rows(0, t * TOP_K), :], stage.at[slot],
                          sems.at[slot]).wait()

    base = 0 if ls_blocked else i * (t * TOP_K)

    def gather(g, carry):
        for u in range(TOKEN_UNROLL):
            tok = g * TOKEN_UNROLL + u
            acc = None
            for k in range(TOP_K):
                j = base + tok * TOP_K + k
                v = stage[slot, _tile_rows(ls_ref[j]), :] * gate_ref[j]
                acc = v if acc is None else acc + v
            ft[_tile_rows(tok), :] = acc
        return carry

    lax.fori_loop(0, t // TOKEN_UNROLL, gather, 0)
    f = _from_row_tiles(ft, t)
    z = DEEPNORM_ALPHA * x1_ref[...] + gf_ref[...] * f
    o_ref[...] = _layer_norm(z) * g2_ref[...] + b2_ref[...]


def _combine(cnt, off, dst, ls, gates, x1, gf, g2, b2, ys, tile, tiles_per_seq):
    n = x1.shape[0]
    ls_blocked = (tile * TOP_K) % 1024 == 0

    def smem_vec():
        if ls_blocked:
            return pl.BlockSpec((tile * TOP_K,), lambda i, *_: (i,), memory_space=pltpu.SMEM)
        return pl.BlockSpec(memory_space=pltpu.SMEM)

    grid_spec = pltpu.PrefetchScalarGridSpec(
        num_scalar_prefetch=3,
        grid=(n // tile,),
        in_specs=[
            smem_vec(), smem_vec(),
            pl.BlockSpec((tile, D_MODEL), lambda i, *_: (i, 0)),
            pl.BlockSpec((None, 1, D_MODEL), lambda i, *_: (i // tiles_per_seq, 0, 0)),
            pl.BlockSpec((1, D_MODEL), lambda i, *_: (0, 0)),
            pl.BlockSpec((1, D_MODEL), lambda i, *_: (0, 0)),
            pl.BlockSpec(memory_space=pl.ANY),
        ],
        out_specs=pl.BlockSpec((tile, D_MODEL), lambda i, *_: (i, 0)),
        scratch_shapes=[pltpu.VMEM((2, tile * TOP_K * ROW_TILE, LANES), F32),
                        pltpu.VMEM((tile * ROW_TILE, LANES), F32),
                        pltpu.SemaphoreType.DMA((2,))],
    )
    return pl.pallas_call(
        functools.partial(_combine_kernel, ls_blocked=ls_blocked),
        grid_spec=grid_spec,
        out_shape=jax.ShapeDtypeStruct((n, D_MODEL), F32),
        compiler_params=_params(("arbitrary",)),
        name="combine",
    )(cnt, off, dst, ls, gates, x1, gf, g2, b2, ys)


def _alibi_slopes():
    return jnp.exp2(-ALIBI_MAX * (jnp.arange(N_HEADS, dtype=F32) + 1.0) / N_HEADS)


def _prompt_bias(slopes):
    r = jnp.arange(Q_TILE)
    j = jnp.arange(WINDOW + Q_TILE)
    dc = (WINDOW + r)[:, None] // CHUNK - j[None, :] // CHUNK
    vis = (dc >= 0) & (dc <= WIN_CHUNKS)
    dist = jnp.abs(r[:, None] + WINDOW - j[None, :]).astype(F32)
    return jnp.where(vis[None], -slopes[:, None, None] * dist[None], NEG_INF)


def _sample_bias(slopes, t, hist):
    qpos = PAST_LEN + jnp.arange(t)
    kpos = PAST_LEN - hist + jnp.arange(hist + t)
    dc = qpos[:, None] // CHUNK - kpos[None, :] // CHUNK
    vis = (dc >= 0) & (dc <= WIN_CHUNKS) & (kpos[None, :] >= 0)
    dist = jnp.abs(qpos[:, None] - kpos[None, :]).astype(F32)
    return jnp.where(vis[None], -slopes[:, None, None] * dist[None], NEG_INF)


def _dup_heads(w):
    lead = w.shape[:-1]
    w4 = w.reshape(lead + (N_KV, HEAD_DIM))
    return jnp.concatenate([w4, w4], axis=-1).reshape(lead + (KD_DIM,))


def _proj_weights(w_in, b_in):
    q0, k0, v0, u0 = 0, ATT_DIM, ATT_DIM + KV_DIM, ATT_DIM + 2 * KV_DIM
    w = jnp.concatenate([w_in[:, q0:k0] * ATTN_SCALE, _dup_heads(w_in[:, k0:v0]),
                         _dup_heads(w_in[:, v0:u0]), w_in[:, u0:]], axis=1)
    b = jnp.concatenate([b_in[q0:k0] * ATTN_SCALE, _dup_heads(b_in[k0:v0]),
                         _dup_heads(b_in[v0:u0]), b_in[u0:]], axis=0)
    return w.astype(BF16), b.reshape(1, PROJ_COLS)


def _pool_weight(w_pool):
    n = len(POOL_WINDOWS)
    eye = jnp.eye(n, dtype=w_pool.dtype)
    return jnp.einsum('gcd,gh->gchd', w_pool, eye).reshape(POOL_DIM, POOL_DIM).astype(BF16)


def _undup(kd, nb, rows):
    return kd[:, -rows:].reshape(nb, rows, N_KV, LANES)[..., :HEAD_DIM].astype(F32)


def kernel(x_prompt, x_sample, c_prompt, c_sample, cache_k, cache_v, state_pool, w_ada, b_ada,
           w_in, b_in, sinks, w_pool, pool_scale, w_out, b_out, ln1_g, ln1_b, ln2_g, ln2_b,
           w_router, b_router, w_gu, b_gu, w_down, b_down):
    nbp, seq, _ = x_prompt.shape
    nbs, tdec, _ = x_sample.shape
    hist = cache_k.shape[2]
    n_p, n_s = nbp * seq, nbs * tdec
    n_asg = (n_p + n_s) * TOP_K
    n_blocks = -(-(n_asg + N_EXPERTS * (MOE_ROWS - 1)) // MOE_ROWS)
    n_rows = n_blocks * MOE_ROWS
    assert hist == WINDOW and tdec >= POOL_PAD and seq % TOK_TILE == 0

    nb_all = nbp + nbs
    nb_pad = -(-nb_all // SUBLANES) * SUBLANES
    c_all = jnp.concatenate([c_prompt, c_sample, jnp.zeros((nb_pad - nb_all, D_MODEL), F32)], 0)
    mod_all = _adaln(c_all, w_ada, b_ada).reshape(DEPTH, nb_pad, 6, D_MODEL)

    slopes = _alibi_slopes()
    bias_p = _prompt_bias(slopes)
    bias_s = _sample_bias(slopes, tdec, hist)
    row = lambda v: v.reshape(1, -1)

    yp, ys_tok = x_prompt, x_sample
    outs = {k: [] for k in ("kp", "vp", "pp", "ks", "vs", "ps")}
    for l in range(DEPTH):
        mod = mod_all[l]
        w_proj, b_proj = _proj_weights(w_in[l], b_in[l])
        wpool = _pool_weight(w_pool[l])
        pscale = row(pool_scale[l])
        wout = w_out[l].astype(BF16)
        wr_f32 = jnp.pad(w_router[l], ((0, 0), (0, LANES - N_EXPERTS)))
        wr_hi = wr_f32.astype(BF16)
        wr = jnp.concatenate([wr_hi, (wr_f32 - wr_hi.astype(F32)).astype(BF16)], axis=1)
        br = jnp.concatenate([b_router[l], jnp.full((LANES - N_EXPERTS,), NEG_INF, F32)]).reshape(1, LANES)

        q, kd, vd, u = _inproj(yp, mod, w_proj, b_proj, TOK_TILE, 0)
        mix_p = _mixer(sinks[l], q, kd, vd, u, None, None, None, bias_p, wpool, pscale,
                       tile=TOK_TILE, tq=Q_TILE, history_starts_empty=True, pos0=0)
        outs["kp"].append(_undup(kd, nbp, WINDOW))
        outs["vp"].append(_undup(vd, nbp, WINDOW))
        outs["pp"].append(u[:, -POOL_PAD:])

        qs, kds, vds, us = _inproj(ys_tok, mod, w_proj, b_proj, tdec, nbp)
        ck = _dup_heads(cache_k[l].reshape(nbs, hist, KV_DIM)).astype(BF16)
        cv = _dup_heads(cache_v[l].reshape(nbs, hist, KV_DIM)).astype(BF16)
        sp = jnp.pad(state_pool[l], ((0, 0), (POOL_HALO - POOL_PAD, 0), (0, 0)))
        mix_s = _mixer(sinks[l], qs, kds, vds, us, ck, cv, sp, bias_s, wpool, pscale,
                       tile=tdec, tq=tdec, history_starts_empty=False, pos0=PAST_LEN)
        k_new = _undup(kds, nbs, tdec)
        v_new = _undup(vds, nbs, tdec)
        outs["ks"].append(jnp.concatenate([cache_k[l], k_new], axis=1)[:, -hist:])
        outs["vs"].append(jnp.concatenate([cache_v[l], v_new], axis=1)[:, -hist:])
        outs["ps"].append(us[:, -POOL_PAD:])

        post_w = (wout, row(b_out[l]), row(ln1_g[l]), row(ln1_b[l]), wr, br)
        zero_counts = jnp.zeros((SUBLANES, LANES), F32)
        x1p, h2tp, rlp, rgp, meta_p, cnt_p = _post(mix_p, yp, mod, *post_w, zero_counts, TOK_TILE, 0)
        x1s, h2ts, rls, rgs, meta_s, cnt = _post(mix_s, ys_tok, mod, *post_w, cnt_p, tdec, nbp)

        counts = cnt[0, :N_EXPERTS].astype(jnp.int32)
        padded = (counts + MOE_ROWS - 1) // MOE_ROWS * MOE_ROWS
        pad_ends = jnp.cumsum(padded).astype(jnp.int32)
        pad_starts = pad_ends - padded
        block_start = jnp.arange(n_blocks, dtype=jnp.int32) * MOE_ROWS
        block_e = jnp.minimum(jnp.sum(pad_ends[None, :] <= block_start[:, None], axis=1),
                              N_EXPERTS - 1).astype(jnp.int32)
        n_valid = (pad_ends[-1:] // MOE_ROWS).astype(jnp.int32)

        def tables(meta):
            m = meta.reshape(-1, SUBLANES, LANES)[:, :, :N_EXPERTS]
            flat = lambda a: a.reshape(-1).astype(jnp.int32)
            return flat(m[:, 0]), flat(m[:, 1]), flat(pad_starts[None, :] + m[:, 2])

        def per_assignment(r, n):
            return r.reshape(n, LANES)[:, :TOP_K].reshape(n * TOP_K)

        tab_p, tab_s = tables(meta_p), tables(meta_s)
        ls_p, ls_s = per_assignment(rlp, n_p), per_assignment(rls, n_s)

        xs = _dispatch(*tab_p, pad_ends, ls_p, h2tp.reshape(n_p * ROW_TILE, LANES), None,
                       n_rows, TOK_TILE, later_rows=n_s)
        xs = _dispatch(*tab_s, pad_ends, ls_s, h2ts.reshape(n_s * ROW_TILE, LANES), xs,
                       n_rows, tdec)
        ye = _moe(block_e, n_valid, xs, w_gu, b_gu, w_down, b_down, l)

        g2, b2 = row(ln2_g[l]), row(ln2_b[l])
        gf = mod[:, 5].reshape(nb_pad, 1, D_MODEL)
        yp = _combine(*tab_p, ls_p, per_assignment(rgp, n_p), x1p.reshape(n_p, D_MODEL),
                      gf[:nbp], g2, b2, ye, TOK_TILE, seq // TOK_TILE).reshape(nbp, seq, D_MODEL)
        ys_tok = _combine(*tab_s, ls_s, per_assignment(rgs, n_s), x1s.reshape(n_s, D_MODEL),
                          gf[nbp:nb_all], g2, b2, ye, tdec, 1).reshape(nbs, tdec, D_MODEL)

    st = lambda k: jnp.stack(outs[k])
    return (yp, ys_tok, st("kp"), st("vp"), st("pp"), st("ks"), st("vs"), st("ps"))
```

```python
import functools

import jax
import jax.numpy as jnp
from jax import lax
from jax.experimental import pallas as pl
from jax.experimental.pallas import tpu as pltpu

F32 = jnp.float32
BF16 = jnp.bfloat16

D_MODEL = 1024
DEPTH = 2
CHUNK = 64
WINDOW = 128
WIN_CHUNKS = WINDOW // CHUNK
HEAD_DIM = 64
ATT_DIM = 768
N_HEADS = 12
N_KV = 4
GROUP = N_HEADS // N_KV
KV_DIM = N_KV * HEAD_DIM
ATTN_SCALE = HEAD_DIM ** -0.5
ALIBI_MAX = 8.0
NEG_INF = -1e30
POOL_DIM = D_MODEL - ATT_DIM
POOL_WINDOWS = (2, 4, 8, 16)
POOL_CH = POOL_DIM // len(POOL_WINDOWS)
POOL_PAD = max(POOL_WINDOWS) - 1
N_EXPERTS = 32
TOP_K = 4
D_FF = D_MODEL
SWIGLU_LIMIT = 7.0
SWIGLU_ALPHA = 1.702
LN_EPS = 1e-5
DEEPNORM_ALPHA = (2.0 * DEPTH) ** 0.25
PAST_LEN = 1024

LANES = 128
SUBLANES = 8
VMEM_LIMIT = 52 * 1024 * 1024

TOK_TILE = 512
Q_TILE = 128
POOL_HALO = 16
KD_DIM = N_KV * LANES
PROJ_COLS = ATT_DIM + 2 * KD_DIM + POOL_DIM
MOE_ROWS = 512
FF_CHUNK = 1024
TOKEN_UNROLL = 8
ROW_TILE = D_MODEL // LANES
assert ROW_TILE == SUBLANES


def _params(sem, vmem=VMEM_LIMIT):
    return pltpu.CompilerParams(dimension_semantics=sem, vmem_limit_bytes=vmem)


def _layer_norm(x):
    mu = jnp.mean(x, axis=-1, keepdims=True)
    xc = x - mu
    var = jnp.mean(xc * xc, axis=-1, keepdims=True)
    return xc * lax.rsqrt(var + LN_EPS)


def _adaln_kernel(c_ref, w_ref, b_ref, o_ref):
    c = c_ref[...]
    s = c * jax.nn.sigmoid(c)
    o_ref[...] = jnp.dot(s, w_ref[...], preferred_element_type=F32,
                         precision=lax.Precision.HIGHEST) + b_ref[...]


def _adaln(c_all, w_ada, b_ada):
    nb = c_all.shape[0]
    ncol = 6 * D_MODEL
    tn = D_MODEL
    return pl.pallas_call(
        _adaln_kernel,
        grid=(DEPTH, ncol // tn),
        in_specs=[
            pl.BlockSpec((nb, D_MODEL), lambda l, j: (0, 0)),
            pl.BlockSpec((None, D_MODEL, tn), lambda l, j: (l, 0, j)),
            pl.BlockSpec((None, 1, tn), lambda l, j: (l, 0, j)),
        ],
        out_specs=pl.BlockSpec((None, nb, tn), lambda l, j: (l, 0, j)),
        out_shape=jax.ShapeDtypeStruct((DEPTH, nb, ncol), F32),
        compiler_params=_params(("parallel", "parallel")),
        name="adaln",
    )(c_all, w_ada, b_ada.reshape(DEPTH, 1, ncol))


def _inproj_kernel(x_ref, mod_ref, w_ref, b_ref, q_ref, kd_ref, vd_ref, u_ref):
    h = _layer_norm(x_ref[...]) * (1.0 + mod_ref[1:2, :]) + mod_ref[0:1, :]
    p = jnp.dot(h.astype(BF16), w_ref[...], preferred_element_type=F32) + b_ref[...]
    q_ref[...] = p[:, :ATT_DIM].astype(BF16)
    kd_ref[...] = p[:, ATT_DIM:ATT_DIM + KD_DIM].astype(BF16)
    vd_ref[...] = p[:, ATT_DIM + KD_DIM:ATT_DIM + 2 * KD_DIM].astype(BF16)
    u_ref[...] = p[:, ATT_DIM + 2 * KD_DIM:]


def _inproj(x, mod, w, b, tile, mod_row0):
    nb, s, _ = x.shape
    nt = s // tile

    def tok(c):
        return pl.BlockSpec((None, tile, c), lambda bi, i: (bi, i, 0))

    return pl.pallas_call(
        _inproj_kernel,
        grid=(nb, nt),
        in_specs=[
            tok(D_MODEL),
            pl.BlockSpec((None, 6, D_MODEL), lambda bi, i: (bi + mod_row0, 0, 0)),
            pl.BlockSpec((D_MODEL, PROJ_COLS), lambda bi, i: (0, 0)),
            pl.BlockSpec((1, PROJ_COLS), lambda bi, i: (0, 0)),
        ],
        out_specs=[tok(ATT_DIM), tok(KD_DIM), tok(KD_DIM), tok(POOL_DIM)],
        out_shape=[
            jax.ShapeDtypeStruct((nb, s, ATT_DIM), BF16),
            jax.ShapeDtypeStruct((nb, s, KD_DIM), BF16),
            jax.ShapeDtypeStruct((nb, s, KD_DIM), BF16),
            jax.ShapeDtypeStruct((nb, s, POOL_DIM), F32),
        ],
        compiler_params=_params(("parallel", "parallel")),
        name="inproj",
    )(x, mod, w, b)


def _mixer_kernel(sinks_ref, q_ref, kd_ref, kdh_ref, vd_ref, vdh_ref, u_ref, uh_ref,
                  bias_ref, wpool_ref, pscale_ref, o_ref, kall_ref, vall_ref,
                  *, tq, history_starts_empty, pos0):
    i = pl.program_id(1)
    t = q_ref.shape[0]
    hk = kdh_ref.shape[0]
    nk = hk + tq
    kall_ref[0:hk, :] = kdh_ref[...]
    kall_ref[hk:, :] = kd_ref[...]
    vall_ref[0:hk, :] = vdh_ref[...]
    vall_ref[hk:, :] = vd_ref[...]

    low_half = lax.broadcasted_iota(jnp.int32, (tq, LANES), 1) < HEAD_DIM
    if history_starts_empty:
        col = lax.broadcasted_iota(jnp.int32, (1, nk), 1)
        no_history = jnp.where((col < hk) & (i == 0), NEG_INF, 0.0).astype(F32)

    for s in range(t // tq):
        rows = slice(s * tq, (s + 1) * tq)
        keys = slice(s * tq, s * tq + nk)
        for pair in range(N_HEADS // 2):
            q2 = q_ref[rows, pair * LANES:(pair + 1) * LANES]
            halves = []
            for half in range(2):
                head = 2 * pair + half
                kv = head // GROUP
                qm = jnp.where(low_half if half == 0 else ~low_half, q2, jnp.zeros_like(q2))
                kd = kall_ref[keys, kv * LANES:(kv + 1) * LANES]
                sc = lax.dot_general(qm, kd, (((1,), (1,)), ((), ())),
                                     preferred_element_type=F32)
                sc = sc + bias_ref[head]
                if history_starts_empty and s == 0:
                    sc = sc + no_history
                sink = sinks_ref[head]
                m = jnp.maximum(jnp.max(sc, axis=1, keepdims=True), sink)
                p = jnp.exp(sc - m)
                den = jnp.sum(p, axis=1, keepdims=True) + jnp.exp(sink - m)
                vd = vall_ref[keys, kv * LANES:(kv + 1) * LANES]
                o2 = jnp.dot(p.astype(BF16), vd, preferred_element_type=F32)
                halves.append(o2 / den)
            o_ref[rows, pair * LANES:(pair + 1) * LANES] = jnp.where(
                low_half, halves[0], halves[1]).astype(BF16)

    uh = uh_ref[...]
    if history_starts_empty:
        uh = jnp.where(i == 0, 0.0, uh)
    u = u_ref[...]
    ue = jnp.concatenate([uh, u], axis=0)
    s2 = ue + pltpu.roll(ue, 1, 0)
    s4 = s2 + pltpu.roll(s2, 2, 0)
    s8 = s4 + pltpu.roll(s4, 4, 0)
    s16 = s8 + pltpu.roll(s8, 8, 0)
    lane = lax.broadcasted_iota(jnp.int32, (1, POOL_DIM), 1)
    g0, g1, g2 = lane < POOL_CH, lane < 2 * POOL_CH, lane < 3 * POOL_CH
    wsum = jnp.where(g0, s2[POOL_HALO:], jnp.where(g1, s4[POOL_HALO:],
                     jnp.where(g2, s8[POOL_HALO:], s16[POOL_HALO:])))
    width = jnp.where(g0, 2.0, jnp.where(g1, 4.0, jnp.where(g2, 8.0, 16.0))).astype(F32)
    pos = (pos0 + i * t + lax.broadcasted_iota(jnp.int32, (t, 1), 0)).astype(F32)
    cnt = jnp.minimum(width, pos + 1.0)
    pooled = wsum / cnt - u
    mixed = jnp.dot(pooled.astype(BF16), wpool_ref[...], preferred_element_type=F32)
    o_ref[:, ATT_DIM:] = (mixed * pscale_ref[...]).astype(BF16)


def _mixer(sinks, q, kd, vd, u, k_hist, v_hist, u_hist, bias, wpool, pscale, *,
           tile, tq, history_starts_empty, pos0):
    nb, s, _ = q.shape
    nt = s // tile
    hk = WINDOW
    own_history = k_hist is None
    if own_history:
        k_hist, v_hist, u_hist = kd, vd, u
        kh_map = lambda bi, i: (bi, jnp.maximum(i * (tile // hk) - 1, 0), 0)
        uh_map = lambda bi, i: (bi, jnp.maximum(i * (tile // POOL_HALO) - 1, 0), 0)
    else:
        kh_map = lambda bi, i: (bi, 0, 0)
        uh_map = lambda bi, i: (bi, 0, 0)

    def tok(c):
        return pl.BlockSpec((None, tile, c), lambda bi, i: (bi, i, 0))

    kern = functools.partial(_mixer_kernel, tq=tq,
                             history_starts_empty=history_starts_empty, pos0=pos0)
    return pl.pallas_call(
        kern,
        grid=(nb, nt),
        in_specs=[
            pl.BlockSpec(memory_space=pltpu.SMEM),
            tok(ATT_DIM),
            tok(KD_DIM),
            pl.BlockSpec((None, hk, KD_DIM), kh_map),
            tok(KD_DIM),
            pl.BlockSpec((None, hk, KD_DIM), kh_map),
            tok(POOL_DIM),
            pl.BlockSpec((None, POOL_HALO, POOL_DIM), uh_map),
            pl.BlockSpec(bias.shape, lambda bi, i: (0, 0, 0)),
            pl.BlockSpec((POOL_DIM, POOL_DIM), lambda bi, i: (0, 0)),
            pl.BlockSpec((1, POOL_DIM), lambda bi, i: (0, 0)),
        ],
        out_specs=tok(D_MODEL),
        out_shape=jax.ShapeDtypeStruct((nb, s, D_MODEL), BF16),
        scratch_shapes=[pltpu.VMEM((hk + tile, KD_DIM), BF16),
                        pltpu.VMEM((hk + tile, KD_DIM), BF16)],
        compiler_params=_params(("parallel", "parallel")),
        name="mixer",
    )(sinks, q, kd, k_hist, vd, v_hist, u, u_hist, bias, wpool, pscale)


def _to_row_tiles(ref, x):
    t = x.shape[0]
    for s in range(ROW_TILE):
        ref[pl.ds(s, t, stride=ROW_TILE), :] = x[:, s * LANES:(s + 1) * LANES]


def _from_row_tiles(ref, t):
    return jnp.concatenate([ref[pl.ds(s, t, stride=ROW_TILE), :] for s in range(ROW_TILE)], axis=1)


def _post_kernel(mix_ref, x_ref, mod_ref, wout_ref, bout_ref, g1_ref, b1_ref, wr_ref, br_ref,
                 cin_ref, x1_ref, h2t_ref, rl_ref, rg_ref, meta_ref, cnt_ref):
    first = (pl.program_id(0) == 0) & (pl.program_id(1) == 0)

    @pl.when(first)
    def _():
        cnt_ref[...] = cin_ref[...]

    t = x_ref.shape[0]
    mix = jnp.dot(mix_ref[...], wout_ref[...], preferred_element_type=F32) + bout_ref[...]
    z = DEEPNORM_ALPHA * x_ref[...] + mod_ref[2:3, :] * mix
    x1 = _layer_norm(z) * g1_ref[...] + b1_ref[...]
    x1_ref[...] = x1
    h2 = _layer_norm(x1) * (1.0 + mod_ref[4:5, :]) + mod_ref[3:4, :]
    _to_row_tiles(h2t_ref, h2)

    h2_hi = h2.astype(BF16)
    h2_lo = (h2 - h2_hi.astype(F32)).astype(BF16)
    contract_last = (((1,), (1,)), ((), ()))
    d_hi = lax.dot_general(wr_ref[...], h2_hi, contract_last, preferred_element_type=F32)
    d_lo = lax.dot_general(wr_ref[0:N_EXPERTS, :], h2_lo, contract_last,
                           preferred_element_type=F32)
    logits = d_hi[:N_EXPERTS] + d_hi[N_EXPERTS:] + d_lo + br_ref[...]
    expert = lax.broadcasted_iota(jnp.int32, (N_EXPERTS, t), 0).astype(F32)
    vals, idxs, hots = [], [], []
    cur = logits
    for _ in range(TOP_K):
        mk = jnp.max(cur, axis=0, keepdims=True)
        ik = jnp.min(jnp.where(cur == mk, expert, float(N_EXPERTS)), axis=0, keepdims=True)
        hot = expert == ik
        cur = jnp.where(hot, -jnp.inf, cur)
        vals.append(mk)
        idxs.append(ik)
        hots.append(hot)
    exps = [jnp.exp(v - vals[0]) for v in vals]
    den = exps[0] + exps[1] + exps[2] + exps[3]
    gates = [e / den for e in exps]

    onehot = jnp.zeros((N_EXPERTS, t), F32)
    for hot in hots:
        onehot = onehot + hot.astype(F32)
    onehot_bf = onehot.astype(BF16)
    tok_r = lax.broadcasted_iota(jnp.int32, (t, t), 0)
    tok_c = lax.broadcasted_iota(jnp.int32, (t, t), 1)
    earlier = jnp.where(tok_r < tok_c, 1.0, 0.0).astype(BF16)
    rank = jnp.dot(onehot_bf, earlier, preferred_element_type=F32)
    exp_r = lax.broadcasted_iota(jnp.int32, (N_EXPERTS, N_EXPERTS), 0)
    exp_c = lax.broadcasted_iota(jnp.int32, (N_EXPERTS, N_EXPERTS), 1)
    lower = jnp.where(exp_c < exp_r, 1.0, 0.0).astype(BF16)
    tile_off = jnp.sum(jnp.dot(lower, onehot_bf, preferred_element_type=F32),
                       axis=1, keepdims=True)
    tile_cnt = jnp.sum(onehot, axis=1, keepdims=True)
    local = (rank + tile_off) * float(ROW_TILE)
    slots = [jnp.sum(jnp.where(hot, local, 0.0), axis=0, keepdims=True) for hot in hots]

    rl_ref[...] = jnp.concatenate(slots + idxs, axis=0).astype(jnp.int32)
    rg_ref[...] = jnp.concatenate(gates + [jnp.zeros_like(g) for g in gates], axis=0)
    lane = lax.broadcasted_iota(jnp.int32, (N_EXPERTS, LANES), 1)
    before = cnt_ref[...]
    meta = jnp.where(lane == 0, tile_cnt, jnp.where(lane == 1, tile_off,
                     jnp.where(lane == 2, before, 0.0)))
    meta_ref[...] = meta.astype(jnp.int32)
    cnt_ref[...] = before + tile_cnt


def _post(mix, x, mod, wout, bout, g1, b1, wr, br, counts_in, tile, mod_row0):
    nb, s, _ = x.shape
    nt = s // tile

    def tok(c):
        return pl.BlockSpec((None, tile, c), lambda bi, i: (bi, i, 0))

    def whole(shape):
        return pl.BlockSpec(shape, lambda bi, i: tuple(0 for _ in shape))

    return pl.pallas_call(
        _post_kernel,
        grid=(nb, nt),
        in_specs=[
            tok(D_MODEL), tok(D_MODEL),
            pl.BlockSpec((None, 6, D_MODEL), lambda bi, i: (bi + mod_row0, 0, 0)),
            whole((D_MODEL, D_MODEL)), whole((1, D_MODEL)),
            whole((1, D_MODEL)), whole((1, D_MODEL)),
            whole((2 * N_EXPERTS, D_MODEL)), whole((N_EXPERTS, 1)),
            whole((N_EXPERTS, LANES)),
        ],
        out_specs=[tok(D_MODEL),
                   pl.BlockSpec((None, tile * ROW_TILE, LANES), lambda bi, i: (bi, i, 0)),
                   pl.BlockSpec((None, None, 2 * TOP_K, tile), lambda bi, i: (bi, i, 0, 0)),
                   pl.BlockSpec((None, None, 2 * TOP_K, tile), lambda bi, i: (bi, i, 0, 0)),
                   pl.BlockSpec((None, None, N_EXPERTS, LANES), lambda bi, i: (bi, i, 0, 0)),
                   whole((N_EXPERTS, LANES))],
        out_shape=[
            jax.ShapeDtypeStruct((nb, s, D_MODEL), F32),
            jax.ShapeDtypeStruct((nb, s * ROW_TILE, LANES), F32),
            jax.ShapeDtypeStruct((nb, nt, 2 * TOP_K, tile), jnp.int32),
            jax.ShapeDtypeStruct((nb, nt, 2 * TOP_K, tile), F32),
            jax.ShapeDtypeStruct((nb, nt, N_EXPERTS, LANES), jnp.int32),
            jax.ShapeDtypeStruct((N_EXPERTS, LANES), F32),
        ],
        compiler_params=_params(("arbitrary", "arbitrary")),
        name="post",
    )(mix, x, mod, wout, bout, g1, b1, wr, br, counts_in)


def _tile_rows(row, n=1):
    return pl.ds(pl.multiple_of(row * ROW_TILE, ROW_TILE), n * ROW_TILE)


def _rows_at(first_row):
    return pl.ds(pl.multiple_of(first_row, ROW_TILE), ROW_TILE)


def _start_run_copies(cnt_ref, off_ref, dst_ref, tile_idx, t, make_copy):
    n_bits = t.bit_length()

    def body(e, carry):
        j = tile_idx * N_EXPERTS + e
        c, off, dst = cnt_ref[j], off_ref[j], dst_ref[j]

        def piece(b):
            size = 1 << b

            @pl.when(((c >> b) & 1) == 1)
            def _():
                done = c & (size - 1)
                make_copy(off + done, dst + done, size).start()

        for b in range(n_bits):
            piece(b)
        return carry

    lax.fori_loop(0, N_EXPERTS, body, 0)


def _dispatch_kernel(cnt_ref, off_ref, dst_ref, pe_ref, ls_ref, h_ref, *rest,
                     clear_blocks, ls_blocked):
    xs_ref, stage, zbuf, sems, sem = rest[-5:]
    i = pl.program_id(0)
    t = h_ref.shape[0] // ROW_TILE

    if clear_blocks:
        @pl.when(i == 0)
        def _():
            zbuf[...] = jnp.zeros_like(zbuf)

            def clear_copy(row):
                return pltpu.make_async_copy(zbuf, xs_ref.at[_tile_rows(row, MOE_ROWS), :], sem)

            def clear_segment_end(e, n):
                end = pe_ref[e]
                start = jnp.where(e == 0, 0, pe_ref[jnp.maximum(e - 1, 0)])
                for back in range(1, clear_blocks + 1):
                    row = end - back * MOE_ROWS

                    @pl.when(row >= start)
                    def _():
                        clear_copy(row).start()
                    n = n + (row >= start).astype(jnp.int32)
                return n

            n_started = lax.fori_loop(0, N_EXPERTS, clear_segment_end, 0)
            n_blocks = xs_ref.shape[0] // (MOE_ROWS * ROW_TILE)
            n_valid = pe_ref[N_EXPERTS - 1] // MOE_ROWS

            def clear_past_end(b, carry):
                clear_copy(b * MOE_ROWS).start()
                return carry

            lax.fori_loop(n_valid, n_blocks, clear_past_end, 0)

            def wait_one(_, carry):
                clear_copy(0).wait()
                return carry

            lax.fori_loop(0, n_started + n_blocks - n_valid, wait_one, 0)

    base = 0 if ls_blocked else i * (t * TOP_K)
    slot = i % 2

    def permute_and_send(s):
        def scatter(g, carry):
            for u in range(TOKEN_UNROLL):
                tok = g * TOKEN_UNROLL + u
                tile = h_ref[_tile_rows(tok), :]
                for k in range(TOP_K):
                    stage[s, _rows_at(ls_ref[base + k * t + tok]), :] = tile
            return carry

        lax.fori_loop(0, t // TOKEN_UNROLL, scatter, 0)
        _start_run_copies(
            cnt_ref, off_ref, dst_ref, i, t,
            lambda loc, glob, n: pltpu.make_async_copy(stage.at[s, _tile_rows(loc, n), :],
                                                       xs_ref.at[_tile_rows(glob, n), :],
                                                       sems.at[s]))

    for s in range(2):
        pl.when(slot == s)(functools.partial(permute_and_send, s))

    def wait_stage(s):
        pltpu.make_async_copy(stage.at[s], xs_ref.at[_tile_rows(0, t * TOP_K), :], sems.at[s]).wait()

    @pl.when(i > 0)
    def _():
        wait_stage(1 - slot)

    @pl.when(i == pl.num_programs(0) - 1)
    def _():
        wait_stage(slot)


def _dispatch(cnt, off, dst, pad_ends, ls, h2t, xs, n_rows, tile, later_rows=0):
    n_tiles = h2t.shape[0] // (tile * ROW_TILE)
    first = xs is None
    clear_blocks = -(-(later_rows + MOE_ROWS - 1) // MOE_ROWS) if first else 0
    ls_blocked = (tile * TOP_K) % 1024 == 0
    ls_spec = (pl.BlockSpec((tile * TOP_K,), lambda i, *_: (i,), memory_space=pltpu.SMEM)
               if ls_blocked else pl.BlockSpec(memory_space=pltpu.SMEM))
    in_specs = [ls_spec, pl.BlockSpec((tile * ROW_TILE, LANES), lambda i, *_: (i, 0))]
    args = [cnt, off, dst, pad_ends, ls, h2t]
    if not first:
        in_specs.append(pl.BlockSpec(memory_space=pl.ANY))
        args.append(xs)
    grid_spec = pltpu.PrefetchScalarGridSpec(
        num_scalar_prefetch=4,
        grid=(n_tiles,),
        in_specs=in_specs,
        out_specs=pl.BlockSpec(memory_space=pl.ANY),
        scratch_shapes=[pltpu.VMEM((2, tile * TOP_K * ROW_TILE, LANES), F32),
                        pltpu.VMEM((MOE_ROWS * ROW_TILE, LANES), F32),
                        pltpu.SemaphoreType.DMA((2,)),
                        pltpu.SemaphoreType.DMA],
    )
    return pl.pallas_call(
        functools.partial(_dispatch_kernel, clear_blocks=clear_blocks, ls_blocked=ls_blocked),
        grid_spec=grid_spec,
        out_shape=jax.ShapeDtypeStruct((n_rows * ROW_TILE, LANES), F32),
        input_output_aliases={} if first else {6: 0},
        compiler_params=_params(("arbitrary",)),
        name="dispatch",
    )(*args)


def _moe_kernel(be_ref, nv_ref, xs_ref, wgu_ref, bgu_ref, wdn_ref, bdn_ref, ys_ref,
                wgu_bf, wdn_bf):
    i = pl.program_id(0)
    e = be_ref[i]
    prev = be_ref[jnp.maximum(i - 1, 0)]
    rows = 128

    @pl.when((i == 0) | (e != prev))
    def _():
        def cast(r, carry):
            sl = pl.ds(pl.multiple_of(r * rows, rows), rows)
            wgu_bf[sl, :] = wgu_ref[sl, :].astype(BF16)
            wdn_bf[sl, :] = wdn_ref[sl, :].astype(BF16)
            return carry
        lax.fori_loop(0, D_MODEL // rows, cast, 0)

    @pl.when(i < nv_ref[0])
    def _():
        x = _from_row_tiles(xs_ref, MOE_ROWS).astype(BF16)
        y = None
        for c in range(D_FF // FF_CHUNK):
            gcols = slice(c * FF_CHUNK, (c + 1) * FF_CHUNK)
            ucols = slice(D_FF + c * FF_CHUNK, D_FF + (c + 1) * FF_CHUNK)
            gate = jnp.dot(x, wgu_bf[:, gcols], preferred_element_type=F32) + bgu_ref[:, gcols]
            up = jnp.dot(x, wgu_bf[:, ucols], preferred_element_type=F32) + bgu_ref[:, ucols]
            gate = jnp.minimum(gate, SWIGLU_LIMIT)
            up = jnp.clip(up, -SWIGLU_LIMIT, SWIGLU_LIMIT)
            a = gate * jax.nn.sigmoid(SWIGLU_ALPHA * gate) * (up + 1.0)
            part = jnp.dot(a.astype(BF16), wdn_bf[gcols, :], preferred_element_type=F32)
            y = part if y is None else y + part
        _to_row_tiles(ys_ref, y + bdn_ref[...])

    @pl.when(i >= nv_ref[0])
    def _():
        ys_ref[...] = jnp.zeros_like(ys_ref)


def _moe(block_e, n_valid, xs, w_gu, b_gu, w_down, b_down, layer):
    n_rows = xs.shape[0] // ROW_TILE
    n_blocks = n_rows // MOE_ROWS
    grid_spec = pltpu.PrefetchScalarGridSpec(
        num_scalar_prefetch=2,
        grid=(n_blocks,),
        in_specs=[
            pl.BlockSpec((MOE_ROWS * ROW_TILE, LANES),
                         lambda i, be, nv: (jnp.minimum(i, nv[0] - 1), 0)),
            pl.BlockSpec((None, None, D_MODEL, 2 * D_FF), lambda i, be, nv: (layer, be[i], 0, 0)),
            pl.BlockSpec((None, None, 1, 2 * D_FF), lambda i, be, nv: (layer, be[i], 0, 0)),
            pl.BlockSpec((None, None, D_FF, D_MODEL), lambda i, be, nv: (layer, be[i], 0, 0)),
            pl.BlockSpec((None, None, 1, D_MODEL), lambda i, be, nv: (layer, be[i], 0, 0)),
        ],
        out_specs=pl.BlockSpec((MOE_ROWS * ROW_TILE, LANES), lambda i, be, nv: (i, 0)),
        scratch_shapes=[pltpu.VMEM((D_MODEL, 2 * D_FF), BF16),
                        pltpu.VMEM((D_FF, D_MODEL), BF16)],
    )
    return pl.pallas_call(
        _moe_kernel,
        grid_spec=grid_spec,
        out_shape=jax.ShapeDtypeStruct((n_rows * ROW_TILE, LANES), F32),
        compiler_params=_params(("arbitrary",)),
        name="moe",
    )(block_e, n_valid, xs, w_gu, b_gu.reshape(DEPTH, N_EXPERTS, 1, 2 * D_FF),
      w_down, b_down.reshape(DEPTH, N_EXPERTS, 1, D_MODEL))


def _combine_kernel(cnt_ref, off_ref, dst_ref, ls_ref, gate_ref, x1_ref, gf_ref, g2_ref, b2_ref,
                    ys_ref, o_ref, stage, ft, sems, *, ls_blocked):
    i = pl.program_id(0)
    t = x1_ref.shape[0]
    slot = i % 2

    def start_fetch(tile_idx, s):
        _start_run_copies(
            cnt_ref, off_ref, dst_ref, tile_idx, t,
            lambda loc, glob, n: pltpu.make_async_copy(ys_ref.at[_tile_rows(glob, n), :],
                                                       stage.at[s, _tile_rows(loc, n), :],
                                                       sems.at[s]))

    @pl.when(i == 0)
    def _():
        start_fetch(0, 0)

    @pl.when(i + 1 < pl.num_programs(0))
    def _():
        start_fetch(i + 1, 1 - slot)

    pltpu.make_async_copy(ys_ref.at[_tile_rows(0, t * TOP_K), :], stage.at[slot],
                          sems.at[slot]).wait()

    base = 0 if ls_blocked else i * (t * TOP_K)

    def gate_sum(s):
        def gather(g, carry):
            for u in range(TOKEN_UNROLL):
                tok = g * TOKEN_UNROLL + u
                acc = None
                for k in range(TOP_K):
                    j = base + k * t + tok
                    v = stage[s, _rows_at(ls_ref[j]), :] * gate_ref[j]
                    acc = v if acc is None else acc + v
                ft[_tile_rows(tok), :] = acc
            return carry

        lax.fori_loop(0, t // TOKEN_UNROLL, gather, 0)

    for s in range(2):
        pl.when(slot == s)(functools.partial(gate_sum, s))
    f = _from_row_tiles(ft, t)
    z = DEEPNORM_ALPHA * x1_ref[...] + gf_ref[...] * f
    o_ref[...] = _layer_norm(z) * g2_ref[...] + b2_ref[...]


def _combine(cnt, off, dst, ls, gates, x1, gf, g2, b2, ys, tile, tiles_per_seq):
    n = x1.shape[0]
    ls_blocked = (tile * TOP_K) % 1024 == 0

    def smem_vec():
        if ls_blocked:
            return pl.BlockSpec((tile * TOP_K,), lambda i, *_: (i,), memory_space=pltpu.SMEM)
        return pl.BlockSpec(memory_space=pltpu.SMEM)

    grid_spec = pltpu.PrefetchScalarGridSpec(
        num_scalar_prefetch=3,
        grid=(n // tile,),
        in_specs=[
            smem_vec(), smem_vec(),
            pl.BlockSpec((tile, D_MODEL), lambda i, *_: (i, 0)),
            pl.BlockSpec((None, 1, D_MODEL), lambda i, *_: (i // tiles_per_seq, 0, 0)),
            pl.BlockSpec((1, D_MODEL), lambda i, *_: (0, 0)),
            pl.BlockSpec((1, D_MODEL), lambda i, *_: (0, 0)),
            pl.BlockSpec(memory_space=pl.ANY),
        ],
        out_specs=pl.BlockSpec((tile, D_MODEL), lambda i, *_: (i, 0)),
        scratch_shapes=[pltpu.VMEM((2, tile * TOP_K * ROW_TILE, LANES), F32),
                        pltpu.VMEM((tile * ROW_TILE, LANES), F32),
                        pltpu.SemaphoreType.DMA((2,))],
    )
    return pl.pallas_call(
        functools.partial(_combine_kernel, ls_blocked=ls_blocked),
        grid_spec=grid_spec,
        out_shape=jax.ShapeDtypeStruct((n, D_MODEL), F32),
        compiler_params=_params(("arbitrary",)),
        name="combine",
    )(cnt, off, dst, ls, gates, x1, gf, g2, b2, ys)


def _alibi_slopes():
    return jnp.exp2(-ALIBI_MAX * (jnp.arange(N_HEADS, dtype=F32) + 1.0) / N_HEADS)


def _prompt_bias(slopes):
    r = jnp.arange(Q_TILE)
    j = jnp.arange(WINDOW + Q_TILE)
    dc = (WINDOW + r)[:, None] // CHUNK - j[None, :] // CHUNK
    vis = (dc >= 0) & (dc <= WIN_CHUNKS)
    dist = jnp.abs(r[:, None] + WINDOW - j[None, :]).astype(F32)
    return jnp.where(vis[None], -slopes[:, None, None] * dist[None], NEG_INF)


def _sample_bias(slopes, t, hist):
    qpos = PAST_LEN + jnp.arange(t)
    kpos = PAST_LEN - hist + jnp.arange(hist + t)
    dc = qpos[:, None] // CHUNK - kpos[None, :] // CHUNK
    vis = (dc >= 0) & (dc <= WIN_CHUNKS) & (kpos[None, :] >= 0)
    dist = jnp.abs(qpos[:, None] - kpos[None, :]).astype(F32)
    return jnp.where(vis[None], -slopes[:, None, None] * dist[None], NEG_INF)


def _dup_heads(w):
    lead = w.shape[:-1]
    w4 = w.reshape(lead + (N_KV, HEAD_DIM))
    return jnp.concatenate([w4, w4], axis=-1).reshape(lead + (KD_DIM,))


def _proj_weights(w_in, b_in):
    q0, k0, v0, u0 = 0, ATT_DIM, ATT_DIM + KV_DIM, ATT_DIM + 2 * KV_DIM
    w = jnp.concatenate([w_in[:, q0:k0] * ATTN_SCALE, _dup_heads(w_in[:, k0:v0]),
                         _dup_heads(w_in[:, v0:u0]), w_in[:, u0:]], axis=1)
    b = jnp.concatenate([b_in[q0:k0] * ATTN_SCALE, _dup_heads(b_in[k0:v0]),
                         _dup_heads(b_in[v0:u0]), b_in[u0:]], axis=0)
    return w.astype(BF16), b.reshape(1, PROJ_COLS)


def _pool_weight(w_pool):
    n = len(POOL_WINDOWS)
    eye = jnp.eye(n, dtype=w_pool.dtype)
    return jnp.einsum('gcd,gh->gchd', w_pool, eye).reshape(POOL_DIM, POOL_DIM).astype(BF16)


def _undup(kd, nb, rows):
    return kd[:, -rows:].reshape(nb, rows, N_KV, LANES)[..., :HEAD_DIM].astype(F32)


def kernel(x_prompt, x_sample, c_prompt, c_sample, cache_k, cache_v, state_pool, w_ada, b_ada,
           w_in, b_in, sinks, w_pool, pool_scale, w_out, b_out, ln1_g, ln1_b, ln2_g, ln2_b,
           w_router, b_router, w_gu, b_gu, w_down, b_down):
    nbp, seq, _ = x_prompt.shape
    nbs, tdec, _ = x_sample.shape
    hist = cache_k.shape[2]
    n_p, n_s = nbp * seq, nbs * tdec
    n_asg = (n_p + n_s) * TOP_K
    n_blocks = -(-(n_asg + N_EXPERTS * (MOE_ROWS - 1)) // MOE_ROWS)
    n_rows = n_blocks * MOE_ROWS
    assert hist == WINDOW and tdec >= POOL_PAD and seq % TOK_TILE == 0

    nb_all = nbp + nbs
    nb_pad = -(-nb_all // SUBLANES) * SUBLANES
    c_all = jnp.concatenate([c_prompt, c_sample, jnp.zeros((nb_pad - nb_all, D_MODEL), F32)], 0)
    mod_all = _adaln(c_all, w_ada, b_ada).reshape(DEPTH, nb_pad, 6, D_MODEL)

    slopes = _alibi_slopes()
    bias_p = _prompt_bias(slopes)
    bias_s = _sample_bias(slopes, tdec, hist)
    row = lambda v: v.reshape(1, -1)

    yp, ys_tok = x_prompt, x_sample
    outs = {k: [] for k in ("kp", "vp", "pp", "ks", "vs", "ps")}
    for l in range(DEPTH):
        mod = mod_all[l]
        w_proj, b_proj = _proj_weights(w_in[l], b_in[l])
        wpool = _pool_weight(w_pool[l])
        pscale = row(pool_scale[l])
        wout = w_out[l].astype(BF16)
        wr_t = w_router[l].T
        wr_hi = wr_t.astype(BF16)
        wr = jnp.concatenate([wr_hi, (wr_t - wr_hi.astype(F32)).astype(BF16)], axis=0)
        br = b_router[l].reshape(N_EXPERTS, 1)

        q, kd, vd, u = _inproj(yp, mod, w_proj, b_proj, TOK_TILE, 0)
        mix_p = _mixer(sinks[l], q, kd, vd, u, None, None, None, bias_p, wpool, pscale,
                       tile=TOK_TILE, tq=Q_TILE, history_starts_empty=True, pos0=0)
        outs["kp"].append(_undup(kd, nbp, WINDOW))
        outs["vp"].append(_undup(vd, nbp, WINDOW))
        outs["pp"].append(u[:, -POOL_PAD:])

        qs, kds, vds, us = _inproj(ys_tok, mod, w_proj, b_proj, tdec, nbp)
        ck = _dup_heads(cache_k[l].reshape(nbs, hist, KV_DIM)).astype(BF16)
        cv = _dup_heads(cache_v[l].reshape(nbs, hist, KV_DIM)).astype(BF16)
        sp = jnp.pad(state_pool[l], ((0, 0), (POOL_HALO - POOL_PAD, 0), (0, 0)))
        mix_s = _mixer(sinks[l], qs, kds, vds, us, ck, cv, sp, bias_s, wpool, pscale,
                       tile=tdec, tq=tdec, history_starts_empty=False, pos0=PAST_LEN)
        k_new = _undup(kds, nbs, tdec)
        v_new = _undup(vds, nbs, tdec)
        outs["ks"].append(jnp.concatenate([cache_k[l], k_new], axis=1)[:, -hist:])
        outs["vs"].append(jnp.concatenate([cache_v[l], v_new], axis=1)[:, -hist:])
        outs["ps"].append(us[:, -POOL_PAD:])

        post_w = (wout, row(b_out[l]), row(ln1_g[l]), row(ln1_b[l]), wr, br)
        zero_counts = jnp.zeros((N_EXPERTS, LANES), F32)
        x1p, h2tp, rlp, rgp, meta_p, cnt_p = _post(mix_p, yp, mod, *post_w, zero_counts, TOK_TILE, 0)
        x1s, h2ts, rls, rgs, meta_s, cnt = _post(mix_s, ys_tok, mod, *post_w, cnt_p, tdec, nbp)

        counts = cnt[:, 0].astype(jnp.int32)
        padded = (counts + MOE_ROWS - 1) // MOE_ROWS * MOE_ROWS
        pad_ends = jnp.cumsum(padded).astype(jnp.int32)
        pad_starts = pad_ends - padded
        block_start = jnp.arange(n_blocks, dtype=jnp.int32) * MOE_ROWS
        block_e = jnp.minimum(jnp.sum(pad_ends[None, :] <= block_start[:, None], axis=1),
                              N_EXPERTS - 1).astype(jnp.int32)
        n_valid = (pad_ends[-1:] // MOE_ROWS).astype(jnp.int32)

        def tables(meta):
            m = meta.reshape(-1, N_EXPERTS, LANES)
            flat = lambda a: a.reshape(-1).astype(jnp.int32)
            return flat(m[:, :, 0]), flat(m[:, :, 1]), flat(pad_starts[None, :] + m[:, :, 2])

        def per_assignment(r, n):
            return r[:, :, :TOP_K, :].reshape(n * TOP_K)

        tab_p, tab_s = tables(meta_p), tables(meta_s)
        ls_p, ls_s = per_assignment(rlp, n_p), per_assignment(rls, n_s)

        xs = _dispatch(*tab_p, pad_ends, ls_p, h2tp.reshape(n_p * ROW_TILE, LANES), None,
                       n_rows, TOK_TILE, later_rows=n_s)
        xs = _dispatch(*tab_s, pad_ends, ls_s, h2ts.reshape(n_s * ROW_TILE, LANES), xs,
                       n_rows, tdec)
        ye = _moe(block_e, n_valid, xs, w_gu, b_gu, w_down, b_down, l)

        g2, b2 = row(ln2_g[l]), row(ln2_b[l])
        gf = mod[:, 5].reshape(nb_pad, 1, D_MODEL)
        yp = _combine(*tab_p, ls_p, per_assignment(rgp, n_p), x1p.reshape(n_p, D_MODEL),
                      gf[:nbp], g2, b2, ye, TOK_TILE, seq // TOK_TILE).reshape(nbp, seq, D_MODEL)
        ys_tok = _combine(*tab_s, ls_s, per_assignment(rgs, n_s), x1s.reshape(n_s, D_MODEL),
                          gf[nbp:nb_all], g2, b2, ye, tdec, 1).reshape(nbs, tdec, D_MODEL)

    st = lambda k: jnp.stack(outs[k])
    return (yp, ys_tok, st("kp"), st("vp"), st("pp"), st("ks"), st("vs"), st("ps"))
```

```python
import functools

import jax
import jax.numpy as jnp
from jax import lax
from jax.experimental import pallas as pl
from jax.experimental.pallas import tpu as pltpu

F32 = jnp.float32
BF16 = jnp.bfloat16

D_MODEL = 1024
DEPTH = 2
CHUNK = 64
WINDOW = 128
WIN_CHUNKS = WINDOW // CHUNK
HEAD_DIM = 64
ATT_DIM = 768
N_HEADS = 12
N_KV = 4
GROUP = N_HEADS // N_KV
KV_DIM = N_KV * HEAD_DIM
ATTN_SCALE = HEAD_DIM ** -0.5
LOG2_E = 1.4426950408889634
ALIBI_MAX = 8.0
NEG_INF = -1e30
POOL_DIM = D_MODEL - ATT_DIM
POOL_WINDOWS = (2, 4, 8, 16)
POOL_CH = POOL_DIM // len(POOL_WINDOWS)
POOL_PAD = max(POOL_WINDOWS) - 1
N_EXPERTS = 32
TOP_K = 4
D_FF = D_MODEL
SWIGLU_LIMIT = 7.0
SWIGLU_ALPHA = 1.702
LN_EPS = 1e-5
DEEPNORM_ALPHA = (2.0 * DEPTH) ** 0.25
PAST_LEN = 1024

LANES = 128
SUBLANES = 8
VMEM_LIMIT = 52 * 1024 * 1024

TOK_TILE = 512
Q_TILE = 128
POOL_HALO = 16
KD_DIM = N_KV * LANES
PROJ_COLS = ATT_DIM + 2 * KD_DIM + POOL_DIM
MOE_ROWS = 512
FF_CHUNK = 1024
TOKEN_UNROLL = 8
PLAN_BITS = TOK_TILE.bit_length()
PLAN_WORDS = 1024
assert 2 * PLAN_BITS * N_EXPERTS + PLAN_BITS <= PLAN_WORDS
ROW_TILE = D_MODEL // LANES
assert ROW_TILE == SUBLANES


def _params(sem, vmem=VMEM_LIMIT):
    return pltpu.CompilerParams(dimension_semantics=sem, vmem_limit_bytes=vmem)


def _layer_norm(x):
    mu = jnp.mean(x, axis=-1, keepdims=True)
    xc = x - mu
    var = jnp.mean(xc * xc, axis=-1, keepdims=True)
    return xc * lax.rsqrt(var + LN_EPS)


def _adaln_kernel(c_ref, w_ref, b_ref, o_ref):
    c = c_ref[...]
    s = c * jax.nn.sigmoid(c)
    o_ref[...] = jnp.dot(s, w_ref[...], preferred_element_type=F32,
                         precision=lax.Precision.HIGHEST) + b_ref[...]


def _adaln(c_all, w_ada, b_ada):
    nb = c_all.shape[0]
    ncol = 6 * D_MODEL
    tn = D_MODEL
    return pl.pallas_call(
        _adaln_kernel,
        grid=(DEPTH, ncol // tn),
        in_specs=[
            pl.BlockSpec((nb, D_MODEL), lambda l, j: (0, 0)),
            pl.BlockSpec((None, D_MODEL, tn), lambda l, j: (l, 0, j)),
            pl.BlockSpec((None, 1, tn), lambda l, j: (l, 0, j)),
        ],
        out_specs=pl.BlockSpec((None, nb, tn), lambda l, j: (l, 0, j)),
        out_shape=jax.ShapeDtypeStruct((DEPTH, nb, ncol), F32),
        compiler_params=_params(("parallel", "parallel")),
        name="adaln",
    )(c_all, w_ada, b_ada.reshape(DEPTH, 1, ncol))


def _inproj_kernel(x_ref, mod_ref, w_ref, b_ref, q_ref, kd_ref, vd_ref, u_ref):
    h = _layer_norm(x_ref[...]) * (1.0 + mod_ref[1:2, :]) + mod_ref[0:1, :]
    p = jnp.dot(h.astype(BF16), w_ref[...], preferred_element_type=F32) + b_ref[...]
    q_ref[...] = p[:, :ATT_DIM].astype(BF16)
    kd_ref[...] = p[:, ATT_DIM:ATT_DIM + KD_DIM].astype(BF16)
    vd_ref[...] = p[:, ATT_DIM + KD_DIM:ATT_DIM + 2 * KD_DIM].astype(BF16)
    u_ref[...] = p[:, ATT_DIM + 2 * KD_DIM:]


def _inproj(x, mod, w, b, tile, mod_row0):
    nb, s, _ = x.shape
    nt = s // tile

    def tok(c):
        return pl.BlockSpec((None, tile, c), lambda bi, i: (bi, i, 0))

    return pl.pallas_call(
        _inproj_kernel,
        grid=(nb, nt),
        in_specs=[
            tok(D_MODEL),
            pl.BlockSpec((None, 6, D_MODEL), lambda bi, i: (bi + mod_row0, 0, 0)),
            pl.BlockSpec((D_MODEL, PROJ_COLS), lambda bi, i: (0, 0)),
            pl.BlockSpec((1, PROJ_COLS), lambda bi, i: (0, 0)),
        ],
        out_specs=[tok(ATT_DIM), tok(KD_DIM), tok(KD_DIM), tok(POOL_DIM)],
        out_shape=[
            jax.ShapeDtypeStruct((nb, s, ATT_DIM), BF16),
            jax.ShapeDtypeStruct((nb, s, KD_DIM), BF16),
            jax.ShapeDtypeStruct((nb, s, KD_DIM), BF16),
            jax.ShapeDtypeStruct((nb, s, POOL_DIM), F32),
        ],
        compiler_params=_params(("parallel", "parallel")),
        name="inproj",
    )(x, mod, w, b)


def _mixer_kernel(sinks_ref, q_ref, kd_ref, kdh_ref, vd_ref, vdh_ref, u_ref, uh_ref,
                  bias_ref, wpool_ref, pscale_ref, o_ref, kall_ref, vall_ref,
                  *, tq, history_starts_empty, pos0):
    i = pl.program_id(1)
    t = q_ref.shape[0]
    hk = kdh_ref.shape[0]
    nk = hk + tq
    kall_ref[0:hk, :] = kdh_ref[...]
    kall_ref[hk:, :] = kd_ref[...]
    vall_ref[0:hk, :] = vdh_ref[...]
    vall_ref[hk:, :] = vd_ref[...]

    low_half = lax.broadcasted_iota(jnp.int32, (tq, LANES), 1) < HEAD_DIM
    for s in range(t // tq):
        rows = slice(s * tq, (s + 1) * tq)
        keys = slice(s * tq, s * tq + nk)
        table = (i == 0).astype(jnp.int32) if (history_starts_empty and s == 0) else 0
        for pair in range(N_HEADS // 2):
            q2 = q_ref[rows, pair * LANES:(pair + 1) * LANES]
            halves = []
            for half in range(2):
                head = 2 * pair + half
                kv = head // GROUP
                qm = jnp.where(low_half if half == 0 else ~low_half, q2, jnp.zeros_like(q2))
                kd = kall_ref[keys, kv * LANES:(kv + 1) * LANES]
                sc = lax.dot_general(qm, kd, (((1,), (1,)), ((), ())),
                                     preferred_element_type=F32)
                sc = sc + bias_ref[table, head]
                sink = sinks_ref[head]
                m = jnp.maximum(jnp.max(sc, axis=1, keepdims=True), sink)
                p = jnp.exp2(sc - m)
                den = jnp.sum(p, axis=1, keepdims=True) + jnp.exp2(sink - m)
                vd = vall_ref[keys, kv * LANES:(kv + 1) * LANES]
                o2 = jnp.dot(p.astype(BF16), vd, preferred_element_type=F32)
                halves.append(o2 / den)
            o_ref[rows, pair * LANES:(pair + 1) * LANES] = jnp.where(
                low_half, halves[0], halves[1]).astype(BF16)

    uh = uh_ref[...]
    if history_starts_empty:
        uh = jnp.where(i == 0, 0.0, uh)
    u = u_ref[...]
    ue = jnp.concatenate([uh, u], axis=0)
    s2 = ue + pltpu.roll(ue, 1, 0)
    s4 = s2 + pltpu.roll(s2, 2, 0)
    s8 = s4 + pltpu.roll(s4, 4, 0)
    s16 = s8 + pltpu.roll(s8, 8, 0)
    lane = lax.broadcasted_iota(jnp.int32, (1, POOL_DIM), 1)
    g0, g1, g2 = lane < POOL_CH, lane < 2 * POOL_CH, lane < 3 * POOL_CH
    wsum = jnp.where(g0, s2[POOL_HALO:], jnp.where(g1, s4[POOL_HALO:],
                     jnp.where(g2, s8[POOL_HALO:], s16[POOL_HALO:])))
    width = jnp.where(g0, 2.0, jnp.where(g1, 4.0, jnp.where(g2, 8.0, 16.0))).astype(F32)
    pos = (pos0 + i * t + lax.broadcasted_iota(jnp.int32, (t, 1), 0)).astype(F32)
    cnt = jnp.minimum(width, pos + 1.0)
    pooled = wsum / cnt - u
    mixed = jnp.dot(pooled.astype(BF16), wpool_ref[...], preferred_element_type=F32)
    o_ref[:, ATT_DIM:] = (mixed * pscale_ref[...]).astype(BF16)


def _mixer(sinks, q, kd, vd, u, k_hist, v_hist, u_hist, bias, wpool, pscale, *,
           tile, tq, history_starts_empty, pos0):
    nb, s, _ = q.shape
    nt = s // tile
    hk = WINDOW
    own_history = k_hist is None
    if own_history:
        k_hist, v_hist, u_hist = kd, vd, u
        kh_map = lambda bi, i: (bi, jnp.maximum(i * (tile // hk) - 1, 0), 0)
        uh_map = lambda bi, i: (bi, jnp.maximum(i * (tile // POOL_HALO) - 1, 0), 0)
    else:
        kh_map = lambda bi, i: (bi, 0, 0)
        uh_map = lambda bi, i: (bi, 0, 0)

    def tok(c):
        return pl.BlockSpec((None, tile, c), lambda bi, i: (bi, i, 0))

    kern = functools.partial(_mixer_kernel, tq=tq,
                             history_starts_empty=history_starts_empty, pos0=pos0)
    return pl.pallas_call(
        kern,
        grid=(nb, nt),
        in_specs=[
            pl.BlockSpec(memory_space=pltpu.SMEM),
            tok(ATT_DIM),
            tok(KD_DIM),
            pl.BlockSpec((None, hk, KD_DIM), kh_map),
            tok(KD_DIM),
            pl.BlockSpec((None, hk, KD_DIM), kh_map),
            tok(POOL_DIM),
            pl.BlockSpec((None, POOL_HALO, POOL_DIM), uh_map),
            pl.BlockSpec(bias.shape, lambda bi, i: (0, 0, 0, 0)),
            pl.BlockSpec((POOL_DIM, POOL_DIM), lambda bi, i: (0, 0)),
            pl.BlockSpec((1, POOL_DIM), lambda bi, i: (0, 0)),
        ],
        out_specs=tok(D_MODEL),
        out_shape=jax.ShapeDtypeStruct((nb, s, D_MODEL), BF16),
        scratch_shapes=[pltpu.VMEM((hk + tile, KD_DIM), BF16),
                        pltpu.VMEM((hk + tile, KD_DIM), BF16)],
        compiler_params=_params(("parallel", "parallel")),
        name="mixer",
    )(sinks, q, kd, k_hist, vd, v_hist, u, u_hist, bias, wpool, pscale)


def _to_row_tiles(ref, x):
    t = x.shape[0]
    for s in range(ROW_TILE):
        ref[pl.ds(s, t, stride=ROW_TILE), :] = x[:, s * LANES:(s + 1) * LANES]


def _from_row_tiles(ref, t):
    return jnp.concatenate([ref[pl.ds(s, t, stride=ROW_TILE), :] for s in range(ROW_TILE)], axis=1)


def _post_kernel(mix_ref, x_ref, mod_ref, wout_ref, bout_ref, g1_ref, b1_ref, wr_ref, br_ref,
                 cin_ref, x1_ref, h2t_ref, rl_ref, rg_ref, meta_ref, cnt_ref):
    first = (pl.program_id(0) == 0) & (pl.program_id(1) == 0)

    @pl.when(first)
    def _():
        cnt_ref[...] = cin_ref[...]

    t = x_ref.shape[0]
    mix = jnp.dot(mix_ref[...], wout_ref[...], preferred_element_type=F32) + bout_ref[...]
    z = DEEPNORM_ALPHA * x_ref[...] + mod_ref[2:3, :] * mix
    x1 = _layer_norm(z) * g1_ref[...] + b1_ref[...]
    x1_ref[...] = x1
    h2 = _layer_norm(x1) * (1.0 + mod_ref[4:5, :]) + mod_ref[3:4, :]
    _to_row_tiles(h2t_ref, h2)

    h2_hi = h2.astype(BF16)
    h2_lo = (h2 - h2_hi.astype(F32)).astype(BF16)
    contract_last = (((1,), (1,)), ((), ()))
    d_hi = lax.dot_general(wr_ref[...], h2_hi, contract_last, preferred_element_type=F32)
    d_lo = lax.dot_general(wr_ref[0:N_EXPERTS, :], h2_lo, contract_last,
                           preferred_element_type=F32)
    logits = d_hi[:N_EXPERTS] + d_hi[N_EXPERTS:] + d_lo + br_ref[...]
    expert = lax.broadcasted_iota(jnp.int32, (N_EXPERTS, t), 0).astype(F32)
    vals, idxs, hots = [], [], []
    cur = logits
    for _ in range(TOP_K):
        mk = jnp.max(cur, axis=0, keepdims=True)
        ik = jnp.min(jnp.where(cur == mk, expert, float(N_EXPERTS)), axis=0, keepdims=True)
        hot = expert == ik
        cur = jnp.where(hot, -jnp.inf, cur)
        vals.append(mk)
        idxs.append(ik)
        hots.append(hot)
    exps = [jnp.exp(v - vals[0]) for v in vals]
    den = exps[0] + exps[1] + exps[2] + exps[3]
    gates = [e / den for e in exps]

    onehot = jnp.zeros((N_EXPERTS, t), F32)
    for hot in hots:
        onehot = onehot + hot.astype(F32)
    onehot_bf = onehot.astype(BF16)
    tok_r = lax.broadcasted_iota(jnp.int32, (t, t), 0)
    tok_c = lax.broadcasted_iota(jnp.int32, (t, t), 1)
    earlier = jnp.where(tok_r < tok_c, 1.0, 0.0).astype(BF16)
    rank = jnp.dot(onehot_bf, earlier, preferred_element_type=F32)
    exp_r = lax.broadcasted_iota(jnp.int32, (N_EXPERTS, N_EXPERTS), 0)
    exp_c = lax.broadcasted_iota(jnp.int32, (N_EXPERTS, N_EXPERTS), 1)
    lower = jnp.where(exp_c < exp_r, 1.0, 0.0).astype(BF16)
    tile_off = jnp.sum(jnp.dot(lower, onehot_bf, preferred_element_type=F32),
                       axis=1, keepdims=True)
    tile_cnt = jnp.sum(onehot, axis=1, keepdims=True)
    local = (rank + tile_off) * float(ROW_TILE)
    slots = [jnp.sum(jnp.where(hot, local, 0.0), axis=0, keepdims=True) for hot in hots]

    rl_ref[...] = jnp.concatenate(slots + idxs, axis=0).astype(jnp.int32)
    rg_ref[...] = jnp.concatenate(gates + [jnp.zeros_like(g) for g in gates], axis=0)
    lane = lax.broadcasted_iota(jnp.int32, (N_EXPERTS, LANES), 1)
    before = cnt_ref[...]
    meta = jnp.where(lane == 0, tile_cnt, jnp.where(lane == 1, tile_off,
                     jnp.where(lane == 2, before, 0.0)))
    meta_ref[...] = meta.astype(jnp.int32)
    cnt_ref[...] = before + tile_cnt


def _post(mix, x, mod, wout, bout, g1, b1, wr, br, counts_in, tile, mod_row0):
    nb, s, _ = x.shape
    nt = s // tile

    def tok(c):
        return pl.BlockSpec((None, tile, c), lambda bi, i: (bi, i, 0))

    def whole(shape):
        return pl.BlockSpec(shape, lambda bi, i: tuple(0 for _ in shape))

    return pl.pallas_call(
        _post_kernel,
        grid=(nb, nt),
        in_specs=[
            tok(D_MODEL), tok(D_MODEL),
            pl.BlockSpec((None, 6, D_MODEL), lambda bi, i: (bi + mod_row0, 0, 0)),
            whole((D_MODEL, D_MODEL)), whole((1, D_MODEL)),
            whole((1, D_MODEL)), whole((1, D_MODEL)),
            whole((2 * N_EXPERTS, D_MODEL)), whole((N_EXPERTS, 1)),
            whole((N_EXPERTS, LANES)),
        ],
        out_specs=[tok(D_MODEL),
                   pl.BlockSpec((None, tile * ROW_TILE, LANES), lambda bi, i: (bi, i, 0)),
                   pl.BlockSpec((None, None, 2 * TOP_K, tile), lambda bi, i: (bi, i, 0, 0)),
                   pl.BlockSpec((None, None, 2 * TOP_K, tile), lambda bi, i: (bi, i, 0, 0)),
                   pl.BlockSpec((None, None, N_EXPERTS, LANES), lambda bi, i: (bi, i, 0, 0)),
                   whole((N_EXPERTS, LANES))],
        out_shape=[
            jax.ShapeDtypeStruct((nb, s, D_MODEL), F32),
            jax.ShapeDtypeStruct((nb, s * ROW_TILE, LANES), F32),
            jax.ShapeDtypeStruct((nb, nt, 2 * TOP_K, tile), jnp.int32),
            jax.ShapeDtypeStruct((nb, nt, 2 * TOP_K, tile), F32),
            jax.ShapeDtypeStruct((nb, nt, N_EXPERTS, LANES), jnp.int32),
            jax.ShapeDtypeStruct((N_EXPERTS, LANES), F32),
        ],
        compiler_params=_params(("arbitrary", "arbitrary")),
        name="post",
    )(mix, x, mod, wout, bout, g1, b1, wr, br, counts_in)


def _tile_rows(row, n=1):
    return pl.ds(pl.multiple_of(row * ROW_TILE, ROW_TILE), n * ROW_TILE)


def _rows_at(first_row):
    return pl.ds(pl.multiple_of(first_row, ROW_TILE), ROW_TILE)


def _copy_plan(cnt, off, dst, n_bits):
    n_tiles = cnt.shape[0]
    bits = jnp.arange(PLAN_BITS, dtype=jnp.int32)[None, :, None]
    c = cnt[:, None, :]
    valid = ((c >> bits) & 1) * (bits < n_bits)
    done = c & ((1 << bits) - 1)
    order = jnp.argsort((1 - valid) * N_EXPERTS + jnp.arange(N_EXPERTS, dtype=jnp.int32), axis=-1)
    local = jnp.take_along_axis(off[:, None, :] + done, order, axis=-1)
    glob = jnp.take_along_axis(dst[:, None, :] + done, order, axis=-1)
    n = jnp.sum(valid, axis=-1)
    used = 2 * PLAN_BITS * N_EXPERTS + PLAN_BITS
    plan = jnp.concatenate([local.reshape(n_tiles, -1), glob.reshape(n_tiles, -1), n,
                            jnp.zeros((n_tiles, PLAN_WORDS - used), jnp.int32)], axis=1)
    return plan.reshape(-1).astype(jnp.int32)


def _start_run_copies(plan_ref, t, make_copy):
    for b in range(t.bit_length()):
        def piece(r, carry, b=b):
            j = b * N_EXPERTS + r
            make_copy(plan_ref[j], plan_ref[PLAN_BITS * N_EXPERTS + j], 1 << b).start()
            return carry

        lax.fori_loop(0, plan_ref[2 * PLAN_BITS * N_EXPERTS + b], piece, 0)


def _dispatch_kernel(pe_ref, plan_ref, ls_ref, h_ref, *rest, clear_blocks, ls_blocked):
    xs_ref, stage, zbuf, sems, sem = rest[-5:]
    i = pl.program_id(0)
    t = h_ref.shape[0] // ROW_TILE

    if clear_blocks:
        @pl.when(i == 0)
        def _():
            zbuf[...] = jnp.zeros_like(zbuf)

            def clear_copy(row):
                return pltpu.make_async_copy(zbuf, xs_ref.at[_tile_rows(row, MOE_ROWS), :], sem)

            def clear_segment_end(e, n):
                end = pe_ref[e]
                start = jnp.where(e == 0, 0, pe_ref[jnp.maximum(e - 1, 0)])
                for back in range(1, clear_blocks + 1):
                    row = end - back * MOE_ROWS

                    @pl.when(row >= start)
                    def _():
                        clear_copy(row).start()
                    n = n + (row >= start).astype(jnp.int32)
                return n

            n_started = lax.fori_loop(0, N_EXPERTS, clear_segment_end, 0)
            n_blocks = xs_ref.shape[0] // (MOE_ROWS * ROW_TILE)
            n_valid = pe_ref[N_EXPERTS - 1] // MOE_ROWS

            def clear_past_end(b, carry):
                clear_copy(b * MOE_ROWS).start()
                return carry

            lax.fori_loop(n_valid, n_blocks, clear_past_end, 0)

            def wait_one(_, carry):
                clear_copy(0).wait()
                return carry

            lax.fori_loop(0, n_started + n_blocks - n_valid, wait_one, 0)

    base = 0 if ls_blocked else i * (t * TOP_K)
    slot = i % 2

    def permute_and_send(s):
        def scatter(g, carry):
            for u in range(TOKEN_UNROLL):
                tok = g * TOKEN_UNROLL + u
                tile = h_ref[_tile_rows(tok), :]
                for k in range(TOP_K):
                    stage[s, _rows_at(ls_ref[base + k * t + tok]), :] = tile
            return carry

        lax.fori_loop(0, t // TOKEN_UNROLL, scatter, 0)
        _start_run_copies(
            plan_ref, t,
            lambda loc, glob, n: pltpu.make_async_copy(stage.at[s, _tile_rows(loc, n), :],
                                                       xs_ref.at[_tile_rows(glob, n), :],
                                                       sems.at[s]))

    for s in range(2):
        pl.when(slot == s)(functools.partial(permute_and_send, s))

    def wait_stage(s):
        pltpu.make_async_copy(stage.at[s], xs_ref.at[_tile_rows(0, t * TOP_K), :], sems.at[s]).wait()

    @pl.when(i > 0)
    def _():
        wait_stage(1 - slot)

    @pl.when(i == pl.num_programs(0) - 1)
    def _():
        wait_stage(slot)


def _dispatch(plan, pad_ends, ls, h2t, xs, n_rows, tile, later_rows=0):
    n_tiles = h2t.shape[0] // (tile * ROW_TILE)
    first = xs is None
    clear_blocks = -(-(later_rows + MOE_ROWS - 1) // MOE_ROWS) if first else 0
    ls_blocked = (tile * TOP_K) % 1024 == 0
    ls_spec = (pl.BlockSpec((tile * TOP_K,), lambda i, *_: (i,), memory_space=pltpu.SMEM)
               if ls_blocked else pl.BlockSpec(memory_space=pltpu.SMEM))
    in_specs = [pl.BlockSpec((PLAN_WORDS,), lambda i, *_: (i,), memory_space=pltpu.SMEM),
                ls_spec, pl.BlockSpec((tile * ROW_TILE, LANES), lambda i, *_: (i, 0))]
    args = [pad_ends, plan, ls, h2t]
    if not first:
        in_specs.append(pl.BlockSpec(memory_space=pl.ANY))
        args.append(xs)
    grid_spec = pltpu.PrefetchScalarGridSpec(
        num_scalar_prefetch=1,
        grid=(n_tiles,),
        in_specs=in_specs,
        out_specs=pl.BlockSpec(memory_space=pl.ANY),
        scratch_shapes=[pltpu.VMEM((2, tile * TOP_K * ROW_TILE, LANES), F32),
                        pltpu.VMEM((MOE_ROWS * ROW_TILE, LANES), F32),
                        pltpu.SemaphoreType.DMA((2,)),
                        pltpu.SemaphoreType.DMA],
    )
    return pl.pallas_call(
        functools.partial(_dispatch_kernel, clear_blocks=clear_blocks, ls_blocked=ls_blocked),
        grid_spec=grid_spec,
        out_shape=jax.ShapeDtypeStruct((n_rows * ROW_TILE, LANES), F32),
        input_output_aliases={} if first else {4: 0},
        compiler_params=_params(("arbitrary",)),
        name="dispatch",
    )(*args)


def _moe_kernel(be_ref, nv_ref, xs_ref, wgu_ref, bgu_ref, wdn_ref, bdn_ref, ys_ref,
                wgu_bf, wdn_bf):
    i = pl.program_id(0)
    e = be_ref[i]
    prev = be_ref[jnp.maximum(i - 1, 0)]
    rows = 128

    @pl.when((i == 0) | (e != prev))
    def _():
        def cast(r, carry):
            sl = pl.ds(pl.multiple_of(r * rows, rows), rows)
            wgu_bf[sl, :] = wgu_ref[sl, :].astype(BF16)
            wdn_bf[sl, :] = wdn_ref[sl, :].astype(BF16)
            return carry
        lax.fori_loop(0, D_MODEL // rows, cast, 0)

    @pl.when(i < nv_ref[0])
    def _():
        x = _from_row_tiles(xs_ref, MOE_ROWS).astype(BF16)
        y = None
        for c in range(D_FF // FF_CHUNK):
            gcols = slice(c * FF_CHUNK, (c + 1) * FF_CHUNK)
            ucols = slice(D_FF + c * FF_CHUNK, D_FF + (c + 1) * FF_CHUNK)
            gate = jnp.dot(x, wgu_bf[:, gcols], preferred_element_type=F32) + bgu_ref[:, gcols]
            up = jnp.dot(x, wgu_bf[:, ucols], preferred_element_type=F32) + bgu_ref[:, ucols]
            gate = jnp.minimum(gate, SWIGLU_LIMIT)
            up = jnp.clip(up, -SWIGLU_LIMIT, SWIGLU_LIMIT)
            a = gate * jax.nn.sigmoid(SWIGLU_ALPHA * gate) * (up + 1.0)
            part = jnp.dot(a.astype(BF16), wdn_bf[gcols, :], preferred_element_type=F32)
            y = part if y is None else y + part
        _to_row_tiles(ys_ref, y + bdn_ref[...])

    @pl.when(i >= nv_ref[0])
    def _():
        ys_ref[...] = jnp.zeros_like(ys_ref)


def _moe(block_e, n_valid, xs, w_gu, b_gu, w_down, b_down, layer):
    n_rows = xs.shape[0] // ROW_TILE
    n_blocks = n_rows // MOE_ROWS
    grid_spec = pltpu.PrefetchScalarGridSpec(
        num_scalar_prefetch=2,
        grid=(n_blocks,),
        in_specs=[
            pl.BlockSpec((MOE_ROWS * ROW_TILE, LANES),
                         lambda i, be, nv: (jnp.minimum(i, nv[0] - 1), 0)),
            pl.BlockSpec((None, None, D_MODEL, 2 * D_FF), lambda i, be, nv: (layer, be[i], 0, 0)),
            pl.BlockSpec((None, None, 1, 2 * D_FF), lambda i, be, nv: (layer, be[i], 0, 0)),
            pl.BlockSpec((None, None, D_FF, D_MODEL), lambda i, be, nv: (layer, be[i], 0, 0)),
            pl.BlockSpec((None, None, 1, D_MODEL), lambda i, be, nv: (layer, be[i], 0, 0)),
        ],
        out_specs=pl.BlockSpec((MOE_ROWS * ROW_TILE, LANES), lambda i, be, nv: (i, 0)),
        scratch_shapes=[pltpu.VMEM((D_MODEL, 2 * D_FF), BF16),
                        pltpu.VMEM((D_FF, D_MODEL), BF16)],
    )
    return pl.pallas_call(
        _moe_kernel,
        grid_spec=grid_spec,
        out_shape=jax.ShapeDtypeStruct((n_rows * ROW_TILE, LANES), F32),
        compiler_params=_params(("arbitrary",)),
        name="moe",
    )(block_e, n_valid, xs, w_gu, b_gu.reshape(DEPTH, N_EXPERTS, 1, 2 * D_FF),
      w_down, b_down.reshape(DEPTH, N_EXPERTS, 1, D_MODEL))


def _combine_kernel(plan_ref, next_plan_ref, ls_ref, gate_ref, x1_ref, gf_ref, g2_ref, b2_ref,
                    ys_ref, o_ref, stage, ft, sems, *, ls_blocked):
    i = pl.program_id(0)
    t = x1_ref.shape[0]
    slot = i % 2

    def start_fetch(tile_plan_ref, s):
        _start_run_copies(
            tile_plan_ref, t,
            lambda loc, glob, n: pltpu.make_async_copy(ys_ref.at[_tile_rows(glob, n), :],
                                                       stage.at[s, _tile_rows(loc, n), :],
                                                       sems.at[s]))

    @pl.when(i == 0)
    def _():
        start_fetch(plan_ref, 0)

    for s in range(2):
        @pl.when((i + 1 < pl.num_programs(0)) & (slot == 1 - s))
        def _(s=s):
            start_fetch(next_plan_ref, s)

    pltpu.make_async_copy(ys_ref.at[_tile_rows(0, t * TOP_K), :], stage.at[slot],
                          sems.at[slot]).wait()

    base = 0 if ls_blocked else i * (t * TOP_K)

    def gate_sum(s):
        def gather(g, carry):
            for u in range(TOKEN_UNROLL):
                tok = g * TOKEN_UNROLL + u
                acc = None
                for k in range(TOP_K):
                    j = base + k * t + tok
                    v = stage[s, _rows_at(ls_ref[j]), :] * gate_ref[j]
                    acc = v if acc is None else acc + v
                ft[_tile_rows(tok), :] = acc
            return carry

        lax.fori_loop(0, t // TOKEN_UNROLL, gather, 0)

    for s in range(2):
        pl.when(slot == s)(functools.partial(gate_sum, s))
    f = _from_row_tiles(ft, t)
    z = DEEPNORM_ALPHA * x1_ref[...] + gf_ref[...] * f
    o_ref[...] = _layer_norm(z) * g2_ref[...] + b2_ref[...]


def _combine(plan, ls, gates, x1, gf, g2, b2, ys, tile, tiles_per_seq):
    n = x1.shape[0]
    n_tiles = n // tile
    ls_blocked = (tile * TOP_K) % 1024 == 0

    def smem_vec():
        if ls_blocked:
            return pl.BlockSpec((tile * TOP_K,), lambda i, *_: (i,), memory_space=pltpu.SMEM)
        return pl.BlockSpec(memory_space=pltpu.SMEM)

    grid_spec = pltpu.PrefetchScalarGridSpec(
        num_scalar_prefetch=0,
        grid=(n_tiles,),
        in_specs=[
            pl.BlockSpec((PLAN_WORDS,), lambda i: (i,), memory_space=pltpu.SMEM),
            pl.BlockSpec((PLAN_WORDS,), lambda i: (jnp.minimum(i + 1, n_tiles - 1),),
                         memory_space=pltpu.SMEM),
            smem_vec(), smem_vec(),
            pl.BlockSpec((tile, D_MODEL), lambda i, *_: (i, 0)),
            pl.BlockSpec((None, 1, D_MODEL), lambda i, *_: (i // tiles_per_seq, 0, 0)),
            pl.BlockSpec((1, D_MODEL), lambda i, *_: (0, 0)),
            pl.BlockSpec((1, D_MODEL), lambda i, *_: (0, 0)),
            pl.BlockSpec(memory_space=pl.ANY),
        ],
        out_specs=pl.BlockSpec((tile, D_MODEL), lambda i, *_: (i, 0)),
        scratch_shapes=[pltpu.VMEM((2, tile * TOP_K * ROW_TILE, LANES), F32),
                        pltpu.VMEM((tile * ROW_TILE, LANES), F32),
                        pltpu.SemaphoreType.DMA((2,))],
    )
    return pl.pallas_call(
        functools.partial(_combine_kernel, ls_blocked=ls_blocked),
        grid_spec=grid_spec,
        out_shape=jax.ShapeDtypeStruct((n, D_MODEL), F32),
        compiler_params=_params(("arbitrary",)),
        name="combine",
    )(plan, plan, ls, gates, x1, gf, g2, b2, ys)


def _alibi_slopes():
    return jnp.exp2(-ALIBI_MAX * (jnp.arange(N_HEADS, dtype=F32) + 1.0) / N_HEADS)


def _prompt_bias(slopes):
    r = jnp.arange(Q_TILE)
    j = jnp.arange(WINDOW + Q_TILE)
    dc = (WINDOW + r)[:, None] // CHUNK - j[None, :] // CHUNK
    vis = (dc >= 0) & (dc <= WIN_CHUNKS)
    dist = jnp.abs(r[:, None] + WINDOW - j[None, :]).astype(F32)
    bias = jnp.where(vis[None], -slopes[:, None, None] * dist[None] * LOG2_E, NEG_INF)
    return jnp.stack([bias, jnp.where(j[None, None, :] < WINDOW, NEG_INF, bias)])


def _sample_bias(slopes, t, hist):
    qpos = PAST_LEN + jnp.arange(t)
    kpos = PAST_LEN - hist + jnp.arange(hist + t)
    dc = qpos[:, None] // CHUNK - kpos[None, :] // CHUNK
    vis = (dc >= 0) & (dc <= WIN_CHUNKS) & (kpos[None, :] >= 0)
    dist = jnp.abs(qpos[:, None] - kpos[None, :]).astype(F32)
    return jnp.where(vis[None], -slopes[:, None, None] * dist[None] * LOG2_E, NEG_INF)[None]


def _dup_heads(w):
    lead = w.shape[:-1]
    w4 = w.reshape(lead + (N_KV, HEAD_DIM))
    return jnp.concatenate([w4, w4], axis=-1).reshape(lead + (KD_DIM,))


def _proj_weights(w_in, b_in):
    q0, k0, v0, u0 = 0, ATT_DIM, ATT_DIM + KV_DIM, ATT_DIM + 2 * KV_DIM
    q_scale = ATTN_SCALE * LOG2_E
    w = jnp.concatenate([w_in[:, q0:k0] * q_scale, _dup_heads(w_in[:, k0:v0]),
                         _dup_heads(w_in[:, v0:u0]), w_in[:, u0:]], axis=1)
    b = jnp.concatenate([b_in[q0:k0] * q_scale, _dup_heads(b_in[k0:v0]),
                         _dup_heads(b_in[v0:u0]), b_in[u0:]], axis=0)
    return w.astype(BF16), b.reshape(1, PROJ_COLS)


def _pool_weight(w_pool):
    n = len(POOL_WINDOWS)
    eye = jnp.eye(n, dtype=w_pool.dtype)
    return jnp.einsum('gcd,gh->gchd', w_pool, eye).reshape(POOL_DIM, POOL_DIM).astype(BF16)


def _undup(kd, nb, rows):
    return kd[:, -rows:].reshape(nb, rows, N_KV, LANES)[..., :HEAD_DIM].astype(F32)


def kernel(x_prompt, x_sample, c_prompt, c_sample, cache_k, cache_v, state_pool, w_ada, b_ada,
           w_in, b_in, sinks, w_pool, pool_scale, w_out, b_out, ln1_g, ln1_b, ln2_g, ln2_b,
           w_router, b_router, w_gu, b_gu, w_down, b_down):
    nbp, seq, _ = x_prompt.shape
    nbs, tdec, _ = x_sample.shape
    hist = cache_k.shape[2]
    n_p, n_s = nbp * seq, nbs * tdec
    n_asg = (n_p + n_s) * TOP_K
    n_blocks = -(-(n_asg + N_EXPERTS * (MOE_ROWS - 1)) // MOE_ROWS)
    n_rows = n_blocks * MOE_ROWS
    assert hist == WINDOW and tdec >= POOL_PAD and seq % TOK_TILE == 0

    nb_all = nbp + nbs
    nb_pad = -(-nb_all // SUBLANES) * SUBLANES
    c_all = jnp.concatenate([c_prompt, c_sample, jnp.zeros((nb_pad - nb_all, D_MODEL), F32)], 0)
    mod_all = _adaln(c_all, w_ada, b_ada).reshape(DEPTH, nb_pad, 6, D_MODEL)

    slopes = _alibi_slopes()
    bias_p = _prompt_bias(slopes)
    bias_s = _sample_bias(slopes, tdec, hist)
    row = lambda v: v.reshape(1, -1)

    yp, ys_tok = x_prompt, x_sample
    outs = {k: [] for k in ("kp", "vp", "pp", "ks", "vs", "ps")}
    for l in range(DEPTH):
        mod = mod_all[l]
        w_proj, b_proj = _proj_weights(w_in[l], b_in[l])
        wpool = _pool_weight(w_pool[l])
        pscale = row(pool_scale[l])
        wout = w_out[l].astype(BF16)
        wr_t = w_router[l].T
        wr_hi = wr_t.astype(BF16)
        wr = jnp.concatenate([wr_hi, (wr_t - wr_hi.astype(F32)).astype(BF16)], axis=0)
        br = b_router[l].reshape(N_EXPERTS, 1)

        q, kd, vd, u = _inproj(yp, mod, w_proj, b_proj, TOK_TILE, 0)
        sinks2 = sinks[l] * LOG2_E
        mix_p = _mixer(sinks2, q, kd, vd, u, None, None, None, bias_p, wpool, pscale,
                       tile=TOK_TILE, tq=Q_TILE, history_starts_empty=True, pos0=0)
        outs["kp"].append(_undup(kd, nbp, WINDOW))
        outs["vp"].append(_undup(vd, nbp, WINDOW))
        outs["pp"].append(u[:, -POOL_PAD:])

        qs, kds, vds, us = _inproj(ys_tok, mod, w_proj, b_proj, tdec, nbp)
        ck = _dup_heads(cache_k[l].reshape(nbs, hist, KV_DIM)).astype(BF16)
        cv = _dup_heads(cache_v[l].reshape(nbs, hist, KV_DIM)).astype(BF16)
        sp = jnp.pad(state_pool[l], ((0, 0), (POOL_HALO - POOL_PAD, 0), (0, 0)))
        mix_s = _mixer(sinks2, qs, kds, vds, us, ck, cv, sp, bias_s, wpool, pscale,
                       tile=tdec, tq=tdec, history_starts_empty=False, pos0=PAST_LEN)
        k_new = _undup(kds, nbs, tdec)
        v_new = _undup(vds, nbs, tdec)
        outs["ks"].append(jnp.concatenate([cache_k[l], k_new], axis=1)[:, -hist:])
        outs["vs"].append(jnp.concatenate([cache_v[l], v_new], axis=1)[:, -hist:])
        outs["ps"].append(us[:, -POOL_PAD:])

        post_w = (wout, row(b_out[l]), row(ln1_g[l]), row(ln1_b[l]), wr, br)
        zero_counts = jnp.zeros((N_EXPERTS, LANES), F32)
        x1p, h2tp, rlp, rgp, meta_p, cnt_p = _post(mix_p, yp, mod, *post_w, zero_counts, TOK_TILE, 0)
        x1s, h2ts, rls, rgs, meta_s, cnt = _post(mix_s, ys_tok, mod, *post_w, cnt_p, tdec, nbp)

        counts = cnt[:, 0].astype(jnp.int32)
        padded = (counts + MOE_ROWS - 1) // MOE_ROWS * MOE_ROWS
        pad_ends = jnp.cumsum(padded).astype(jnp.int32)
        pad_starts = pad_ends - padded
        block_start = jnp.arange(n_blocks, dtype=jnp.int32) * MOE_ROWS
        block_e = jnp.minimum(jnp.sum(pad_ends[None, :] <= block_start[:, None], axis=1),
                              N_EXPERTS - 1).astype(jnp.int32)
        n_valid = (pad_ends[-1:] // MOE_ROWS).astype(jnp.int32)

        def copy_plan(meta, tile):
            m = meta.reshape(-1, N_EXPERTS, LANES)
            return _copy_plan(m[:, :, 0], m[:, :, 1], pad_starts[None, :] + m[:, :, 2],
                              tile.bit_length())

        def per_assignment(r, n):
            return r[:, :, :TOP_K, :].reshape(n * TOP_K)

        plan_p, plan_s = copy_plan(meta_p, TOK_TILE), copy_plan(meta_s, tdec)
        ls_p, ls_s = per_assignment(rlp, n_p), per_assignment(rls, n_s)

        xs = _dispatch(plan_p, pad_ends, ls_p, h2tp.reshape(n_p * ROW_TILE, LANES), None,
                       n_rows, TOK_TILE, later_rows=n_s)
        xs = _dispatch(plan_s, pad_ends, ls_s, h2ts.reshape(n_s * ROW_TILE, LANES), xs,
                       n_rows, tdec)
        ye = _moe(block_e, n_valid, xs, w_gu, b_gu, w_down, b_down, l)

        g2, b2 = row(ln2_g[l]), row(ln2_b[l])
        gf = mod[:, 5].reshape(nb_pad, 1, D_MODEL)
        yp = _combine(plan_p, ls_p, per_assignment(rgp, n_p), x1p.reshape(n_p, D_MODEL),
                      gf[:nbp], g2, b2, ye, TOK_TILE, seq // TOK_TILE).reshape(nbp, seq, D_MODEL)
        ys_tok = _combine(plan_s, ls_s, per_assignment(rgs, n_s), x1s.reshape(n_s, D_MODEL),
                          gf[nbp:nb_all], g2, b2, ye, tdec, 1).reshape(nbs, tdec, D_MODEL)

    st = lambda k: jnp.stack(outs[k])
    return (yp, ys_tok, st("kp"), st("vp"), st("pp"), st("ks"), st("vs"), st("ps"))
```

```python
import functools

import jax
import jax.numpy as jnp
from jax import lax
from jax.experimental import pallas as pl
from jax.experimental.pallas import tpu as pltpu

F32 = jnp.float32
BF16 = jnp.bfloat16

D_MODEL = 1024
DEPTH = 2
CHUNK = 64
WINDOW = 128
WIN_CHUNKS = WINDOW // CHUNK
HEAD_DIM = 64
ATT_DIM = 768
N_HEADS = 12
N_KV = 4
GROUP = N_HEADS // N_KV
KV_DIM = N_KV * HEAD_DIM
ATTN_SCALE = HEAD_DIM ** -0.5
LOG2_E = 1.4426950408889634
ALIBI_MAX = 8.0
NEG_INF = -1e30
POOL_DIM = D_MODEL - ATT_DIM
POOL_WINDOWS = (2, 4, 8, 16)
POOL_CH = POOL_DIM // len(POOL_WINDOWS)
POOL_PAD = max(POOL_WINDOWS) - 1
N_EXPERTS = 32
TOP_K = 4
D_FF = D_MODEL
SWIGLU_LIMIT = 7.0
SWIGLU_ALPHA = 1.702
LN_EPS = 1e-5
DEEPNORM_ALPHA = (2.0 * DEPTH) ** 0.25
PAST_LEN = 1024

LANES = 128
SUBLANES = 8
VMEM_LIMIT = 52 * 1024 * 1024

TOK_TILE = 512
Q_TILE = 128
POOL_HALO = 16
KD_DIM = N_KV * LANES
PROJ_COLS = ATT_DIM + 2 * KD_DIM + POOL_DIM
MOE_ROWS = 512
TOKEN_UNROLL = 8
PLAN_BITS = TOK_TILE.bit_length()
PLAN_WORDS = 1024
assert 2 * PLAN_BITS * N_EXPERTS + PLAN_BITS <= PLAN_WORDS
ROW_TILE = D_MODEL // LANES
assert ROW_TILE == SUBLANES


def _params(sem, vmem=VMEM_LIMIT):
    return pltpu.CompilerParams(dimension_semantics=sem, vmem_limit_bytes=vmem)


def _layer_norm(x):
    mu = jnp.mean(x, axis=-1, keepdims=True)
    xc = x - mu
    var = jnp.mean(xc * xc, axis=-1, keepdims=True)
    return xc * lax.rsqrt(var + LN_EPS)


def _adaln_kernel(c_ref, w_ref, b_ref, o_ref):
    c = c_ref[...]
    s = c * jax.nn.sigmoid(c)
    o_ref[...] = jnp.dot(s, w_ref[...], preferred_element_type=F32,
                         precision=lax.Precision.HIGHEST) + b_ref[...]


def _adaln(c_all, w_ada, b_ada):
    nb = c_all.shape[0]
    ncol = 6 * D_MODEL
    tn = D_MODEL
    return pl.pallas_call(
        _adaln_kernel,
        grid=(DEPTH, ncol // tn),
        in_specs=[
            pl.BlockSpec((nb, D_MODEL), lambda l, j: (0, 0)),
            pl.BlockSpec((None, D_MODEL, tn), lambda l, j: (l, 0, j)),
            pl.BlockSpec((None, 1, tn), lambda l, j: (l, 0, j)),
        ],
        out_specs=pl.BlockSpec((None, nb, tn), lambda l, j: (l, 0, j)),
        out_shape=jax.ShapeDtypeStruct((DEPTH, nb, ncol), F32),
        compiler_params=_params(("parallel", "parallel")),
        name="adaln",
    )(c_all, w_ada, b_ada.reshape(DEPTH, 1, ncol))


def _inproj_kernel(x_ref, mod_ref, w_ref, b_ref, q_ref, kd_ref, vd_ref, u_ref):
    h = _layer_norm(x_ref[...]) * (1.0 + mod_ref[1:2, :]) + mod_ref[0:1, :]
    p = jnp.dot(h.astype(BF16), w_ref[...], preferred_element_type=F32) + b_ref[...]
    q_ref[...] = p[:, :ATT_DIM].astype(BF16)
    kd_ref[...] = p[:, ATT_DIM:ATT_DIM + KD_DIM].astype(BF16)
    vd_ref[...] = p[:, ATT_DIM + KD_DIM:ATT_DIM + 2 * KD_DIM].astype(BF16)
    u_ref[...] = p[:, ATT_DIM + 2 * KD_DIM:]


def _inproj(x, mod, w, b, tile, mod_row0):
    nb, s, _ = x.shape
    nt = s // tile

    def tok(c):
        return pl.BlockSpec((None, tile, c), lambda bi, i: (bi, i, 0))

    return pl.pallas_call(
        _inproj_kernel,
        grid=(nb, nt),
        in_specs=[
            tok(D_MODEL),
            pl.BlockSpec((None, 6, D_MODEL), lambda bi, i: (bi + mod_row0, 0, 0)),
            pl.BlockSpec((D_MODEL, PROJ_COLS), lambda bi, i: (0, 0)),
            pl.BlockSpec((1, PROJ_COLS), lambda bi, i: (0, 0)),
        ],
        out_specs=[tok(ATT_DIM), tok(KD_DIM), tok(KD_DIM), tok(POOL_DIM)],
        out_shape=[
            jax.ShapeDtypeStruct((nb, s, ATT_DIM), BF16),
            jax.ShapeDtypeStruct((nb, s, KD_DIM), BF16),
            jax.ShapeDtypeStruct((nb, s, KD_DIM), BF16),
            jax.ShapeDtypeStruct((nb, s, POOL_DIM), F32),
        ],
        compiler_params=_params(("parallel", "parallel")),
        name="inproj",
    )(x, mod, w, b)


def _mixer_kernel(sinks_ref, q_ref, kd_ref, kdh_ref, vd_ref, vdh_ref, u_ref, uh_ref,
                  bias_ref, wpool_ref, pscale_ref, o_ref, kall_ref, vall_ref,
                  *, tq, history_starts_empty, pos0):
    i = pl.program_id(1)
    t = q_ref.shape[0]
    hk = kdh_ref.shape[0]
    nk = hk + tq
    kall_ref[0:hk, :] = kdh_ref[...]
    kall_ref[hk:, :] = kd_ref[...]
    vall_ref[0:hk, :] = vdh_ref[...]
    vall_ref[hk:, :] = vd_ref[...]

    low_half = lax.broadcasted_iota(jnp.int32, (tq, LANES), 1) < HEAD_DIM
    for s in range(t // tq):
        rows = slice(s * tq, (s + 1) * tq)
        keys = slice(s * tq, s * tq + nk)
        table = (i == 0).astype(jnp.int32) if (history_starts_empty and s == 0) else 0
        for pair in range(N_HEADS // 2):
            q2 = q_ref[rows, pair * LANES:(pair + 1) * LANES]
            halves = []
            for half in range(2):
                head = 2 * pair + half
                kv = head // GROUP
                qm = jnp.where(low_half if half == 0 else ~low_half, q2, jnp.zeros_like(q2))
                kd = kall_ref[keys, kv * LANES:(kv + 1) * LANES]
                sc = lax.dot_general(qm, kd, (((1,), (1,)), ((), ())),
                                     preferred_element_type=F32)
                sc = sc + bias_ref[table, head]
                sink = sinks_ref[head]
                m = jnp.maximum(jnp.max(sc, axis=1, keepdims=True), sink)
                p = jnp.exp2(sc - m)
                den = jnp.sum(p, axis=1, keepdims=True) + jnp.exp2(sink - m)
                vd = vall_ref[keys, kv * LANES:(kv + 1) * LANES]
                o2 = jnp.dot(p.astype(BF16), vd, preferred_element_type=F32)
                halves.append(o2 / den)
            o_ref[rows, pair * LANES:(pair + 1) * LANES] = jnp.where(
                low_half, halves[0], halves[1]).astype(BF16)

    uh = uh_ref[...]
    if history_starts_empty:
        uh = jnp.where(i == 0, 0.0, uh)
    u = u_ref[...]
    ue = jnp.concatenate([uh, u], axis=0)
    s2 = ue + pltpu.roll(ue, 1, 0)
    s4 = s2 + pltpu.roll(s2, 2, 0)
    s8 = s4 + pltpu.roll(s4, 4, 0)
    s16 = s8 + pltpu.roll(s8, 8, 0)
    lane = lax.broadcasted_iota(jnp.int32, (1, POOL_DIM), 1)
    g0, g1, g2 = lane < POOL_CH, lane < 2 * POOL_CH, lane < 3 * POOL_CH
    wsum = jnp.where(g0, s2[POOL_HALO:], jnp.where(g1, s4[POOL_HALO:],
                     jnp.where(g2, s8[POOL_HALO:], s16[POOL_HALO:])))
    width = jnp.where(g0, 2.0, jnp.where(g1, 4.0, jnp.where(g2, 8.0, 16.0))).astype(F32)
    pos = (pos0 + i * t + lax.broadcasted_iota(jnp.int32, (t, 1), 0)).astype(F32)
    cnt = jnp.minimum(width, pos + 1.0)
    pooled = wsum / cnt - u
    mixed = jnp.dot(pooled.astype(BF16), wpool_ref[...], preferred_element_type=F32)
    o_ref[:, ATT_DIM:] = (mixed * pscale_ref[...]).astype(BF16)


def _mixer(sinks, q, kd, vd, u, k_hist, v_hist, u_hist, bias, wpool, pscale, *,
           tile, tq, history_starts_empty, pos0):
    nb, s, _ = q.shape
    nt = s // tile
    hk = WINDOW
    own_history = k_hist is None
    if own_history:
        k_hist, v_hist, u_hist = kd, vd, u
        kh_map = lambda bi, i: (bi, jnp.maximum(i * (tile // hk) - 1, 0), 0)
        uh_map = lambda bi, i: (bi, jnp.maximum(i * (tile // POOL_HALO) - 1, 0), 0)
    else:
        kh_map = lambda bi, i: (bi, 0, 0)
        uh_map = lambda bi, i: (bi, 0, 0)

    def tok(c):
        return pl.BlockSpec((None, tile, c), lambda bi, i: (bi, i, 0))

    kern = functools.partial(_mixer_kernel, tq=tq,
                             history_starts_empty=history_starts_empty, pos0=pos0)
    return pl.pallas_call(
        kern,
        grid=(nb, nt),
        in_specs=[
            pl.BlockSpec(memory_space=pltpu.SMEM),
            tok(ATT_DIM),
            tok(KD_DIM),
            pl.BlockSpec((None, hk, KD_DIM), kh_map),
            tok(KD_DIM),
            pl.BlockSpec((None, hk, KD_DIM), kh_map),
            tok(POOL_DIM),
            pl.BlockSpec((None, POOL_HALO, POOL_DIM), uh_map),
            pl.BlockSpec(bias.shape, lambda bi, i: (0, 0, 0, 0)),
            pl.BlockSpec((POOL_DIM, POOL_DIM), lambda bi, i: (0, 0)),
            pl.BlockSpec((1, POOL_DIM), lambda bi, i: (0, 0)),
        ],
        out_specs=tok(D_MODEL),
        out_shape=jax.ShapeDtypeStruct((nb, s, D_MODEL), BF16),
        scratch_shapes=[pltpu.VMEM((hk + tile, KD_DIM), BF16),
                        pltpu.VMEM((hk + tile, KD_DIM), BF16)],
        compiler_params=_params(("parallel", "parallel")),
        name="mixer",
    )(sinks, q, kd, k_hist, vd, v_hist, u, u_hist, bias, wpool, pscale)


def _to_row_tiles(ref, x):
    t = x.shape[0]
    for s in range(ROW_TILE):
        ref[pl.ds(s, t, stride=ROW_TILE), :] = x[:, s * LANES:(s + 1) * LANES]


def _from_row_tiles(ref, t):
    return jnp.concatenate([ref[pl.ds(s, t, stride=ROW_TILE), :] for s in range(ROW_TILE)], axis=1)


def _post_kernel(mix_ref, x_ref, mod_ref, wout_ref, bout_ref, g1_ref, b1_ref, wr_ref, br_ref,
                 cin_ref, x1_ref, h2t_ref, rl_ref, rg_ref, meta_ref, cnt_ref):
    first = (pl.program_id(0) == 0) & (pl.program_id(1) == 0)

    @pl.when(first)
    def _():
        cnt_ref[...] = cin_ref[...]

    t = x_ref.shape[0]
    mix = jnp.dot(mix_ref[...], wout_ref[...], preferred_element_type=F32) + bout_ref[...]
    z = DEEPNORM_ALPHA * x_ref[...] + mod_ref[2:3, :] * mix
    x1 = _layer_norm(z) * g1_ref[...] + b1_ref[...]
    x1_ref[...] = x1
    h2 = _layer_norm(x1) * (1.0 + mod_ref[4:5, :]) + mod_ref[3:4, :]
    _to_row_tiles(h2t_ref, h2)

    h2_hi = h2.astype(BF16)
    h2_lo = (h2 - h2_hi.astype(F32)).astype(BF16)
    contract_last = (((1,), (1,)), ((), ()))
    d_hi = lax.dot_general(wr_ref[...], h2_hi, contract_last, preferred_element_type=F32)
    d_lo = lax.dot_general(wr_ref[0:N_EXPERTS, :], h2_lo, contract_last,
                           preferred_element_type=F32)
    logits = d_hi[:N_EXPERTS] + d_hi[N_EXPERTS:] + d_lo + br_ref[...]
    expert = lax.broadcasted_iota(jnp.int32, (N_EXPERTS, t), 0).astype(F32)
    vals, idxs, hots = [], [], []
    cur = logits
    for _ in range(TOP_K):
        mk = jnp.max(cur, axis=0, keepdims=True)
        ik = jnp.min(jnp.where(cur == mk, expert, float(N_EXPERTS)), axis=0, keepdims=True)
        hot = expert == ik
        cur = jnp.where(hot, -jnp.inf, cur)
        vals.append(mk)
        idxs.append(ik)
        hots.append(hot)
    exps = [jnp.exp(v - vals[0]) for v in vals]
    den = exps[0] + exps[1] + exps[2] + exps[3]
    gates = [e / den for e in exps]

    onehot = jnp.zeros((N_EXPERTS, t), F32)
    for hot in hots:
        onehot = onehot + hot.astype(F32)
    onehot_bf = onehot.astype(BF16)
    tok_r = lax.broadcasted_iota(jnp.int32, (t, t), 0)
    tok_c = lax.broadcasted_iota(jnp.int32, (t, t), 1)
    earlier = jnp.where(tok_r < tok_c, 1.0, 0.0).astype(BF16)
    rank = jnp.dot(onehot_bf, earlier, preferred_element_type=F32)
    exp_r = lax.broadcasted_iota(jnp.int32, (N_EXPERTS, N_EXPERTS), 0)
    exp_c = lax.broadcasted_iota(jnp.int32, (N_EXPERTS, N_EXPERTS), 1)
    lower = jnp.where(exp_c < exp_r, 1.0, 0.0).astype(BF16)
    tile_off = jnp.sum(jnp.dot(lower, onehot_bf, preferred_element_type=F32),
                       axis=1, keepdims=True)
    tile_cnt = jnp.sum(onehot, axis=1, keepdims=True)
    local = (rank + tile_off) * float(ROW_TILE)
    slots = [jnp.sum(jnp.where(hot, local, 0.0), axis=0, keepdims=True) for hot in hots]

    rl_ref[...] = jnp.concatenate(slots + idxs, axis=0).astype(jnp.int32)
    rg_ref[...] = jnp.concatenate(gates + [jnp.zeros_like(g) for g in gates], axis=0)
    lane = lax.broadcasted_iota(jnp.int32, (N_EXPERTS, LANES), 1)
    before = cnt_ref[...]
    meta = jnp.where(lane == 0, tile_cnt, jnp.where(lane == 1, tile_off,
                     jnp.where(lane == 2, before, 0.0)))
    meta_ref[...] = meta.astype(jnp.int32)
    cnt_ref[...] = before + tile_cnt


def _post(mix, x, mod, wout, bout, g1, b1, wr, br, counts_in, tile, mod_row0):
    nb, s, _ = x.shape
    nt = s // tile

    def tok(c):
        return pl.BlockSpec((None, tile, c), lambda bi, i: (bi, i, 0))

    def whole(shape):
        return pl.BlockSpec(shape, lambda bi, i: tuple(0 for _ in shape))

    return pl.pallas_call(
        _post_kernel,
        grid=(nb, nt),
        in_specs=[
            tok(D_MODEL), tok(D_MODEL),
            pl.BlockSpec((None, 6, D_MODEL), lambda bi, i: (bi + mod_row0, 0, 0)),
            whole((D_MODEL, D_MODEL)), whole((1, D_MODEL)),
            whole((1, D_MODEL)), whole((1, D_MODEL)),
            whole((2 * N_EXPERTS, D_MODEL)), whole((N_EXPERTS, 1)),
            whole((N_EXPERTS, LANES)),
        ],
        out_specs=[tok(D_MODEL),
                   pl.BlockSpec((None, tile * ROW_TILE, LANES), lambda bi, i: (bi, i, 0)),
                   pl.BlockSpec((None, None, 2 * TOP_K, tile), lambda bi, i: (bi, i, 0, 0)),
                   pl.BlockSpec((None, None, 2 * TOP_K, tile), lambda bi, i: (bi, i, 0, 0)),
                   pl.BlockSpec((None, None, N_EXPERTS, LANES), lambda bi, i: (bi, i, 0, 0)),
                   whole((N_EXPERTS, LANES))],
        out_shape=[
            jax.ShapeDtypeStruct((nb, s, D_MODEL), F32),
            jax.ShapeDtypeStruct((nb, s * ROW_TILE, LANES), F32),
            jax.ShapeDtypeStruct((nb, nt, 2 * TOP_K, tile), jnp.int32),
            jax.ShapeDtypeStruct((nb, nt, 2 * TOP_K, tile), F32),
            jax.ShapeDtypeStruct((nb, nt, N_EXPERTS, LANES), jnp.int32),
            jax.ShapeDtypeStruct((N_EXPERTS, LANES), F32),
        ],
        compiler_params=_params(("arbitrary", "arbitrary")),
        name="post",
    )(mix, x, mod, wout, bout, g1, b1, wr, br, counts_in)


def _tile_rows(row, n=1):
    return pl.ds(pl.multiple_of(row * ROW_TILE, ROW_TILE), n * ROW_TILE)


def _rows_at(first_row):
    return pl.ds(pl.multiple_of(first_row, ROW_TILE), ROW_TILE)


def _copy_plan(cnt, off, dst, n_bits):
    n_tiles = cnt.shape[0]
    bits = jnp.arange(PLAN_BITS, dtype=jnp.int32)[None, :, None]
    c = cnt[:, None, :]
    valid = ((c >> bits) & 1) * (bits < n_bits)
    done = c & ((1 << bits) - 1)
    pos = jnp.cumsum(valid, axis=-1) - valid
    place = (pos[..., :, None] == jnp.arange(N_EXPERTS, dtype=jnp.int32)) & (valid[..., :, None] == 1)
    pack = lambda v: jnp.sum(jnp.where(place, v[..., :, None], 0), axis=-2)
    local = pack(off[:, None, :] + done)
    glob = pack(dst[:, None, :] + done)
    n = jnp.sum(valid, axis=-1)
    used = 2 * PLAN_BITS * N_EXPERTS + PLAN_BITS
    plan = jnp.concatenate([local.reshape(n_tiles, -1), glob.reshape(n_tiles, -1), n,
                            jnp.zeros((n_tiles, PLAN_WORDS - used), jnp.int32)], axis=1)
    return plan.reshape(-1).astype(jnp.int32)


def _start_run_copies(plan_ref, t, make_copy):
    for b in range(t.bit_length()):
        def piece(r, carry, b=b):
            j = b * N_EXPERTS + r
            make_copy(plan_ref[j], plan_ref[PLAN_BITS * N_EXPERTS + j], 1 << b).start()
            return carry

        lax.fori_loop(0, plan_ref[2 * PLAN_BITS * N_EXPERTS + b], piece, 0)


def _dispatch_kernel(pe_ref, plan_ref, ls_ref, h_ref, *rest, clear_blocks, ls_blocked):
    xs_ref, stage, zbuf, sems, sem = rest[-5:]
    i = pl.program_id(0)
    t = h_ref.shape[0] // ROW_TILE

    if clear_blocks:
        @pl.when(i == 0)
        def _():
            zbuf[...] = jnp.zeros_like(zbuf)

            def clear_copy(row):
                return pltpu.make_async_copy(zbuf, xs_ref.at[_tile_rows(row, MOE_ROWS), :], sem)

            def clear_segment_end(e, n):
                end = pe_ref[e]
                start = jnp.where(e == 0, 0, pe_ref[jnp.maximum(e - 1, 0)])
                for back in range(1, clear_blocks + 1):
                    row = end - back * MOE_ROWS

                    @pl.when(row >= start)
                    def _():
                        clear_copy(row).start()
                    n = n + (row >= start).astype(jnp.int32)
                return n

            n_started = lax.fori_loop(0, N_EXPERTS, clear_segment_end, 0)
            n_blocks = xs_ref.shape[0] // (MOE_ROWS * ROW_TILE)
            n_valid = pe_ref[N_EXPERTS - 1] // MOE_ROWS

            def clear_past_end(b, carry):
                clear_copy(b * MOE_ROWS).start()
                return carry

            lax.fori_loop(n_valid, n_blocks, clear_past_end, 0)

            def wait_one(_, carry):
                clear_copy(0).wait()
                return carry

            lax.fori_loop(0, n_started + n_blocks - n_valid, wait_one, 0)

    base = 0 if ls_blocked else i * (t * TOP_K)
    slot = i % 2

    def permute_and_send(s):
        def scatter(g, carry):
            for u in range(TOKEN_UNROLL):
                tok = g * TOKEN_UNROLL + u
                tile = h_ref[_tile_rows(tok), :]
                for k in range(TOP_K):
                    stage[s, _rows_at(ls_ref[base + k * t + tok]), :] = tile
            return carry

        lax.fori_loop(0, t // TOKEN_UNROLL, scatter, 0)
        _start_run_copies(
            plan_ref, t,
            lambda loc, glob, n: pltpu.make_async_copy(stage.at[s, _tile_rows(loc, n), :],
                                                       xs_ref.at[_tile_rows(glob, n), :],
                                                       sems.at[s]))

    for s in range(2):
        pl.when(slot == s)(functools.partial(permute_and_send, s))

    def wait_stage(s):
        pltpu.make_async_copy(stage.at[s], xs_ref.at[_tile_rows(0, t * TOP_K), :], sems.at[s]).wait()

    @pl.when(i > 0)
    def _():
        wait_stage(1 - slot)

    @pl.when(i == pl.num_programs(0) - 1)
    def _():
        wait_stage(slot)


def _dispatch(plan, pad_ends, ls, h2t, xs, n_rows, tile, later_rows=0):
    n_tiles = h2t.shape[0] // (tile * ROW_TILE)
    first = xs is None
    clear_blocks = -(-(later_rows + MOE_ROWS - 1) // MOE_ROWS) if first else 0
    ls_blocked = (tile * TOP_K) % 1024 == 0
    ls_spec = (pl.BlockSpec((tile * TOP_K,), lambda i, *_: (i,), memory_space=pltpu.SMEM)
               if ls_blocked else pl.BlockSpec(memory_space=pltpu.SMEM))
    in_specs = [pl.BlockSpec((PLAN_WORDS,), lambda i, *_: (i,), memory_space=pltpu.SMEM),
                ls_spec, pl.BlockSpec((tile * ROW_TILE, LANES), lambda i, *_: (i, 0))]
    args = [pad_ends, plan, ls, h2t]
    if not first:
        in_specs.append(pl.BlockSpec(memory_space=pl.ANY))
        args.append(xs)
    grid_spec = pltpu.PrefetchScalarGridSpec(
        num_scalar_prefetch=1,
        grid=(n_tiles,),
        in_specs=in_specs,
        out_specs=pl.BlockSpec(memory_space=pl.ANY),
        scratch_shapes=[pltpu.VMEM((2, tile * TOP_K * ROW_TILE, LANES), F32),
                        pltpu.VMEM((MOE_ROWS * ROW_TILE, LANES), F32),
                        pltpu.SemaphoreType.DMA((2,)),
                        pltpu.SemaphoreType.DMA],
    )
    return pl.pallas_call(
        functools.partial(_dispatch_kernel, clear_blocks=clear_blocks, ls_blocked=ls_blocked),
        grid_spec=grid_spec,
        out_shape=jax.ShapeDtypeStruct((n_rows * ROW_TILE, LANES), F32),
        input_output_aliases={} if first else {4: 0},
        compiler_params=_params(("arbitrary",)),
        name="dispatch",
    )(*args)


def _moe_kernel(be_ref, nv_ref, nx_ref, par_ref, xs_ref, bgu_ref, bdn_ref, wgu_hbm, wdn_hbm,
                ys_ref, wgu_f32, wdn_f32, wgu_bf, wdn_bf, sems, *, layer):
    i = pl.program_id(0)
    e = be_ref[i]
    prev = be_ref[jnp.maximum(i - 1, 0)]
    valid = i < nv_ref[0]
    rows = 128

    def weight_copies(expert, slot):
        return (pltpu.make_async_copy(wgu_hbm.at[layer, expert], wgu_f32.at[slot], sems.at[slot]),
                pltpu.make_async_copy(wdn_hbm.at[layer, expert], wdn_f32.at[slot], sems.at[slot]))

    @pl.when(valid & ((i == 0) | (e != prev)))
    def _():
        slot = par_ref[i]

        @pl.when(i == 0)
        def _():
            for cp in weight_copies(e, slot):
                cp.start()

        for cp in weight_copies(e, slot):
            cp.wait()
        nxt = nx_ref[i]

        @pl.when(nxt < N_EXPERTS)
        def _():
            for cp in weight_copies(nxt, 1 - slot):
                cp.start()

        def cast(r, carry):
            sl = pl.ds(pl.multiple_of(r * rows, rows), rows)
            wgu_bf[sl, :] = wgu_f32[slot, sl, :].astype(BF16)
            wdn_bf[sl, :] = wdn_f32[slot, sl, :].astype(BF16)
            return carry
        lax.fori_loop(0, D_MODEL // rows, cast, 0)

    @pl.when(valid)
    def _():
        x = _from_row_tiles(xs_ref, MOE_ROWS).astype(BF16)
        gu = jnp.dot(x, wgu_bf[...], preferred_element_type=F32) + bgu_ref[...]
        gate = jnp.minimum(gu[:, :D_FF], SWIGLU_LIMIT)
        up = jnp.clip(gu[:, D_FF:], -SWIGLU_LIMIT, SWIGLU_LIMIT)
        a = gate * jax.nn.sigmoid(SWIGLU_ALPHA * gate) * (up + 1.0)
        y = jnp.dot(a.astype(BF16), wdn_bf[...], preferred_element_type=F32) + bdn_ref[...]
        _to_row_tiles(ys_ref, y)

    @pl.when(jnp.logical_not(valid))
    def _():
        ys_ref[...] = jnp.zeros_like(ys_ref)


def _moe(block_e, n_valid, next_e, parity, xs, w_gu, b_gu, w_down, b_down, layer):
    n_rows = xs.shape[0] // ROW_TILE
    n_blocks = n_rows // MOE_ROWS
    grid_spec = pltpu.PrefetchScalarGridSpec(
        num_scalar_prefetch=4,
        grid=(n_blocks,),
        in_specs=[
            pl.BlockSpec((MOE_ROWS * ROW_TILE, LANES),
                         lambda i, be, nv, *_: (jnp.minimum(i, nv[0] - 1), 0)),
            pl.BlockSpec((None, None, 1, 2 * D_FF), lambda i, be, *_: (layer, be[i], 0, 0)),
            pl.BlockSpec((None, None, 1, D_MODEL), lambda i, be, *_: (layer, be[i], 0, 0)),
            pl.BlockSpec(memory_space=pl.ANY),
            pl.BlockSpec(memory_space=pl.ANY),
        ],
        out_specs=pl.BlockSpec((MOE_ROWS * ROW_TILE, LANES), lambda i, *_: (i, 0)),
        scratch_shapes=[pltpu.VMEM((2, D_MODEL, 2 * D_FF), F32),
                        pltpu.VMEM((2, D_FF, D_MODEL), F32),
                        pltpu.VMEM((D_MODEL, 2 * D_FF), BF16),
                        pltpu.VMEM((D_FF, D_MODEL), BF16),
                        pltpu.SemaphoreType.DMA((2,))],
    )
    return pl.pallas_call(
        functools.partial(_moe_kernel, layer=layer),
        grid_spec=grid_spec,
        out_shape=jax.ShapeDtypeStruct((n_rows * ROW_TILE, LANES), F32),
        compiler_params=_params(("arbitrary",)),
        name="moe",
    )(block_e, n_valid, next_e, parity, xs, b_gu.reshape(DEPTH, N_EXPERTS, 1, 2 * D_FF),
      b_down.reshape(DEPTH, N_EXPERTS, 1, D_MODEL), w_gu, w_down)


def _combine_kernel(plan_ref, next_plan_ref, ls_ref, gate_ref, x1_ref, gf_ref, g2_ref, b2_ref,
                    ys_ref, o_ref, stage, ft, sems, *, ls_blocked):
    i = pl.program_id(0)
    t = x1_ref.shape[0]
    slot = i % 2

    def start_fetch(tile_plan_ref, s):
        _start_run_copies(
            tile_plan_ref, t,
            lambda loc, glob, n: pltpu.make_async_copy(ys_ref.at[_tile_rows(glob, n), :],
                                                       stage.at[s, _tile_rows(loc, n), :],
                                                       sems.at[s]))

    @pl.when(i == 0)
    def _():
        start_fetch(plan_ref, 0)

    for s in range(2):
        @pl.when((i + 1 < pl.num_programs(0)) & (slot == 1 - s))
        def _(s=s):
            start_fetch(next_plan_ref, s)

    pltpu.make_async_copy(ys_ref.at[_tile_rows(0, t * TOP_K), :], stage.at[slot],
                          sems.at[slot]).wait()

    base = 0 if ls_blocked else i * (t * TOP_K)

    def gate_sum(s):
        def gather(g, carry):
            for u in range(TOKEN_UNROLL):
                tok = g * TOKEN_UNROLL + u
                acc = None
                for k in range(TOP_K):
                    j = base + k * t + tok
                    v = stage[s, _rows_at(ls_ref[j]), :] * gate_ref[j]
                    acc = v if acc is None else acc + v
                ft[_tile_rows(tok), :] = acc
            return carry

        lax.fori_loop(0, t // TOKEN_UNROLL, gather, 0)

    for s in range(2):
        pl.when(slot == s)(functools.partial(gate_sum, s))
    f = _from_row_tiles(ft, t)
    z = DEEPNORM_ALPHA * x1_ref[...] + gf_ref[...] * f
    o_ref[...] = _layer_norm(z) * g2_ref[...] + b2_ref[...]


def _combine(plan, ls, gates, x1, gf, g2, b2, ys, tile, tiles_per_seq):
    n = x1.shape[0]
    n_tiles = n // tile
    ls_blocked = (tile * TOP_K) % 1024 == 0

    def smem_vec():
        if ls_blocked:
            return pl.BlockSpec((tile * TOP_K,), lambda i, *_: (i,), memory_space=pltpu.SMEM)
        return pl.BlockSpec(memory_space=pltpu.SMEM)

    grid_spec = pltpu.PrefetchScalarGridSpec(
        num_scalar_prefetch=0,
        grid=(n_tiles,),
        in_specs=[
            pl.BlockSpec((PLAN_WORDS,), lambda i: (i,), memory_space=pltpu.SMEM),
            pl.BlockSpec((PLAN_WORDS,), lambda i: (jnp.minimum(i + 1, n_tiles - 1),),
                         memory_space=pltpu.SMEM),
            smem_vec(), smem_vec(),
            pl.BlockSpec((tile, D_MODEL), lambda i, *_: (i, 0)),
            pl.BlockSpec((None, 1, D_MODEL), lambda i, *_: (i // tiles_per_seq, 0, 0)),
            pl.BlockSpec((1, D_MODEL), lambda i, *_: (0, 0)),
            pl.BlockSpec((1, D_MODEL), lambda i, *_: (0, 0)),
            pl.BlockSpec(memory_space=pl.ANY),
        ],
        out_specs=pl.BlockSpec((tile, D_MODEL), lambda i, *_: (i, 0)),
        scratch_shapes=[pltpu.VMEM((2, tile * TOP_K * ROW_TILE, LANES), F32),
                        pltpu.VMEM((tile * ROW_TILE, LANES), F32),
                        pltpu.SemaphoreType.DMA((2,))],
    )
    return pl.pallas_call(
        functools.partial(_combine_kernel, ls_blocked=ls_blocked),
        grid_spec=grid_spec,
        out_shape=jax.ShapeDtypeStruct((n, D_MODEL), F32),
        compiler_params=_params(("arbitrary",)),
        name="combine",
    )(plan, plan, ls, gates, x1, gf, g2, b2, ys)


def _alibi_slopes():
    return jnp.exp2(-ALIBI_MAX * (jnp.arange(N_HEADS, dtype=F32) + 1.0) / N_HEADS)


def _prompt_bias(slopes):
    r = jnp.arange(Q_TILE)
    j = jnp.arange(WINDOW + Q_TILE)
    dc = (WINDOW + r)[:, None] // CHUNK - j[None, :] // CHUNK
    vis = (dc >= 0) & (dc <= WIN_CHUNKS)
    dist = jnp.abs(r[:, None] + WINDOW - j[None, :]).astype(F32)
    bias = jnp.where(vis[None], -slopes[:, None, None] * dist[None] * LOG2_E, NEG_INF)
    return jnp.stack([bias, jnp.where(j[None, None, :] < WINDOW, NEG_INF, bias)])


def _sample_bias(slopes, t, hist):
    qpos = PAST_LEN + jnp.arange(t)
    kpos = PAST_LEN - hist + jnp.arange(hist + t)
    dc = qpos[:, None] // CHUNK - kpos[None, :] // CHUNK
    vis = (dc >= 0) & (dc <= WIN_CHUNKS) & (kpos[None, :] >= 0)
    dist = jnp.abs(qpos[:, None] - kpos[None, :]).astype(F32)
    return jnp.where(vis[None], -slopes[:, None, None] * dist[None] * LOG2_E, NEG_INF)[None]


def _dup_heads(w):
    lead = w.shape[:-1]
    w4 = w.reshape(lead + (N_KV, HEAD_DIM))
    return jnp.concatenate([w4, w4], axis=-1).reshape(lead + (KD_DIM,))


def _proj_weights(w_in, b_in):
    q0, k0, v0, u0 = 0, ATT_DIM, ATT_DIM + KV_DIM, ATT_DIM + 2 * KV_DIM
    q_scale = ATTN_SCALE * LOG2_E
    w = jnp.concatenate([w_in[:, q0:k0] * q_scale, _dup_heads(w_in[:, k0:v0]),
                         _dup_heads(w_in[:, v0:u0]), w_in[:, u0:]], axis=1)
    b = jnp.concatenate([b_in[q0:k0] * q_scale, _dup_heads(b_in[k0:v0]),
                         _dup_heads(b_in[v0:u0]), b_in[u0:]], axis=0)
    return w.astype(BF16), b.reshape(1, PROJ_COLS)


def _pool_weight(w_pool):
    n = len(POOL_WINDOWS)
    eye = jnp.eye(n, dtype=w_pool.dtype)
    return jnp.einsum('gcd,gh->gchd', w_pool, eye).reshape(POOL_DIM, POOL_DIM).astype(BF16)


def _undup(kd, nb, rows):
    return kd[:, -rows:].reshape(nb, rows, N_KV, LANES)[..., :HEAD_DIM].astype(F32)


def kernel(x_prompt, x_sample, c_prompt, c_sample, cache_k, cache_v, state_pool, w_ada, b_ada,
           w_in, b_in, sinks, w_pool, pool_scale, w_out, b_out, ln1_g, ln1_b, ln2_g, ln2_b,
           w_router, b_router, w_gu, b_gu, w_down, b_down):
    nbp, seq, _ = x_prompt.shape
    nbs, tdec, _ = x_sample.shape
    hist = cache_k.shape[2]
    n_p, n_s = nbp * seq, nbs * tdec
    n_asg = (n_p + n_s) * TOP_K
    n_blocks = -(-(n_asg + N_EXPERTS * (MOE_ROWS - 1)) // MOE_ROWS)
    n_rows = n_blocks * MOE_ROWS
    assert hist == WINDOW and tdec >= POOL_PAD and seq % TOK_TILE == 0

    nb_all = nbp + nbs
    nb_pad = -(-nb_all // SUBLANES) * SUBLANES
    c_all = jnp.concatenate([c_prompt, c_sample, jnp.zeros((nb_pad - nb_all, D_MODEL), F32)], 0)
    mod_all = _adaln(c_all, w_ada, b_ada).reshape(DEPTH, nb_pad, 6, D_MODEL)

    slopes = _alibi_slopes()
    bias_p = _prompt_bias(slopes)
    bias_s = _sample_bias(slopes, tdec, hist)
    row = lambda v: v.reshape(1, -1)

    yp, ys_tok = x_prompt, x_sample
    outs = {k: [] for k in ("kp", "vp", "pp", "ks", "vs", "ps")}
    for l in range(DEPTH):
        mod = mod_all[l]
        w_proj, b_proj = _proj_weights(w_in[l], b_in[l])
        wpool = _pool_weight(w_pool[l])
        pscale = row(pool_scale[l])
        wout = w_out[l].astype(BF16)
        wr_t = w_router[l].T
        wr_hi = wr_t.astype(BF16)
        wr = jnp.concatenate([wr_hi, (wr_t - wr_hi.astype(F32)).astype(BF16)], axis=0)
        br = b_router[l].reshape(N_EXPERTS, 1)

        q, kd, vd, u = _inproj(yp, mod, w_proj, b_proj, TOK_TILE, 0)
        sinks2 = sinks[l] * LOG2_E
        mix_p = _mixer(sinks2, q, kd, vd, u, None, None, None, bias_p, wpool, pscale,
                       tile=TOK_TILE, tq=Q_TILE, history_starts_empty=True, pos0=0)
        outs["kp"].append(_undup(kd, nbp, WINDOW))
        outs["vp"].append(_undup(vd, nbp, WINDOW))
        outs["pp"].append(u[:, -POOL_PAD:])

        qs, kds, vds, us = _inproj(ys_tok, mod, w_proj, b_proj, tdec, nbp)
        ck = _dup_heads(cache_k[l].reshape(nbs, hist, KV_DIM)).astype(BF16)
        cv = _dup_heads(cache_v[l].reshape(nbs, hist, KV_DIM)).astype(BF16)
        sp = jnp.pad(state_pool[l], ((0, 0), (POOL_HALO - POOL_PAD, 0), (0, 0)))
        mix_s = _mixer(sinks2, qs, kds, vds, us, ck, cv, sp, bias_s, wpool, pscale,
                       tile=tdec, tq=tdec, history_starts_empty=False, pos0=PAST_LEN)
        k_new = _undup(kds, nbs, tdec)
        v_new = _undup(vds, nbs, tdec)
        outs["ks"].append(jnp.concatenate([cache_k[l], k_new], axis=1)[:, -hist:])
        outs["vs"].append(jnp.concatenate([cache_v[l], v_new], axis=1)[:, -hist:])
        outs["ps"].append(us[:, -POOL_PAD:])

        post_w = (wout, row(b_out[l]), row(ln1_g[l]), row(ln1_b[l]), wr, br)
        zero_counts = jnp.zeros((N_EXPERTS, LANES), F32)
        x1p, h2tp, rlp, rgp, meta_p, cnt_p = _post(mix_p, yp, mod, *post_w, zero_counts, TOK_TILE, 0)
        x1s, h2ts, rls, rgs, meta_s, cnt = _post(mix_s, ys_tok, mod, *post_w, cnt_p, tdec, nbp)

        counts = cnt[:, 0].astype(jnp.int32)
        padded = (counts + MOE_ROWS - 1) // MOE_ROWS * MOE_ROWS
        pad_ends = jnp.cumsum(padded).astype(jnp.int32)
        pad_starts = pad_ends - padded
        block_start = jnp.arange(n_blocks, dtype=jnp.int32) * MOE_ROWS
        block_e = jnp.minimum(jnp.sum(pad_ends[None, :] <= block_start[:, None], axis=1),
                              N_EXPERTS - 1).astype(jnp.int32)
        n_valid = (pad_ends[-1:] // MOE_ROWS).astype(jnp.int32)
        ids = jnp.arange(N_EXPERTS, dtype=jnp.int32)
        present = padded > 0
        later = (ids[None, :] > ids[:, None]) & present[None, :]
        next_present = jnp.min(jnp.where(later, ids[None, :], N_EXPERTS), axis=1)
        ordinal = jnp.cumsum(present.astype(jnp.int32)) - 1
        is_block_e = block_e[:, None] == ids[None, :]
        next_e = jnp.sum(jnp.where(is_block_e, next_present[None, :], 0), axis=1).astype(jnp.int32)
        parity = jnp.sum(jnp.where(is_block_e, ordinal[None, :] % 2, 0), axis=1).astype(jnp.int32)

        def copy_plan(meta, tile):
            m = meta.reshape(-1, N_EXPERTS, LANES)
            return _copy_plan(m[:, :, 0], m[:, :, 1], pad_starts[None, :] + m[:, :, 2],
                              tile.bit_length())

        def per_assignment(r, n):
            return r[:, :, :TOP_K, :].reshape(n * TOP_K)

        plan_p, plan_s = copy_plan(meta_p, TOK_TILE), copy_plan(meta_s, tdec)
        ls_p, ls_s = per_assignment(rlp, n_p), per_assignment(rls, n_s)

        xs = _dispatch(plan_p, pad_ends, ls_p, h2tp.reshape(n_p * ROW_TILE, LANES), None,
                       n_rows, TOK_TILE, later_rows=n_s)
        xs = _dispatch(plan_s, pad_ends, ls_s, h2ts.reshape(n_s * ROW_TILE, LANES), xs,
                       n_rows, tdec)
        ye = _moe(block_e, n_valid, next_e, parity, xs, w_gu, b_gu, w_down, b_down, l)

        g2, b2 = row(ln2_g[l]), row(ln2_b[l])
        gf = mod[:, 5].reshape(nb_pad, 1, D_MODEL)
        yp = _combine(plan_p, ls_p, per_assignment(rgp, n_p), x1p.reshape(n_p, D_MODEL),
                      gf[:nbp], g2, b2, ye, TOK_TILE, seq // TOK_TILE).reshape(nbp, seq, D_MODEL)
        ys_tok = _combine(plan_s, ls_s, per_assignment(rgs, n_s), x1s.reshape(n_s, D_MODEL),
                          gf[nbp:nb_all], g2, b2, ye, tdec, 1).reshape(nbs, tdec, D_MODEL)

    st = lambda k: jnp.stack(outs[k])
    return (yp, ys_tok, st("kp"), st("vp"), st("pp"), st("ks"), st("vs"), st("ps"))
```

```python
import functools

import jax
import jax.numpy as jnp
from jax import lax
from jax.experimental import pallas as pl
from jax.experimental.pallas import tpu as pltpu

F32 = jnp.float32
BF16 = jnp.bfloat16

D_MODEL = 1024
DEPTH = 2
CHUNK = 64
WINDOW = 128
WIN_CHUNKS = WINDOW // CHUNK
HEAD_DIM = 64
ATT_DIM = 768
N_HEADS = 12
N_KV = 4
GROUP = N_HEADS // N_KV
KV_DIM = N_KV * HEAD_DIM
ATTN_SCALE = HEAD_DIM ** -0.5
LOG2_E = 1.4426950408889634
ALIBI_MAX = 8.0
NEG_INF = -1e30
POOL_DIM = D_MODEL - ATT_DIM
POOL_WINDOWS = (2, 4, 8, 16)
POOL_CH = POOL_DIM // len(POOL_WINDOWS)
POOL_PAD = max(POOL_WINDOWS) - 1
N_EXPERTS = 32
TOP_K = 4
D_FF = D_MODEL
SWIGLU_LIMIT = 7.0
SWIGLU_ALPHA = 1.702
LN_EPS = 1e-5
DEEPNORM_ALPHA = (2.0 * DEPTH) ** 0.25
PAST_LEN = 1024

LANES = 128
SUBLANES = 8
VMEM_LIMIT = 52 * 1024 * 1024

TOK_TILE = 512
Q_TILE = 128
POOL_HALO = 16
KD_DIM = N_KV * LANES
PROJ_COLS = ATT_DIM + 2 * KD_DIM + POOL_DIM
MOE_ROWS = 512
TOKEN_UNROLL = 8
PLAN_BITS = TOK_TILE.bit_length()
PLAN_WORDS = 1024
assert 2 * PLAN_BITS * N_EXPERTS + PLAN_BITS <= PLAN_WORDS
ROW_TILE = D_MODEL // LANES
assert ROW_TILE == SUBLANES
PACKED_ROWS = ROW_TILE // 2


def _params(sem, vmem=VMEM_LIMIT):
    return pltpu.CompilerParams(dimension_semantics=sem, vmem_limit_bytes=vmem)


def _layer_norm(x):
    mu = jnp.mean(x, axis=-1, keepdims=True)
    xc = x - mu
    var = jnp.mean(xc * xc, axis=-1, keepdims=True)
    return xc * lax.rsqrt(var + LN_EPS)


def _adaln_kernel(c_ref, w_ref, b_ref, o_ref):
    c = c_ref[...]
    s = c * jax.nn.sigmoid(c)
    o_ref[...] = jnp.dot(s, w_ref[...], preferred_element_type=F32,
                         precision=lax.Precision.HIGHEST) + b_ref[...]


def _adaln(c_all, w_ada, b_ada):
    nb = c_all.shape[0]
    ncol = 6 * D_MODEL
    tn = D_MODEL
    return pl.pallas_call(
        _adaln_kernel,
        grid=(DEPTH, ncol // tn),
        in_specs=[
            pl.BlockSpec((nb, D_MODEL), lambda l, j: (0, 0)),
            pl.BlockSpec((None, D_MODEL, tn), lambda l, j: (l, 0, j)),
            pl.BlockSpec((None, 1, tn), lambda l, j: (l, 0, j)),
        ],
        out_specs=pl.BlockSpec((None, nb, tn), lambda l, j: (l, 0, j)),
        out_shape=jax.ShapeDtypeStruct((DEPTH, nb, ncol), F32),
        compiler_params=_params(("parallel", "parallel")),
        name="adaln",
    )(c_all, w_ada, b_ada.reshape(DEPTH, 1, ncol))


def _inproj_kernel(x_ref, mod_ref, w_ref, b_ref, q_ref, kd_ref, vd_ref, u_ref):
    h = _layer_norm(x_ref[...]) * (1.0 + mod_ref[1:2, :]) + mod_ref[0:1, :]
    p = jnp.dot(h.astype(BF16), w_ref[...], preferred_element_type=F32) + b_ref[...]
    q_ref[...] = p[:, :ATT_DIM].astype(BF16)
    kd_ref[...] = p[:, ATT_DIM:ATT_DIM + KD_DIM].astype(BF16)
    vd_ref[...] = p[:, ATT_DIM + KD_DIM:ATT_DIM + 2 * KD_DIM].astype(BF16)
    u_ref[...] = p[:, ATT_DIM + 2 * KD_DIM:]


def _inproj(x, mod, w, b, tile, mod_row0):
    nb, s, _ = x.shape
    nt = s // tile

    def tok(c):
        return pl.BlockSpec((None, tile, c), lambda bi, i: (bi, i, 0))

    return pl.pallas_call(
        _inproj_kernel,
        grid=(nb, nt),
        in_specs=[
            tok(D_MODEL),
            pl.BlockSpec((None, 6, D_MODEL), lambda bi, i: (bi + mod_row0, 0, 0)),
            pl.BlockSpec((D_MODEL, PROJ_COLS), lambda bi, i: (0, 0)),
            pl.BlockSpec((1, PROJ_COLS), lambda bi, i: (0, 0)),
        ],
        out_specs=[tok(ATT_DIM), tok(KD_DIM), tok(KD_DIM), tok(POOL_DIM)],
        out_shape=[
            jax.ShapeDtypeStruct((nb, s, ATT_DIM), BF16),
            jax.ShapeDtypeStruct((nb, s, KD_DIM), BF16),
            jax.ShapeDtypeStruct((nb, s, KD_DIM), BF16),
            jax.ShapeDtypeStruct((nb, s, POOL_DIM), F32),
        ],
        compiler_params=_params(("parallel", "parallel")),
        name="inproj",
    )(x, mod, w, b)


def _mixer_kernel(sinks_ref, q_ref, kd_ref, kdh_ref, vd_ref, vdh_ref, u_ref, uh_ref,
                  bias_ref, wpool_ref, pscale_ref, o_ref, kall_ref, vall_ref,
                  *, tq, history_starts_empty, pos0):
    i = pl.program_id(1)
    t = q_ref.shape[0]
    hk = kdh_ref.shape[0]
    nk = hk + tq
    kall_ref[0:hk, :] = kdh_ref[...]
    kall_ref[hk:, :] = kd_ref[...]
    vall_ref[0:hk, :] = vdh_ref[...]
    vall_ref[hk:, :] = vd_ref[...]

    low_half = lax.broadcasted_iota(jnp.int32, (tq, LANES), 1) < HEAD_DIM
    for s in range(t // tq):
        rows = slice(s * tq, (s + 1) * tq)
        keys = slice(s * tq, s * tq + nk)
        table = (i == 0).astype(jnp.int32) if (history_starts_empty and s == 0) else 0
        for pair in range(N_HEADS // 2):
            q2 = q_ref[rows, pair * LANES:(pair + 1) * LANES]
            halves = []
            for half in range(2):
                head = 2 * pair + half
                kv = head // GROUP
                qm = jnp.where(low_half if half == 0 else ~low_half, q2, jnp.zeros_like(q2))
                kd = kall_ref[keys, kv * LANES:(kv + 1) * LANES]
                sc = lax.dot_general(qm, kd, (((1,), (1,)), ((), ())),
                                     preferred_element_type=F32)
                sc = sc + bias_ref[table, head]
                sink = sinks_ref[head]
                m = jnp.maximum(jnp.max(sc, axis=1, keepdims=True), sink)
                p = jnp.exp2(sc - m)
                den = jnp.sum(p, axis=1, keepdims=True) + jnp.exp2(sink - m)
                vd = vall_ref[keys, kv * LANES:(kv + 1) * LANES]
                o2 = jnp.dot(p.astype(BF16), vd, preferred_element_type=F32)
                halves.append(o2 / den)
            o_ref[rows, pair * LANES:(pair + 1) * LANES] = jnp.where(
                low_half, halves[0], halves[1]).astype(BF16)

    uh = uh_ref[...]
    if history_starts_empty:
        uh = jnp.where(i == 0, 0.0, uh)
    u = u_ref[...]
    ue = jnp.concatenate([uh, u], axis=0)
    s2 = ue + pltpu.roll(ue, 1, 0)
    s4 = s2 + pltpu.roll(s2, 2, 0)
    s8 = s4 + pltpu.roll(s4, 4, 0)
    s16 = s8 + pltpu.roll(s8, 8, 0)
    lane = lax.broadcasted_iota(jnp.int32, (1, POOL_DIM), 1)
    g0, g1, g2 = lane < POOL_CH, lane < 2 * POOL_CH, lane < 3 * POOL_CH
    wsum = jnp.where(g0, s2[POOL_HALO:], jnp.where(g1, s4[POOL_HALO:],
                     jnp.where(g2, s8[POOL_HALO:], s16[POOL_HALO:])))
    width = jnp.where(g0, 2.0, jnp.where(g1, 4.0, jnp.where(g2, 8.0, 16.0))).astype(F32)
    pos = (pos0 + i * t + lax.broadcasted_iota(jnp.int32, (t, 1), 0)).astype(F32)
    cnt = jnp.minimum(width, pos + 1.0)
    pooled = wsum / cnt - u
    mixed = jnp.dot(pooled.astype(BF16), wpool_ref[...], preferred_element_type=F32)
    o_ref[:, ATT_DIM:] = (mixed * pscale_ref[...]).astype(BF16)


def _mixer(sinks, q, kd, vd, u, k_hist, v_hist, u_hist, bias, wpool, pscale, *,
           tile, tq, history_starts_empty, pos0):
    nb, s, _ = q.shape
    nt = s // tile
    hk = WINDOW
    own_history = k_hist is None
    if own_history:
        k_hist, v_hist, u_hist = kd, vd, u
        kh_map = lambda bi, i: (bi, jnp.maximum(i * (tile // hk) - 1, 0), 0)
        uh_map = lambda bi, i: (bi, jnp.maximum(i * (tile // POOL_HALO) - 1, 0), 0)
    else:
        kh_map = lambda bi, i: (bi, 0, 0)
        uh_map = lambda bi, i: (bi, 0, 0)

    def tok(c):
        return pl.BlockSpec((None, tile, c), lambda bi, i: (bi, i, 0))

    kern = functools.partial(_mixer_kernel, tq=tq,
                             history_starts_empty=history_starts_empty, pos0=pos0)
    return pl.pallas_call(
        kern,
        grid=(nb, nt),
        in_specs=[
            pl.BlockSpec(memory_space=pltpu.SMEM),
            tok(ATT_DIM),
            tok(KD_DIM),
            pl.BlockSpec((None, hk, KD_DIM), kh_map),
            tok(KD_DIM),
            pl.BlockSpec((None, hk, KD_DIM), kh_map),
            tok(POOL_DIM),
            pl.BlockSpec((None, POOL_HALO, POOL_DIM), uh_map),
            pl.BlockSpec(bias.shape, lambda bi, i: (0, 0, 0, 0)),
            pl.BlockSpec((POOL_DIM, POOL_DIM), lambda bi, i: (0, 0)),
            pl.BlockSpec((1, POOL_DIM), lambda bi, i: (0, 0)),
        ],
        out_specs=tok(D_MODEL),
        out_shape=jax.ShapeDtypeStruct((nb, s, D_MODEL), BF16),
        scratch_shapes=[pltpu.VMEM((hk + tile, KD_DIM), BF16),
                        pltpu.VMEM((hk + tile, KD_DIM), BF16)],
        compiler_params=_params(("parallel", "parallel")),
        name="mixer",
    )(sinks, q, kd, k_hist, vd, v_hist, u, u_hist, bias, wpool, pscale)


def _to_row_tiles(ref, x):
    t = x.shape[0]
    for s in range(ROW_TILE):
        ref[pl.ds(s, t, stride=ROW_TILE), :] = x[:, s * LANES:(s + 1) * LANES]


def _from_row_tiles(ref, t):
    return jnp.concatenate([ref[pl.ds(s, t, stride=ROW_TILE), :] for s in range(ROW_TILE)], axis=1)


_HIGH_HALF = 0xFFFF0000


def _to_packed_rows(ref, x):
    t = x.shape[0]
    half = D_MODEL // 2
    as_bits = lambda v: pltpu.bitcast(v.astype(BF16).astype(F32), jnp.uint32)
    words = (as_bits(x[:, :half]) >> 16) | (as_bits(x[:, half:]) & jnp.uint32(_HIGH_HALF))
    for s in range(PACKED_ROWS):
        ref[pl.ds(s, t, stride=PACKED_ROWS), :] = words[:, s * LANES:(s + 1) * LANES]


def _from_packed_rows(ref, t):
    words = [ref[pl.ds(s, t, stride=PACKED_ROWS), :] for s in range(PACKED_ROWS)]
    low = [pltpu.bitcast(w << 16, F32) for w in words]
    high = [pltpu.bitcast(w & jnp.uint32(_HIGH_HALF), F32) for w in words]
    return jnp.concatenate(low + high, axis=1).astype(BF16)


def _post_kernel(mix_ref, x_ref, mod_ref, wout_ref, bout_ref, g1_ref, b1_ref, wr_ref, br_ref,
                 cin_ref, x1_ref, h2t_ref, rl_ref, rg_ref, meta_ref, cnt_ref):
    first = (pl.program_id(0) == 0) & (pl.program_id(1) == 0)

    @pl.when(first)
    def _():
        cnt_ref[...] = cin_ref[...]

    t = x_ref.shape[0]
    mix = jnp.dot(mix_ref[...], wout_ref[...], preferred_element_type=F32) + bout_ref[...]
    z = DEEPNORM_ALPHA * x_ref[...] + mod_ref[2:3, :] * mix
    x1 = _layer_norm(z) * g1_ref[...] + b1_ref[...]
    x1_ref[...] = x1
    h2 = _layer_norm(x1) * (1.0 + mod_ref[4:5, :]) + mod_ref[3:4, :]
    _to_packed_rows(h2t_ref, h2)

    h2_hi = h2.astype(BF16)
    h2_lo = (h2 - h2_hi.astype(F32)).astype(BF16)
    contract_last = (((1,), (1,)), ((), ()))
    d_hi = lax.dot_general(wr_ref[...], h2_hi, contract_last, preferred_element_type=F32)
    d_lo = lax.dot_general(wr_ref[0:N_EXPERTS, :], h2_lo, contract_last,
                           preferred_element_type=F32)
    logits = d_hi[:N_EXPERTS] + d_hi[N_EXPERTS:] + d_lo + br_ref[...]
    expert = lax.broadcasted_iota(jnp.int32, (N_EXPERTS, t), 0).astype(F32)
    vals, hots = [], []
    cur = logits
    for _ in range(TOP_K):
        mk = jnp.max(cur, axis=0, keepdims=True)
        ik = jnp.min(jnp.where(cur == mk, expert, float(N_EXPERTS)), axis=0, keepdims=True)
        hot = expert == ik
        cur = jnp.where(hot, -jnp.inf, cur)
        vals.append(mk)
        hots.append(hot)
    exps = [jnp.exp(v - vals[0]) for v in vals]
    den = exps[0] + exps[1] + exps[2] + exps[3]
    gates = [e / den for e in exps]

    onehot = jnp.zeros((N_EXPERTS, t), F32)
    for hot in hots:
        onehot = onehot + hot.astype(F32)
    onehot_bf = onehot.astype(BF16)
    tok_r = lax.broadcasted_iota(jnp.int32, (t, t), 0)
    tok_c = lax.broadcasted_iota(jnp.int32, (t, t), 1)
    earlier = jnp.where(tok_r < tok_c, 1.0, 0.0).astype(BF16)
    rank = jnp.dot(onehot_bf, earlier, preferred_element_type=F32)
    exp_r = lax.broadcasted_iota(jnp.int32, (N_EXPERTS, N_EXPERTS), 0)
    exp_c = lax.broadcasted_iota(jnp.int32, (N_EXPERTS, N_EXPERTS), 1)
    lower = jnp.where(exp_c < exp_r, 1.0, 0.0).astype(BF16)
    tile_off = jnp.sum(jnp.dot(lower, onehot_bf, preferred_element_type=F32),
                       axis=1, keepdims=True)
    tile_cnt = jnp.sum(onehot, axis=1, keepdims=True)
    local = rank + tile_off
    slots = [jnp.sum(jnp.where(hot, local, 0.0), axis=0, keepdims=True) for hot in hots]
    rl_ref[...] = jnp.concatenate([v * float(ROW_TILE) for v in slots] +
                                  [v * float(PACKED_ROWS) for v in slots],
                                  axis=0).astype(jnp.int32)
    rg_ref[...] = jnp.concatenate(gates + [jnp.zeros_like(g) for g in gates], axis=0)
    lane = lax.broadcasted_iota(jnp.int32, (N_EXPERTS, LANES), 1)
    before = cnt_ref[...]
    meta = jnp.where(lane == 0, tile_cnt, jnp.where(lane == 1, tile_off,
                     jnp.where(lane == 2, before, 0.0)))
    meta_ref[...] = meta.astype(jnp.int32)
    cnt_ref[...] = before + tile_cnt


def _post(mix, x, mod, wout, bout, g1, b1, wr, br, counts_in, tile, mod_row0):
    nb, s, _ = x.shape
    nt = s // tile

    def tok(c):
        return pl.BlockSpec((None, tile, c), lambda bi, i: (bi, i, 0))

    def whole(shape):
        return pl.BlockSpec(shape, lambda bi, i: tuple(0 for _ in shape))

    return pl.pallas_call(
        _post_kernel,
        grid=(nb, nt),
        in_specs=[
            tok(D_MODEL), tok(D_MODEL),
            pl.BlockSpec((None, 6, D_MODEL), lambda bi, i: (bi + mod_row0, 0, 0)),
            whole((D_MODEL, D_MODEL)), whole((1, D_MODEL)),
            whole((1, D_MODEL)), whole((1, D_MODEL)),
            whole((2 * N_EXPERTS, D_MODEL)), whole((N_EXPERTS, 1)),
            whole((N_EXPERTS, LANES)),
        ],
        out_specs=[tok(D_MODEL),
                   pl.BlockSpec((None, tile * PACKED_ROWS, LANES), lambda bi, i: (bi, i, 0)),
                   pl.BlockSpec((None, None, 2 * TOP_K, tile), lambda bi, i: (bi, i, 0, 0)),
                   pl.BlockSpec((None, None, 2 * TOP_K, tile), lambda bi, i: (bi, i, 0, 0)),
                   pl.BlockSpec((None, None, N_EXPERTS, LANES), lambda bi, i: (bi, i, 0, 0)),
                   whole((N_EXPERTS, LANES))],
        out_shape=[
            jax.ShapeDtypeStruct((nb, s, D_MODEL), F32),
            jax.ShapeDtypeStruct((nb, s * PACKED_ROWS, LANES), jnp.uint32),
            jax.ShapeDtypeStruct((nb, nt, 2 * TOP_K, tile), jnp.int32),
            jax.ShapeDtypeStruct((nb, nt, 2 * TOP_K, tile), F32),
            jax.ShapeDtypeStruct((nb, nt, N_EXPERTS, LANES), jnp.int32),
            jax.ShapeDtypeStruct((N_EXPERTS, LANES), F32),
        ],
        compiler_params=_params(("arbitrary", "arbitrary")),
        name="post",
    )(mix, x, mod, wout, bout, g1, b1, wr, br, counts_in)


def _tile_rows(row, n=1, per=ROW_TILE):
    return pl.ds(pl.multiple_of(row * per, per), n * per)


def _rows_at(first_row, per=ROW_TILE):
    return pl.ds(pl.multiple_of(first_row, per), per)


def _copy_plan(cnt, off, dst, n_bits):
    n_tiles = cnt.shape[0]
    bits = jnp.arange(PLAN_BITS, dtype=jnp.int32)[None, :, None]
    c = cnt[:, None, :]
    valid = ((c >> bits) & 1) * (bits < n_bits)
    done = c & ((1 << bits) - 1)
    pos = jnp.cumsum(valid, axis=-1) - valid
    place = (pos[..., :, None] == jnp.arange(N_EXPERTS, dtype=jnp.int32)) & (valid[..., :, None] == 1)
    pack = lambda v: jnp.sum(jnp.where(place, v[..., :, None], 0), axis=-2)
    local = pack(off[:, None, :] + done)
    glob = pack(dst[:, None, :] + done)
    n = jnp.sum(valid, axis=-1)
    used = 2 * PLAN_BITS * N_EXPERTS + PLAN_BITS
    plan = jnp.concatenate([local.reshape(n_tiles, -1), glob.reshape(n_tiles, -1), n,
                            jnp.zeros((n_tiles, PLAN_WORDS - used), jnp.int32)], axis=1)
    return plan.reshape(-1).astype(jnp.int32)


def _start_run_copies(plan_ref, t, make_copy):
    for b in range(t.bit_length()):
        def piece(r, carry, b=b):
            j = b * N_EXPERTS + r
            make_copy(plan_ref[j], plan_ref[PLAN_BITS * N_EXPERTS + j], 1 << b).start()
            return carry

        lax.fori_loop(0, plan_ref[2 * PLAN_BITS * N_EXPERTS + b], piece, 0)


def _dispatch_kernel(pe_ref, plan_ref, ls_ref, h_ref, *rest, clear_blocks, ls_blocked):
    xs_ref, stage, zbuf, sems, sem = rest[-5:]
    i = pl.program_id(0)
    t = h_ref.shape[0] // PACKED_ROWS
    rows = functools.partial(_tile_rows, per=PACKED_ROWS)

    if clear_blocks:
        @pl.when(i == 0)
        def _():
            zbuf[...] = jnp.zeros_like(zbuf)

            def clear_copy(row):
                return pltpu.make_async_copy(zbuf, xs_ref.at[rows(row, MOE_ROWS), :], sem)

            def clear_segment_end(e, n):
                end = pe_ref[e]
                start = jnp.where(e == 0, 0, pe_ref[jnp.maximum(e - 1, 0)])
                for back in range(1, clear_blocks + 1):
                    row = end - back * MOE_ROWS

                    @pl.when(row >= start)
                    def _():
                        clear_copy(row).start()
                    n = n + (row >= start).astype(jnp.int32)
                return n

            n_started = lax.fori_loop(0, N_EXPERTS, clear_segment_end, 0)
            n_blocks = xs_ref.shape[0] // (MOE_ROWS * PACKED_ROWS)
            n_valid = pe_ref[N_EXPERTS - 1] // MOE_ROWS

            def clear_past_end(b, carry):
                clear_copy(b * MOE_ROWS).start()
                return carry

            lax.fori_loop(n_valid, n_blocks, clear_past_end, 0)

            def wait_one(_, carry):
                clear_copy(0).wait()
                return carry

            lax.fori_loop(0, n_started + n_blocks - n_valid, wait_one, 0)

    base = 0 if ls_blocked else i * (t * TOP_K)
    slot = i % 2

    def permute_and_send(s):
        def scatter(g, carry):
            for u in range(TOKEN_UNROLL):
                tok = g * TOKEN_UNROLL + u
                token_row = h_ref[rows(tok), :]
                for k in range(TOP_K):
                    stage[s, _rows_at(ls_ref[base + k * t + tok], PACKED_ROWS), :] = token_row
            return carry

        lax.fori_loop(0, t // TOKEN_UNROLL, scatter, 0)
        _start_run_copies(
            plan_ref, t,
            lambda loc, glob, n: pltpu.make_async_copy(stage.at[s, rows(loc, n), :],
                                                       xs_ref.at[rows(glob, n), :],
                                                       sems.at[s]))

    for s in range(2):
        pl.when(slot == s)(functools.partial(permute_and_send, s))

    def wait_stage(s):
        pltpu.make_async_copy(stage.at[s], xs_ref.at[rows(0, t * TOP_K), :], sems.at[s]).wait()

    @pl.when(i > 0)
    def _():
        wait_stage(1 - slot)

    @pl.when(i == pl.num_programs(0) - 1)
    def _():
        wait_stage(slot)


def _dispatch(plan, pad_ends, ls, h2t, xs, n_rows, tile, later_rows=0):
    n_tiles = h2t.shape[0] // (tile * PACKED_ROWS)
    first = xs is None
    clear_blocks = -(-(later_rows + MOE_ROWS - 1) // MOE_ROWS) if first else 0
    ls_blocked = (tile * TOP_K) % 1024 == 0
    ls_spec = (pl.BlockSpec((tile * TOP_K,), lambda i, *_: (i,), memory_space=pltpu.SMEM)
               if ls_blocked else pl.BlockSpec(memory_space=pltpu.SMEM))
    in_specs = [pl.BlockSpec((PLAN_WORDS,), lambda i, *_: (i,), memory_space=pltpu.SMEM),
                ls_spec, pl.BlockSpec((tile * PACKED_ROWS, LANES), lambda i, *_: (i, 0))]
    args = [pad_ends, plan, ls, h2t]
    if not first:
        in_specs.append(pl.BlockSpec(memory_space=pl.ANY))
        args.append(xs)
    grid_spec = pltpu.PrefetchScalarGridSpec(
        num_scalar_prefetch=1,
        grid=(n_tiles,),
        in_specs=in_specs,
        out_specs=pl.BlockSpec(memory_space=pl.ANY),
        scratch_shapes=[pltpu.VMEM((2, tile * TOP_K * PACKED_ROWS, LANES), jnp.uint32),
                        pltpu.VMEM((MOE_ROWS * PACKED_ROWS, LANES), jnp.uint32),
                        pltpu.SemaphoreType.DMA((2,)),
                        pltpu.SemaphoreType.DMA],
    )
    return pl.pallas_call(
        functools.partial(_dispatch_kernel, clear_blocks=clear_blocks, ls_blocked=ls_blocked),
        grid_spec=grid_spec,
        out_shape=jax.ShapeDtypeStruct((n_rows * PACKED_ROWS, LANES), jnp.uint32),
        input_output_aliases={} if first else {4: 0},
        compiler_params=_params(("arbitrary",)),
        name="dispatch",
    )(*args)


def _moe_kernel(be_ref, nv_ref, nx_ref, par_ref, xs_ref, bgu_ref, bdn_ref, wgu_hbm, wdn_hbm,
                ys_ref, wgu_f32, wdn_f32, wgu_bf, wdn_bf, sems, *, layer):
    i = pl.program_id(0)
    e = be_ref[i]
    prev = be_ref[jnp.maximum(i - 1, 0)]
    valid = i < nv_ref[0]
    rows = 128

    def weight_copies(expert, slot):
        return (pltpu.make_async_copy(wgu_hbm.at[layer, expert], wgu_f32.at[slot], sems.at[slot]),
                pltpu.make_async_copy(wdn_hbm.at[layer, expert], wdn_f32.at[slot], sems.at[slot]))

    @pl.when(valid & ((i == 0) | (e != prev)))
    def _():
        slot = par_ref[i]

        @pl.when(i == 0)
        def _():
            for cp in weight_copies(e, slot):
                cp.start()

        for cp in weight_copies(e, slot):
            cp.wait()
        nxt = nx_ref[i]

        @pl.when(nxt < N_EXPERTS)
        def _():
            for cp in weight_copies(nxt, 1 - slot):
                cp.start()

        def cast(r, carry):
            sl = pl.ds(pl.multiple_of(r * rows, rows), rows)
            wgu_bf[sl, :] = wgu_f32[slot, sl, :].astype(BF16)
            wdn_bf[sl, :] = wdn_f32[slot, sl, :].astype(BF16)
            return carry
        lax.fori_loop(0, D_MODEL // rows, cast, 0)

    @pl.when(valid)
    def _():
        x = _from_packed_rows(xs_ref, MOE_ROWS)
        gu = jnp.dot(x, wgu_bf[...], preferred_element_type=F32) + bgu_ref[...]
        gate = jnp.minimum(gu[:, :D_FF], SWIGLU_LIMIT)
        up = jnp.clip(gu[:, D_FF:], -SWIGLU_LIMIT, SWIGLU_LIMIT)
        a = gate * jax.nn.sigmoid(SWIGLU_ALPHA * gate) * (up + 1.0)
        y = jnp.dot(a.astype(BF16), wdn_bf[...], preferred_element_type=F32) + bdn_ref[...]
        _to_row_tiles(ys_ref, y)

    @pl.when(jnp.logical_not(valid))
    def _():
        ys_ref[...] = jnp.zeros_like(ys_ref)


def _moe(block_e, n_valid, next_e, parity, xs, w_gu, b_gu, w_down, b_down, layer):
    n_rows = xs.shape[0] // PACKED_ROWS
    n_blocks = n_rows // MOE_ROWS
    grid_spec = pltpu.PrefetchScalarGridSpec(
        num_scalar_prefetch=4,
        grid=(n_blocks,),
        in_specs=[
            pl.BlockSpec((MOE_ROWS * PACKED_ROWS, LANES),
                         lambda i, be, nv, *_: (jnp.minimum(i, nv[0] - 1), 0)),
            pl.BlockSpec((None, None, 1, 2 * D_FF), lambda i, be, *_: (layer, be[i], 0, 0)),
            pl.BlockSpec((None, None, 1, D_MODEL), lambda i, be, *_: (layer, be[i], 0, 0)),
            pl.BlockSpec(memory_space=pl.ANY),
            pl.BlockSpec(memory_space=pl.ANY),
        ],
        out_specs=pl.BlockSpec((MOE_ROWS * ROW_TILE, LANES), lambda i, *_: (i, 0)),
        scratch_shapes=[pltpu.VMEM((2, D_MODEL, 2 * D_FF), F32),
                        pltpu.VMEM((2, D_FF, D_MODEL), F32),
                        pltpu.VMEM((D_MODEL, 2 * D_FF), BF16),
                        pltpu.VMEM((D_FF, D_MODEL), BF16),
                        pltpu.SemaphoreType.DMA((2,))],
    )
    return pl.pallas_call(
        functools.partial(_moe_kernel, layer=layer),
        grid_spec=grid_spec,
        out_shape=jax.ShapeDtypeStruct((n_rows * ROW_TILE, LANES), F32),
        compiler_params=_params(("arbitrary",)),
        name="moe",
    )(block_e, n_valid, next_e, parity, xs, b_gu.reshape(DEPTH, N_EXPERTS, 1, 2 * D_FF),
      b_down.reshape(DEPTH, N_EXPERTS, 1, D_MODEL), w_gu, w_down)


def _combine_kernel(plan_ref, next_plan_ref, ls_ref, gate_ref, x1_ref, gf_ref, g2_ref, b2_ref,
                    ys_ref, o_ref, stage, ft, sems, *, ls_blocked):
    i = pl.program_id(0)
    t = x1_ref.shape[0]
    slot = i % 2

    def start_fetch(tile_plan_ref, s):
        _start_run_copies(
            tile_plan_ref, t,
            lambda loc, glob, n: pltpu.make_async_copy(ys_ref.at[_tile_rows(glob, n), :],
                                                       stage.at[s, _tile_rows(loc, n), :],
                                                       sems.at[s]))

    @pl.when(i == 0)
    def _():
        start_fetch(plan_ref, 0)

    for s in range(2):
        @pl.when((i + 1 < pl.num_programs(0)) & (slot == 1 - s))
        def _(s=s):
            start_fetch(next_plan_ref, s)

    pltpu.make_async_copy(ys_ref.at[_tile_rows(0, t * TOP_K), :], stage.at[slot],
                          sems.at[slot]).wait()

    base = 0 if ls_blocked else i * (t * TOP_K)

    def gate_sum(s):
        def gather(g, carry):
            for u in range(TOKEN_UNROLL):
                tok = g * TOKEN_UNROLL + u
                acc = None
                for k in range(TOP_K):
                    j = base + k * t + tok
                    v = stage[s, _rows_at(ls_ref[j]), :] * gate_ref[j]
                    acc = v if acc is None else acc + v
                ft[_tile_rows(tok), :] = acc
            return carry

        lax.fori_loop(0, t // TOKEN_UNROLL, gather, 0)

    for s in range(2):
        pl.when(slot == s)(functools.partial(gate_sum, s))
    f = _from_row_tiles(ft, t)
    z = DEEPNORM_ALPHA * x1_ref[...] + gf_ref[...] * f
    o_ref[...] = _layer_norm(z) * g2_ref[...] + b2_ref[...]


def _combine(plan, ls, gates, x1, gf, g2, b2, ys, tile, tiles_per_seq):
    n = x1.shape[0]
    n_tiles = n // tile
    ls_blocked = (tile * TOP_K) % 1024 == 0

    def smem_vec():
        if ls_blocked:
            return pl.BlockSpec((tile * TOP_K,), lambda i, *_: (i,), memory_space=pltpu.SMEM)
        return pl.BlockSpec(memory_space=pltpu.SMEM)

    grid_spec = pltpu.PrefetchScalarGridSpec(
        num_scalar_prefetch=0,
        grid=(n_tiles,),
        in_specs=[
            pl.BlockSpec((PLAN_WORDS,), lambda i: (i,), memory_space=pltpu.SMEM),
            pl.BlockSpec((PLAN_WORDS,), lambda i: (jnp.minimum(i + 1, n_tiles - 1),),
                         memory_space=pltpu.SMEM),
            smem_vec(), smem_vec(),
            pl.BlockSpec((tile, D_MODEL), lambda i, *_: (i, 0)),
            pl.BlockSpec((None, 1, D_MODEL), lambda i, *_: (i // tiles_per_seq, 0, 0)),
            pl.BlockSpec((1, D_MODEL), lambda i, *_: (0, 0)),
            pl.BlockSpec((1, D_MODEL), lambda i, *_: (0, 0)),
            pl.BlockSpec(memory_space=pl.ANY),
        ],
        out_specs=pl.BlockSpec((tile, D_MODEL), lambda i, *_: (i, 0)),
        scratch_shapes=[pltpu.VMEM((2, tile * TOP_K * ROW_TILE, LANES), F32),
                        pltpu.VMEM((tile * ROW_TILE, LANES), F32),
                        pltpu.SemaphoreType.DMA((2,))],
    )
    return pl.pallas_call(
        functools.partial(_combine_kernel, ls_blocked=ls_blocked),
        grid_spec=grid_spec,
        out_shape=jax.ShapeDtypeStruct((n, D_MODEL), F32),
        compiler_params=_params(("arbitrary",)),
        name="combine",
    )(plan, plan, ls, gates, x1, gf, g2, b2, ys)


def _alibi_slopes():
    return jnp.exp2(-ALIBI_MAX * (jnp.arange(N_HEADS, dtype=F32) + 1.0) / N_HEADS)


def _prompt_bias(slopes):
    r = jnp.arange(Q_TILE)
    j = jnp.arange(WINDOW + Q_TILE)
    dc = (WINDOW + r)[:, None] // CHUNK - j[None, :] // CHUNK
    vis = (dc >= 0) & (dc <= WIN_CHUNKS)
    dist = jnp.abs(r[:, None] + WINDOW - j[None, :]).astype(F32)
    bias = jnp.where(vis[None], -slopes[:, None, None] * dist[None] * LOG2_E, NEG_INF)
    return jnp.stack([bias, jnp.where(j[None, None, :] < WINDOW, NEG_INF, bias)])


def _sample_bias(slopes, t, hist):
    qpos = PAST_LEN + jnp.arange(t)
    kpos = PAST_LEN - hist + jnp.arange(hist + t)
    dc = qpos[:, None] // CHUNK - kpos[None, :] // CHUNK
    vis = (dc >= 0) & (dc <= WIN_CHUNKS) & (kpos[None, :] >= 0)
    dist = jnp.abs(qpos[:, None] - kpos[None, :]).astype(F32)
    return jnp.where(vis[None], -slopes[:, None, None] * dist[None] * LOG2_E, NEG_INF)[None]


def _dup_heads(w):
    lead = w.shape[:-1]
    w4 = w.reshape(lead + (N_KV, HEAD_DIM))
    return jnp.concatenate([w4, w4], axis=-1).reshape(lead + (KD_DIM,))


def _proj_weights(w_in, b_in):
    q0, k0, v0, u0 = 0, ATT_DIM, ATT_DIM + KV_DIM, ATT_DIM + 2 * KV_DIM
    q_scale = ATTN_SCALE * LOG2_E
    w = jnp.concatenate([w_in[:, q0:k0] * q_scale, _dup_heads(w_in[:, k0:v0]),
                         _dup_heads(w_in[:, v0:u0]), w_in[:, u0:]], axis=1)
    b = jnp.concatenate([b_in[q0:k0] * q_scale, _dup_heads(b_in[k0:v0]),
                         _dup_heads(b_in[v0:u0]), b_in[u0:]], axis=0)
    return w.astype(BF16), b.reshape(1, PROJ_COLS)


def _pool_weight(w_pool):
    n = len(POOL_WINDOWS)
    eye = jnp.eye(n, dtype=w_pool.dtype)
    return jnp.einsum('gcd,gh->gchd', w_pool, eye).reshape(POOL_DIM, POOL_DIM).astype(BF16)


def _undup(kd, nb, rows):
    return kd[:, -rows:].reshape(nb, rows, N_KV, LANES)[..., :HEAD_DIM].astype(F32)


def kernel(x_prompt, x_sample, c_prompt, c_sample, cache_k, cache_v, state_pool, w_ada, b_ada,
           w_in, b_in, sinks, w_pool, pool_scale, w_out, b_out, ln1_g, ln1_b, ln2_g, ln2_b,
           w_router, b_router, w_gu, b_gu, w_down, b_down):
    nbp, seq, _ = x_prompt.shape
    nbs, tdec, _ = x_sample.shape
    hist = cache_k.shape[2]
    n_p, n_s = nbp * seq, nbs * tdec
    n_asg = (n_p + n_s) * TOP_K
    n_blocks = -(-(n_asg + N_EXPERTS * (MOE_ROWS - 1)) // MOE_ROWS)
    n_rows = n_blocks * MOE_ROWS
    assert hist == WINDOW and tdec >= POOL_PAD and seq % TOK_TILE == 0

    nb_all = nbp + nbs
    nb_pad = -(-nb_all // SUBLANES) * SUBLANES
    c_all = jnp.concatenate([c_prompt, c_sample, jnp.zeros((nb_pad - nb_all, D_MODEL), F32)], 0)
    mod_all = _adaln(c_all, w_ada, b_ada).reshape(DEPTH, nb_pad, 6, D_MODEL)

    slopes = _alibi_slopes()
    bias_p = _prompt_bias(slopes)
    bias_s = _sample_bias(slopes, tdec, hist)
    row = lambda v: v.reshape(1, -1)

    yp, ys_tok = x_prompt, x_sample
    outs = {k: [] for k in ("kp", "vp", "pp", "ks", "vs", "ps")}
    for l in range(DEPTH):
        mod = mod_all[l]
        w_proj, b_proj = _proj_weights(w_in[l], b_in[l])
        wpool = _pool_weight(w_pool[l])
        pscale = row(pool_scale[l])
        wout = w_out[l].astype(BF16)
        wr_t = w_router[l].T
        wr_hi = wr_t.astype(BF16)
        wr = jnp.concatenate([wr_hi, (wr_t - wr_hi.astype(F32)).astype(BF16)], axis=0)
        br = b_router[l].reshape(N_EXPERTS, 1)

        q, kd, vd, u = _inproj(yp, mod, w_proj, b_proj, TOK_TILE, 0)
        sinks2 = sinks[l] * LOG2_E
        mix_p = _mixer(sinks2, q, kd, vd, u, None, None, None, bias_p, wpool, pscale,
                       tile=TOK_TILE, tq=Q_TILE, history_starts_empty=True, pos0=0)
        outs["kp"].append(_undup(kd, nbp, WINDOW))
        outs["vp"].append(_undup(vd, nbp, WINDOW))
        outs["pp"].append(u[:, -POOL_PAD:])

        qs, kds, vds, us = _inproj(ys_tok, mod, w_proj, b_proj, tdec, nbp)
        ck = _dup_heads(cache_k[l].reshape(nbs, hist, KV_DIM)).astype(BF16)
        cv = _dup_heads(cache_v[l].reshape(nbs, hist, KV_DIM)).astype(BF16)
        sp = jnp.pad(state_pool[l], ((0, 0), (POOL_HALO - POOL_PAD, 0), (0, 0)))
        mix_s = _mixer(sinks2, qs, kds, vds, us, ck, cv, sp, bias_s, wpool, pscale,
                       tile=tdec, tq=tdec, history_starts_empty=False, pos0=PAST_LEN)
        k_new = _undup(kds, nbs, tdec)
        v_new = _undup(vds, nbs, tdec)
        outs["ks"].append(jnp.concatenate([cache_k[l], k_new], axis=1)[:, -hist:])
        outs["vs"].append(jnp.concatenate([cache_v[l], v_new], axis=1)[:, -hist:])
        outs["ps"].append(us[:, -POOL_PAD:])

        post_w = (wout, row(b_out[l]), row(ln1_g[l]), row(ln1_b[l]), wr, br)
        zero_counts = jnp.zeros((N_EXPERTS, LANES), F32)
        x1p, h2tp, rlp, rgp, meta_p, cnt_p = _post(mix_p, yp, mod, *post_w, zero_counts, TOK_TILE, 0)
        x1s, h2ts, rls, rgs, meta_s, cnt = _post(mix_s, ys_tok, mod, *post_w, cnt_p, tdec, nbp)

        counts = cnt[:, 0].astype(jnp.int32)
        padded = (counts + MOE_ROWS - 1) // MOE_ROWS * MOE_ROWS
        pad_ends = jnp.cumsum(padded).astype(jnp.int32)
        pad_starts = pad_ends - padded
        block_start = jnp.arange(n_blocks, dtype=jnp.int32) * MOE_ROWS
        block_e = jnp.minimum(jnp.sum(pad_ends[None, :] <= block_start[:, None], axis=1),
                              N_EXPERTS - 1).astype(jnp.int32)
        n_valid = (pad_ends[-1:] // MOE_ROWS).astype(jnp.int32)
        ids = jnp.arange(N_EXPERTS, dtype=jnp.int32)
        present = padded > 0
        later = (ids[None, :] > ids[:, None]) & present[None, :]
        next_present = jnp.min(jnp.where(later, ids[None, :], N_EXPERTS), axis=1)
        ordinal = jnp.cumsum(present.astype(jnp.int32)) - 1
        is_block_e = block_e[:, None] == ids[None, :]
        next_e = jnp.sum(jnp.where(is_block_e, next_present[None, :], 0), axis=1).astype(jnp.int32)
        parity = jnp.sum(jnp.where(is_block_e, ordinal[None, :] % 2, 0), axis=1).astype(jnp.int32)

        def copy_plan(meta, tile):
            m = meta.reshape(-1, N_EXPERTS, LANES)
            return _copy_plan(m[:, :, 0], m[:, :, 1], pad_starts[None, :] + m[:, :, 2],
                              tile.bit_length())

        def per_assignment(r, n, first=0):
            return r[:, :, first:first + TOP_K, :].reshape(n * TOP_K)

        plan_p, plan_s = copy_plan(meta_p, TOK_TILE), copy_plan(meta_s, tdec)
        ls_p, ls_s = per_assignment(rlp, n_p), per_assignment(rls, n_s)

        xs = _dispatch(plan_p, pad_ends, per_assignment(rlp, n_p, TOP_K),
                       h2tp.reshape(n_p * PACKED_ROWS, LANES), None, n_rows, TOK_TILE,
                       later_rows=n_s)
        xs = _dispatch(plan_s, pad_ends, per_assignment(rls, n_s, TOP_K),
                       h2ts.reshape(n_s * PACKED_ROWS, LANES), xs, n_rows, tdec)
        ye = _moe(block_e, n_valid, next_e, parity, xs, w_gu, b_gu, w_down, b_down, l)

        g2, b2 = row(ln2_g[l]), row(ln2_b[l])
        gf = mod[:, 5].reshape(nb_pad, 1, D_MODEL)
        yp = _combine(plan_p, ls_p, per_assignment(rgp, n_p), x1p.reshape(n_p, D_MODEL),
                      gf[:nbp], g2, b2, ye, TOK_TILE, seq // TOK_TILE).reshape(nbp, seq, D_MODEL)
        ys_tok = _combine(plan_s, ls_s, per_assignment(rgs, n_s), x1s.reshape(n_s, D_MODEL),
                          gf[nbp:nb_all], g2, b2, ye, tdec, 1).reshape(nbs, tdec, D_MODEL)

    st = lambda k: jnp.stack(outs[k])
    return (yp, ys_tok, st("kp"), st("vp"), st("pp"), st("ks"), st("vs"), st("ps"))
```

```python
import functools

import jax
import jax.numpy as jnp
from jax import lax
from jax.experimental import pallas as pl
from jax.experimental.pallas import tpu as pltpu

F32 = jnp.float32
BF16 = jnp.bfloat16

D_MODEL = 1024
DEPTH = 2
CHUNK = 64
WINDOW = 128
WIN_CHUNKS = WINDOW // CHUNK
HEAD_DIM = 64
ATT_DIM = 768
N_HEADS = 12
N_KV = 4
GROUP = N_HEADS // N_KV
KV_DIM = N_KV * HEAD_DIM
ATTN_SCALE = HEAD_DIM ** -0.5
LOG2_E = 1.4426950408889634
ALIBI_MAX = 8.0
NEG_INF = -1e30
POOL_DIM = D_MODEL - ATT_DIM
POOL_WINDOWS = (2, 4, 8, 16)
POOL_CH = POOL_DIM // len(POOL_WINDOWS)
POOL_PAD = max(POOL_WINDOWS) - 1
N_EXPERTS = 32
TOP_K = 4
D_FF = D_MODEL
SWIGLU_LIMIT = 7.0
SWIGLU_ALPHA = 1.702
LN_EPS = 1e-5
DEEPNORM_ALPHA = (2.0 * DEPTH) ** 0.25
PAST_LEN = 1024

LANES = 128
SUBLANES = 8
VMEM_LIMIT = 52 * 1024 * 1024

TOK_TILE = 512
Q_TILE = 128
POOL_HALO = 16
KD_DIM = N_KV * LANES
PROJ_COLS = ATT_DIM + 2 * KV_DIM + POOL_DIM
INPROJ_TILE = 1024
MOE_ROWS = 512
TOKEN_UNROLL = 8
PLAN_BITS = TOK_TILE.bit_length()
PLAN_WORDS = 1024
assert 2 * PLAN_BITS * N_EXPERTS + PLAN_BITS <= PLAN_WORDS
ROW_TILE = D_MODEL // LANES
assert ROW_TILE == SUBLANES
PACKED_ROWS = ROW_TILE // 2


def _params(sem, vmem=VMEM_LIMIT):
    return pltpu.CompilerParams(dimension_semantics=sem, vmem_limit_bytes=vmem)


def _layer_norm(x):
    mu = jnp.mean(x, axis=-1, keepdims=True)
    xc = x - mu
    var = jnp.mean(xc * xc, axis=-1, keepdims=True)
    return xc * lax.rsqrt(var + LN_EPS)


def _adaln_kernel(c_ref, w_ref, b_ref, o_ref):
    c = c_ref[...]
    s = c * jax.nn.sigmoid(c)
    o_ref[...] = jnp.dot(s, w_ref[...], preferred_element_type=F32,
                         precision=lax.Precision.HIGHEST) + b_ref[...]


def _adaln(c_all, w_ada, b_ada):
    nb = c_all.shape[0]
    ncol = 6 * D_MODEL
    tn = D_MODEL
    return pl.pallas_call(
        _adaln_kernel,
        grid=(DEPTH, ncol // tn),
        in_specs=[
            pl.BlockSpec((nb, D_MODEL), lambda l, j: (0, 0)),
            pl.BlockSpec((None, D_MODEL, tn), lambda l, j: (l, 0, j)),
            pl.BlockSpec((None, 1, tn), lambda l, j: (l, 0, j)),
        ],
        out_specs=pl.BlockSpec((None, nb, tn), lambda l, j: (l, 0, j)),
        out_shape=jax.ShapeDtypeStruct((DEPTH, nb, ncol), F32),
        compiler_params=_params(("parallel", "parallel")),
        name="adaln",
    )(c_all, w_ada, b_ada.reshape(DEPTH, 1, ncol))


def _inproj_kernel(x_ref, mod_ref, w_ref, b_ref, q_ref, kv_ref, u_ref):
    h = _layer_norm(x_ref[...]) * (1.0 + mod_ref[1:2, :]) + mod_ref[0:1, :]
    p = jnp.dot(h.astype(BF16), w_ref[...], preferred_element_type=F32) + b_ref[...]
    q_ref[...] = p[:, :ATT_DIM].astype(BF16)
    kv_ref[...] = p[:, ATT_DIM:ATT_DIM + 2 * KV_DIM].astype(BF16)
    u_ref[...] = p[:, ATT_DIM + 2 * KV_DIM:]


def _inproj(x, mod, w, b, tile, mod_row0):
    nb, s, _ = x.shape
    nt = s // tile

    def tok(c):
        return pl.BlockSpec((None, tile, c), lambda bi, i: (bi, i, 0))

    return pl.pallas_call(
        _inproj_kernel,
        grid=(nb, nt),
        in_specs=[
            tok(D_MODEL),
            pl.BlockSpec((None, 6, D_MODEL), lambda bi, i: (bi + mod_row0, 0, 0)),
            pl.BlockSpec((D_MODEL, PROJ_COLS), lambda bi, i: (0, 0)),
            pl.BlockSpec((1, PROJ_COLS), lambda bi, i: (0, 0)),
        ],
        out_specs=[tok(ATT_DIM), tok(2 * KV_DIM), tok(POOL_DIM)],
        out_shape=[
            jax.ShapeDtypeStruct((nb, s, ATT_DIM), BF16),
            jax.ShapeDtypeStruct((nb, s, 2 * KV_DIM), BF16),
            jax.ShapeDtypeStruct((nb, s, POOL_DIM), F32),
        ],
        compiler_params=_params(("parallel", "parallel")),
        name="inproj",
    )(x, mod, w, b)


def _dup_heads(x):
    r = x.shape[0]
    low = lax.broadcasted_iota(jnp.int32, (r // 2, LANES), 1) < HEAD_DIM
    parts = []
    for g in range(KV_DIM // LANES):
        pair = pltpu.bitcast(x[:, g * LANES:(g + 1) * LANES], jnp.uint32)
        swapped = pltpu.roll(pair, HEAD_DIM, 1)
        parts += [jnp.where(low, pair, swapped), jnp.where(low, swapped, pair)]
    return pltpu.bitcast(jnp.concatenate(parts, axis=1), BF16)


def _mixer_kernel(sinks_ref, q_ref, kv_ref, kvh_ref, u_ref, uh_ref,
                  bias_ref, wpool_ref, pscale_ref, o_ref, kall_ref, vall_ref,
                  *, tq, history_starts_empty, pos0):
    i = pl.program_id(1)
    t = q_ref.shape[0]
    hk = kvh_ref.shape[0]
    nk = hk + tq
    @pl.when(i >= 0)
    def _():
        kall_ref[0:hk, :] = _dup_heads(kvh_ref[:, :KV_DIM])
        kall_ref[hk:, :] = _dup_heads(kv_ref[:, :KV_DIM])
        vall_ref[0:hk, :] = _dup_heads(kvh_ref[:, KV_DIM:])
        vall_ref[hk:, :] = _dup_heads(kv_ref[:, KV_DIM:])

    low_half = lax.broadcasted_iota(jnp.int32, (tq, LANES), 1) < HEAD_DIM
    for s in range(t // tq):
        rows = slice(s * tq, (s + 1) * tq)
        keys = slice(s * tq, s * tq + nk)
        table = (i == 0).astype(jnp.int32) if (history_starts_empty and s == 0) else 0
        for pair in range(N_HEADS // 2):
            q2 = q_ref[rows, pair * LANES:(pair + 1) * LANES]
            halves = []
            for half in range(2):
                head = 2 * pair + half
                kv = head // GROUP
                qm = jnp.where(low_half if half == 0 else ~low_half, q2, jnp.zeros_like(q2))
                kd = kall_ref[keys, kv * LANES:(kv + 1) * LANES]
                sc = lax.dot_general(qm, kd, (((1,), (1,)), ((), ())),
                                     preferred_element_type=F32)
                sc = sc + bias_ref[table, head]
                sink = sinks_ref[head]
                m = jnp.maximum(jnp.max(sc, axis=1, keepdims=True), sink)
                p = jnp.exp2(sc - m)
                den = jnp.sum(p, axis=1, keepdims=True) + jnp.exp2(sink - m)
                vd = vall_ref[keys, kv * LANES:(kv + 1) * LANES]
                o2 = jnp.dot(p.astype(BF16), vd, preferred_element_type=F32)
                halves.append(o2 / den)
            o_ref[rows, pair * LANES:(pair + 1) * LANES] = jnp.where(
                low_half, halves[0], halves[1]).astype(BF16)

    uh = uh_ref[...]
    if history_starts_empty:
        uh = jnp.where(i == 0, 0.0, uh)
    u = u_ref[...]
    ue = jnp.concatenate([uh, u], axis=0)
    s2 = ue + pltpu.roll(ue, 1, 0)
    s4 = s2 + pltpu.roll(s2, 2, 0)
    s8 = s4 + pltpu.roll(s4, 4, 0)
    s16 = s8 + pltpu.roll(s8, 8, 0)
    lane = lax.broadcasted_iota(jnp.int32, (1, POOL_DIM), 1)
    g0, g1, g2 = lane < POOL_CH, lane < 2 * POOL_CH, lane < 3 * POOL_CH
    wsum = jnp.where(g0, s2[POOL_HALO:], jnp.where(g1, s4[POOL_HALO:],
                     jnp.where(g2, s8[POOL_HALO:], s16[POOL_HALO:])))
    width = jnp.where(g0, 2.0, jnp.where(g1, 4.0, jnp.where(g2, 8.0, 16.0))).astype(F32)
    pos = (pos0 + i * t + lax.broadcasted_iota(jnp.int32, (t, 1), 0)).astype(F32)
    cnt = jnp.minimum(width, pos + 1.0)
    pooled = wsum / cnt - u
    mixed = jnp.dot(pooled.astype(BF16), wpool_ref[...], preferred_element_type=F32)
    o_ref[:, ATT_DIM:] = (mixed * pscale_ref[...]).astype(BF16)


def _mixer(sinks, q, kv, u, kv_hist, u_hist, bias, wpool, pscale, *,
           tile, tq, history_starts_empty, pos0):
    nb, s, _ = q.shape
    nt = s // tile
    hk = WINDOW
    own_history = kv_hist is None
    if own_history:
        kv_hist, u_hist = kv, u
        kh_map = lambda bi, i: (bi, jnp.maximum(i * (tile // hk) - 1, 0), 0)
        uh_map = lambda bi, i: (bi, jnp.maximum(i * (tile // POOL_HALO) - 1, 0), 0)
    else:
        kh_map = lambda bi, i: (bi, 0, 0)
        uh_map = lambda bi, i: (bi, 0, 0)

    def tok(c):
        return pl.BlockSpec((None, tile, c), lambda bi, i: (bi, i, 0))

    kern = functools.partial(_mixer_kernel, tq=tq,
                             history_starts_empty=history_starts_empty, pos0=pos0)
    return pl.pallas_call(
        kern,
        grid=(nb, nt),
        in_specs=[
            pl.BlockSpec(memory_space=pltpu.SMEM),
            tok(ATT_DIM),
            tok(2 * KV_DIM),
            pl.BlockSpec((None, hk, 2 * KV_DIM), kh_map),
            tok(POOL_DIM),
            pl.BlockSpec((None, POOL_HALO, POOL_DIM), uh_map),
            pl.BlockSpec(bias.shape, lambda bi, i: (0, 0, 0, 0)),
            pl.BlockSpec((POOL_DIM, POOL_DIM), lambda bi, i: (0, 0)),
            pl.BlockSpec((1, POOL_DIM), lambda bi, i: (0, 0)),
        ],
        out_specs=tok(D_MODEL),
        out_shape=jax.ShapeDtypeStruct((nb, s, D_MODEL), BF16),
        scratch_shapes=[pltpu.VMEM((hk + tile, KD_DIM), BF16),
                        pltpu.VMEM((hk + tile, KD_DIM), BF16)],
        compiler_params=_params(("parallel", "parallel")),
        name="mixer",
    )(sinks, q, kv, kv_hist, u, u_hist, bias, wpool, pscale)


def _to_row_tiles(ref, x):
    t = x.shape[0]
    for s in range(ROW_TILE):
        ref[pl.ds(s, t, stride=ROW_TILE), :] = x[:, s * LANES:(s + 1) * LANES]


def _from_row_tiles(ref, t):
    return jnp.concatenate([ref[pl.ds(s, t, stride=ROW_TILE), :] for s in range(ROW_TILE)], axis=1)


_HIGH_HALF = 0xFFFF0000


def _to_packed_rows(ref, x):
    t = x.shape[0]
    half = D_MODEL // 2
    as_bits = lambda v: pltpu.bitcast(v.astype(BF16).astype(F32), jnp.uint32)
    words = (as_bits(x[:, :half]) >> 16) | (as_bits(x[:, half:]) & jnp.uint32(_HIGH_HALF))
    for s in range(PACKED_ROWS):
        ref[pl.ds(s, t, stride=PACKED_ROWS), :] = words[:, s * LANES:(s + 1) * LANES]


def _from_packed_rows(ref, t):
    words = [ref[pl.ds(s, t, stride=PACKED_ROWS), :] for s in range(PACKED_ROWS)]
    low = [pltpu.bitcast(w << 16, F32) for w in words]
    high = [pltpu.bitcast(w & jnp.uint32(_HIGH_HALF), F32) for w in words]
    return jnp.concatenate(low + high, axis=1).astype(BF16)


def _post_kernel(mix_ref, x_ref, mod_ref, wout_ref, bout_ref, g1_ref, b1_ref, wr_ref, br_ref,
                 cin_ref, x1_ref, h2t_ref, rl_ref, rg_ref, meta_ref, cnt_ref):
    first = (pl.program_id(0) == 0) & (pl.program_id(1) == 0)

    @pl.when(first)
    def _():
        cnt_ref[...] = cin_ref[...]

    t = x_ref.shape[0]
    mix = jnp.dot(mix_ref[...], wout_ref[...], preferred_element_type=F32) + bout_ref[...]
    z = DEEPNORM_ALPHA * x_ref[...] + mod_ref[2:3, :] * mix
    x1 = _layer_norm(z) * g1_ref[...] + b1_ref[...]
    x1_ref[...] = x1
    h2 = _layer_norm(x1) * (1.0 + mod_ref[4:5, :]) + mod_ref[3:4, :]
    _to_packed_rows(h2t_ref, h2)

    h2_hi = h2.astype(BF16)
    h2_lo = (h2 - h2_hi.astype(F32)).astype(BF16)
    contract_last = (((1,), (1,)), ((), ()))
    d_hi = lax.dot_general(wr_ref[...], h2_hi, contract_last, preferred_element_type=F32)
    d_lo = lax.dot_general(wr_ref[0:N_EXPERTS, :], h2_lo, contract_last,
                           preferred_element_type=F32)
    logits = d_hi[:N_EXPERTS] + d_hi[N_EXPERTS:] + d_lo + br_ref[...]
    expert = lax.broadcasted_iota(jnp.int32, (N_EXPERTS, t), 0).astype(F32)
    vals, hots = [], []
    cur = logits
    for _ in range(TOP_K):
        mk = jnp.max(cur, axis=0, keepdims=True)
        ik = jnp.min(jnp.where(cur == mk, expert, float(N_EXPERTS)), axis=0, keepdims=True)
        hot = expert == ik
        cur = jnp.where(hot, -jnp.inf, cur)
        vals.append(mk)
        hots.append(hot)
    exps = [jnp.exp(v - vals[0]) for v in vals]
    den = exps[0] + exps[1] + exps[2] + exps[3]
    gates = [e / den for e in exps]

    onehot = jnp.zeros((N_EXPERTS, t), F32)
    for hot in hots:
        onehot = onehot + hot.astype(F32)
    onehot_bf = onehot.astype(BF16)
    tok_r = lax.broadcasted_iota(jnp.int32, (t, t), 0)
    tok_c = lax.broadcasted_iota(jnp.int32, (t, t), 1)
    earlier = jnp.where(tok_r < tok_c, 1.0, 0.0).astype(BF16)
    rank = jnp.dot(onehot_bf, earlier, preferred_element_type=F32)
    exp_r = lax.broadcasted_iota(jnp.int32, (N_EXPERTS, N_EXPERTS), 0)
    exp_c = lax.broadcasted_iota(jnp.int32, (N_EXPERTS, N_EXPERTS), 1)
    lower = jnp.where(exp_c < exp_r, 1.0, 0.0).astype(BF16)
    tile_off = jnp.sum(jnp.dot(lower, onehot_bf, preferred_element_type=F32),
                       axis=1, keepdims=True)
    tile_cnt = jnp.sum(onehot, axis=1, keepdims=True)
    local = rank + tile_off
    slots = [jnp.sum(jnp.where(hot, local, 0.0), axis=0, keepdims=True) for hot in hots]
    rl_ref[...] = jnp.concatenate([v * float(ROW_TILE) for v in slots] +
                                  [v * float(PACKED_ROWS) for v in slots],
                                  axis=0).astype(jnp.int32)
    rg_ref[...] = jnp.concatenate(gates + [jnp.zeros_like(g) for g in gates], axis=0)
    lane = lax.broadcasted_iota(jnp.int32, (N_EXPERTS, LANES), 1)
    before = cnt_ref[...]
    meta = jnp.where(lane == 0, tile_cnt, jnp.where(lane == 1, tile_off,
                     jnp.where(lane == 2, before, 0.0)))
    meta_ref[...] = meta.astype(jnp.int32)
    cnt_ref[...] = before + tile_cnt


def _post(mix, x, mod, wout, bout, g1, b1, wr, br, counts_in, tile, mod_row0):
    nb, s, _ = x.shape
    nt = s // tile

    def tok(c):
        return pl.BlockSpec((None, tile, c), lambda bi, i: (bi, i, 0))

    def whole(shape):
        return pl.BlockSpec(shape, lambda bi, i: tuple(0 for _ in shape))

    return pl.pallas_call(
        _post_kernel,
        grid=(nb, nt),
        in_specs=[
            tok(D_MODEL), tok(D_MODEL),
            pl.BlockSpec((None, 6, D_MODEL), lambda bi, i: (bi + mod_row0, 0, 0)),
            whole((D_MODEL, D_MODEL)), whole((1, D_MODEL)),
            whole((1, D_MODEL)), whole((1, D_MODEL)),
            whole((2 * N_EXPERTS, D_MODEL)), whole((N_EXPERTS, 1)),
            whole((N_EXPERTS, LANES)),
        ],
        out_specs=[tok(D_MODEL),
                   pl.BlockSpec((None, tile * PACKED_ROWS, LANES), lambda bi, i: (bi, i, 0)),
                   pl.BlockSpec((None, None, 2 * TOP_K, tile), lambda bi, i: (bi, i, 0, 0)),
                   pl.BlockSpec((None, None, 2 * TOP_K, tile), lambda bi, i: (bi, i, 0, 0)),
                   pl.BlockSpec((None, None, N_EXPERTS, LANES), lambda bi, i: (bi, i, 0, 0)),
                   whole((N_EXPERTS, LANES))],
        out_shape=[
            jax.ShapeDtypeStruct((nb, s, D_MODEL), F32),
            jax.ShapeDtypeStruct((nb, s * PACKED_ROWS, LANES), jnp.uint32),
            jax.ShapeDtypeStruct((nb, nt, 2 * TOP_K, tile), jnp.int32),
            jax.ShapeDtypeStruct((nb, nt, 2 * TOP_K, tile), F32),
            jax.ShapeDtypeStruct((nb, nt, N_EXPERTS, LANES), jnp.int32),
            jax.ShapeDtypeStruct((N_EXPERTS, LANES), F32),
        ],
        compiler_params=_params(("arbitrary", "arbitrary")),
        name="post",
    )(mix, x, mod, wout, bout, g1, b1, wr, br, counts_in)


def _tile_rows(row, n=1, per=ROW_TILE):
    return pl.ds(pl.multiple_of(row * per, per), n * per)


def _rows_at(first_row, per=ROW_TILE):
    return pl.ds(pl.multiple_of(first_row, per), per)


def _copy_plan(cnt, off, dst, n_bits):
    n_tiles = cnt.shape[0]
    bits = jnp.arange(PLAN_BITS, dtype=jnp.int32)[None, :, None]
    c = cnt[:, None, :]
    valid = ((c >> bits) & 1) * (bits < n_bits)
    done = c & ((1 << bits) - 1)
    ids = jnp.arange(N_EXPERTS, dtype=jnp.int32)
    pos = jnp.sum(jnp.where(ids[None, :] < ids[:, None], valid[..., None, :], 0), axis=-1)
    place = (pos[..., :, None] == ids) & (valid[..., :, None] == 1)
    pack = lambda v: jnp.sum(jnp.where(place, v[..., :, None], 0), axis=-2)
    local = pack(off[:, None, :] + done)
    glob = pack(dst[:, None, :] + done)
    n = jnp.sum(valid, axis=-1)
    used = 2 * PLAN_BITS * N_EXPERTS + PLAN_BITS
    plan = jnp.concatenate([local.reshape(n_tiles, -1), glob.reshape(n_tiles, -1), n,
                            jnp.zeros((n_tiles, PLAN_WORDS - used), jnp.int32)], axis=1)
    return plan.reshape(-1).astype(jnp.int32)


def _start_run_copies(plan_ref, t, make_copy):
    for b in range(t.bit_length()):
        def piece(r, carry, b=b):
            j = b * N_EXPERTS + r
            make_copy(plan_ref[j], plan_ref[PLAN_BITS * N_EXPERTS + j], 1 << b).start()
            return carry

        lax.fori_loop(0, plan_ref[2 * PLAN_BITS * N_EXPERTS + b], piece, 0)


def _dispatch_kernel(pe_ref, plan_ref, ls_ref, h_ref, *rest, clear_blocks, ls_blocked):
    xs_ref, stage, zbuf, sems, sem = rest[-5:]
    i = pl.program_id(0)
    t = h_ref.shape[0] // PACKED_ROWS
    rows = functools.partial(_tile_rows, per=PACKED_ROWS)

    if clear_blocks:
        @pl.when(i == 0)
        def _():
            zbuf[...] = jnp.zeros_like(zbuf)

            def clear_copy(row):
                return pltpu.make_async_copy(zbuf, xs_ref.at[rows(row, MOE_ROWS), :], sem)

            def clear_segment_end(e, n):
                end = pe_ref[e]
                start = jnp.where(e == 0, 0, pe_ref[jnp.maximum(e - 1, 0)])
                for back in range(1, clear_blocks + 1):
                    row = end - back * MOE_ROWS

                    @pl.when(row >= start)
                    def _():
                        clear_copy(row).start()
                    n = n + (row >= start).astype(jnp.int32)
                return n

            n_started = lax.fori_loop(0, N_EXPERTS, clear_segment_end, 0)
            n_blocks = xs_ref.shape[0] // (MOE_ROWS * PACKED_ROWS)
            n_valid = pe_ref[N_EXPERTS - 1] // MOE_ROWS

            def clear_past_end(b, carry):
                clear_copy(b * MOE_ROWS).start()
                return carry

            lax.fori_loop(n_valid, n_blocks, clear_past_end, 0)

            def wait_one(_, carry):
                clear_copy(0).wait()
                return carry

            lax.fori_loop(0, n_started + n_blocks - n_valid, wait_one, 0)

    base = 0 if ls_blocked else i * (t * TOP_K)
    slot = i % 2

    def permute_and_send(s):
        def scatter(g, carry):
            for u in range(TOKEN_UNROLL):
                tok = g * TOKEN_UNROLL + u
                token_row = h_ref[rows(tok), :]
                for k in range(TOP_K):
                    stage[s, _rows_at(ls_ref[base + k * t + tok], PACKED_ROWS), :] = token_row
            return carry

        lax.fori_loop(0, t // TOKEN_UNROLL, scatter, 0)
        _start_run_copies(
            plan_ref, t,
            lambda loc, glob, n: pltpu.make_async_copy(stage.at[s, rows(loc, n), :],
                                                       xs_ref.at[rows(glob, n), :],
                                                       sems.at[s]))

    for s in range(2):
        pl.when(slot == s)(functools.partial(permute_and_send, s))

    def wait_stage(s):
        pltpu.make_async_copy(stage.at[s], xs_ref.at[rows(0, t * TOP_K), :], sems.at[s]).wait()

    @pl.when(i > 0)
    def _():
        wait_stage(1 - slot)

    @pl.when(i == pl.num_programs(0) - 1)
    def _():
        wait_stage(slot)


def _dispatch(plan, pad_ends, ls, h2t, xs, n_rows, tile, later_rows=0):
    n_tiles = h2t.shape[0] // (tile * PACKED_ROWS)
    first = xs is None
    clear_blocks = -(-(later_rows + MOE_ROWS - 1) // MOE_ROWS) if first else 0
    ls_blocked = (tile * TOP_K) % 1024 == 0
    ls_spec = (pl.BlockSpec((tile * TOP_K,), lambda i, *_: (i,), memory_space=pltpu.SMEM)
               if ls_blocked else pl.BlockSpec(memory_space=pltpu.SMEM))
    in_specs = [pl.BlockSpec((PLAN_WORDS,), lambda i, *_: (i,), memory_space=pltpu.SMEM),
                ls_spec, pl.BlockSpec((tile * PACKED_ROWS, LANES), lambda i, *_: (i, 0))]
    args = [pad_ends, plan, ls, h2t]
    if not first:
        in_specs.append(pl.BlockSpec(memory_space=pl.ANY))
        args.append(xs)
    grid_spec = pltpu.PrefetchScalarGridSpec(
        num_scalar_prefetch=1,
        grid=(n_tiles,),
        in_specs=in_specs,
        out_specs=pl.BlockSpec(memory_space=pl.ANY),
        scratch_shapes=[pltpu.VMEM((2, tile * TOP_K * PACKED_ROWS, LANES), jnp.uint32),
                        pltpu.VMEM((MOE_ROWS * PACKED_ROWS, LANES), jnp.uint32),
                        pltpu.SemaphoreType.DMA((2,)),
                        pltpu.SemaphoreType.DMA],
    )
    return pl.pallas_call(
        functools.partial(_dispatch_kernel, clear_blocks=clear_blocks, ls_blocked=ls_blocked),
        grid_spec=grid_spec,
        out_shape=jax.ShapeDtypeStruct((n_rows * PACKED_ROWS, LANES), jnp.uint32),
        input_output_aliases={} if first else {4: 0},
        compiler_params=_params(("arbitrary",)),
        name="dispatch",
    )(*args)


def _moe_kernel(be_ref, nv_ref, nx_ref, par_ref, xs_ref, bgu_ref, bdn_ref, wgu_hbm, wdn_hbm,
                ys_ref, wgu_f32, wdn_f32, wgu_bf, wdn_bf, sems, *, layer):
    i = pl.program_id(0)
    e = be_ref[i]
    prev = be_ref[jnp.maximum(i - 1, 0)]
    valid = i < nv_ref[0]
    rows = 128

    def weight_copies(expert, slot):
        return (pltpu.make_async_copy(wgu_hbm.at[layer, expert], wgu_f32.at[slot], sems.at[slot]),
                pltpu.make_async_copy(wdn_hbm.at[layer, expert], wdn_f32.at[slot], sems.at[slot]))

    @pl.when(valid & ((i == 0) | (e != prev)))
    def _():
        slot = par_ref[i]

        @pl.when(i == 0)
        def _():
            for cp in weight_copies(e, slot):
                cp.start()

        for cp in weight_copies(e, slot):
            cp.wait()
        nxt = nx_ref[i]

        @pl.when(nxt < N_EXPERTS)
        def _():
            for cp in weight_copies(nxt, 1 - slot):
                cp.start()

        def cast(r, carry):
            sl = pl.ds(pl.multiple_of(r * rows, rows), rows)
            wgu_bf[sl, :] = wgu_f32[slot, sl, :].astype(BF16)
            wdn_bf[sl, :] = wdn_f32[slot, sl, :].astype(BF16)
            return carry
        lax.fori_loop(0, D_MODEL // rows, cast, 0)

    @pl.when(valid)
    def _():
        x = _from_packed_rows(xs_ref, MOE_ROWS)
        gu = jnp.dot(x, wgu_bf[...], preferred_element_type=F32) + bgu_ref[...]
        gate = jnp.minimum(gu[:, :D_FF], SWIGLU_LIMIT)
        up = jnp.clip(gu[:, D_FF:], -SWIGLU_LIMIT, SWIGLU_LIMIT)
        a = gate * jax.nn.sigmoid(SWIGLU_ALPHA * gate) * (up + 1.0)
        y = jnp.dot(a.astype(BF16), wdn_bf[...], preferred_element_type=F32) + bdn_ref[...]
        _to_row_tiles(ys_ref, y)

    @pl.when(jnp.logical_not(valid))
    def _():
        ys_ref[...] = jnp.zeros_like(ys_ref)


def _moe(block_e, n_valid, next_e, parity, xs, w_gu, b_gu, w_down, b_down, layer):
    n_rows = xs.shape[0] // PACKED_ROWS
    n_blocks = n_rows // MOE_ROWS
    grid_spec = pltpu.PrefetchScalarGridSpec(
        num_scalar_prefetch=4,
        grid=(n_blocks,),
        in_specs=[
            pl.BlockSpec((MOE_ROWS * PACKED_ROWS, LANES),
                         lambda i, be, nv, *_: (jnp.minimum(i, nv[0] - 1), 0)),
            pl.BlockSpec((None, None, 1, 2 * D_FF), lambda i, be, *_: (layer, be[i], 0, 0)),
            pl.BlockSpec((None, None, 1, D_MODEL), lambda i, be, *_: (layer, be[i], 0, 0)),
            pl.BlockSpec(memory_space=pl.ANY),
            pl.BlockSpec(memory_space=pl.ANY),
        ],
        out_specs=pl.BlockSpec((MOE_ROWS * ROW_TILE, LANES), lambda i, *_: (i, 0)),
        scratch_shapes=[pltpu.VMEM((2, D_MODEL, 2 * D_FF), F32),
                        pltpu.VMEM((2, D_FF, D_MODEL), F32),
                        pltpu.VMEM((D_MODEL, 2 * D_FF), BF16),
                        pltpu.VMEM((D_FF, D_MODEL), BF16),
                        pltpu.SemaphoreType.DMA((2,))],
    )
    return pl.pallas_call(
        functools.partial(_moe_kernel, layer=layer),
        grid_spec=grid_spec,
        out_shape=jax.ShapeDtypeStruct((n_rows * ROW_TILE, LANES), F32),
        compiler_params=_params(("arbitrary",)),
        name="moe",
    )(block_e, n_valid, next_e, parity, xs, b_gu.reshape(DEPTH, N_EXPERTS, 1, 2 * D_FF),
      b_down.reshape(DEPTH, N_EXPERTS, 1, D_MODEL), w_gu, w_down)


def _combine_kernel(plan_ref, next_plan_ref, ls_ref, gate_ref, x1_ref, gf_ref, g2_ref, b2_ref,
                    ys_ref, o_ref, stage, ft, sems, *, ls_blocked):
    i = pl.program_id(0)
    t = x1_ref.shape[0]
    slot = i % 2

    def start_fetch(tile_plan_ref, s):
        _start_run_copies(
            tile_plan_ref, t,
            lambda loc, glob, n: pltpu.make_async_copy(ys_ref.at[_tile_rows(glob, n), :],
                                                       stage.at[s, _tile_rows(loc, n), :],
                                                       sems.at[s]))

    @pl.when(i == 0)
    def _():
        start_fetch(plan_ref, 0)

    for s in range(2):
        @pl.when((i + 1 < pl.num_programs(0)) & (slot == 1 - s))
        def _(s=s):
            start_fetch(next_plan_ref, s)

    pltpu.make_async_copy(ys_ref.at[_tile_rows(0, t * TOP_K), :], stage.at[slot],
                          sems.at[slot]).wait()

    base = 0 if ls_blocked else i * (t * TOP_K)

    def gate_sum(s):
        def gather(g, carry):
            for u in range(TOKEN_UNROLL):
                tok = g * TOKEN_UNROLL + u
                acc = None
                for k in range(TOP_K):
                    j = base + k * t + tok
                    v = stage[s, _rows_at(ls_ref[j]), :] * gate_ref[j]
                    acc = v if acc is None else acc + v
                ft[_tile_rows(tok), :] = acc
            return carry

        lax.fori_loop(0, t // TOKEN_UNROLL, gather, 0)

    for s in range(2):
        pl.when(slot == s)(functools.partial(gate_sum, s))
    f = _from_row_tiles(ft, t)
    z = DEEPNORM_ALPHA * x1_ref[...] + gf_ref[...] * f
    o_ref[...] = _layer_norm(z) * g2_ref[...] + b2_ref[...]


def _combine(plan, ls, gates, x1, gf, g2, b2, ys, tile, tiles_per_seq):
    n = x1.shape[0]
    n_tiles = n // tile
    ls_blocked = (tile * TOP_K) % 1024 == 0

    def smem_vec():
        if ls_blocked:
            return pl.BlockSpec((tile * TOP_K,), lambda i, *_: (i,), memory_space=pltpu.SMEM)
        return pl.BlockSpec(memory_space=pltpu.SMEM)

    grid_spec = pltpu.PrefetchScalarGridSpec(
        num_scalar_prefetch=0,
        grid=(n_tiles,),
        in_specs=[
            pl.BlockSpec((PLAN_WORDS,), lambda i: (i,), memory_space=pltpu.SMEM),
            pl.BlockSpec((PLAN_WORDS,), lambda i: (jnp.minimum(i + 1, n_tiles - 1),),
                         memory_space=pltpu.SMEM),
            smem_vec(), smem_vec(),
            pl.BlockSpec((tile, D_MODEL), lambda i, *_: (i, 0)),
            pl.BlockSpec((None, 1, D_MODEL), lambda i, *_: (i // tiles_per_seq, 0, 0)),
            pl.BlockSpec((1, D_MODEL), lambda i, *_: (0, 0)),
            pl.BlockSpec((1, D_MODEL), lambda i, *_: (0, 0)),
            pl.BlockSpec(memory_space=pl.ANY),
        ],
        out_specs=pl.BlockSpec((tile, D_MODEL), lambda i, *_: (i, 0)),
        scratch_shapes=[pltpu.VMEM((2, tile * TOP_K * ROW_TILE, LANES), F32),
                        pltpu.VMEM((tile * ROW_TILE, LANES), F32),
                        pltpu.SemaphoreType.DMA((2,))],
    )
    return pl.pallas_call(
        functools.partial(_combine_kernel, ls_blocked=ls_blocked),
        grid_spec=grid_spec,
        out_shape=jax.ShapeDtypeStruct((n, D_MODEL), F32),
        compiler_params=_params(("arbitrary",)),
        name="combine",
    )(plan, plan, ls, gates, x1, gf, g2, b2, ys)


def _alibi_slopes():
    return jnp.exp2(-ALIBI_MAX * (jnp.arange(N_HEADS, dtype=F32) + 1.0) / N_HEADS)


def _prompt_bias(slopes):
    r = jnp.arange(Q_TILE)
    j = jnp.arange(WINDOW + Q_TILE)
    dc = (WINDOW + r)[:, None] // CHUNK - j[None, :] // CHUNK
    vis = (dc >= 0) & (dc <= WIN_CHUNKS)
    dist = jnp.abs(r[:, None] + WINDOW - j[None, :]).astype(F32)
    bias = jnp.where(vis[None], -slopes[:, None, None] * dist[None] * LOG2_E, NEG_INF)
    return jnp.stack([bias, jnp.where(j[None, None, :] < WINDOW, NEG_INF, bias)])


def _sample_bias(slopes, t, hist):
    qpos = PAST_LEN + jnp.arange(t)
    kpos = PAST_LEN - hist + jnp.arange(hist + t)
    dc = qpos[:, None] // CHUNK - kpos[None, :] // CHUNK
    vis = (dc >= 0) & (dc <= WIN_CHUNKS) & (kpos[None, :] >= 0)
    dist = jnp.abs(qpos[:, None] - kpos[None, :]).astype(F32)
    return jnp.where(vis[None], -slopes[:, None, None] * dist[None] * LOG2_E, NEG_INF)[None]


def _proj_weights(w_in, b_in):
    col_scale = jnp.where(jnp.arange(PROJ_COLS) < ATT_DIM, ATTN_SCALE * LOG2_E, 1.0).astype(F32)
    return (w_in * col_scale).astype(BF16), (b_in * col_scale).reshape(1, PROJ_COLS)


def _pool_weight(w_pool):
    n = len(POOL_WINDOWS)
    eye = jnp.eye(n, dtype=w_pool.dtype)
    return jnp.einsum('gcd,gh->gchd', w_pool, eye).reshape(POOL_DIM, POOL_DIM).astype(BF16)


def _split_kv(kv, nb, rows):
    tail = kv[:, -rows:].astype(F32)
    shape = (nb, rows, N_KV, HEAD_DIM)
    return tail[..., :KV_DIM].reshape(shape), tail[..., KV_DIM:].reshape(shape)


def kernel(x_prompt, x_sample, c_prompt, c_sample, cache_k, cache_v, state_pool, w_ada, b_ada,
           w_in, b_in, sinks, w_pool, pool_scale, w_out, b_out, ln1_g, ln1_b, ln2_g, ln2_b,
           w_router, b_router, w_gu, b_gu, w_down, b_down):
    nbp, seq, _ = x_prompt.shape
    nbs, tdec, _ = x_sample.shape
    hist = cache_k.shape[2]
    n_p, n_s = nbp * seq, nbs * tdec
    n_asg = (n_p + n_s) * TOP_K
    n_blocks = -(-(n_asg + N_EXPERTS * (MOE_ROWS - 1)) // MOE_ROWS)
    n_rows = n_blocks * MOE_ROWS
    assert hist == WINDOW and tdec >= POOL_PAD and seq % TOK_TILE == 0

    nb_all = nbp + nbs
    nb_pad = -(-nb_all // SUBLANES) * SUBLANES
    c_all = jnp.concatenate([c_prompt, c_sample, jnp.zeros((nb_pad - nb_all, D_MODEL), F32)], 0)
    mod_all = _adaln(c_all, w_ada, b_ada).reshape(DEPTH, nb_pad, 6, D_MODEL)

    slopes = _alibi_slopes()
    bias_p = _prompt_bias(slopes)
    bias_s = _sample_bias(slopes, tdec, hist)
    row = lambda v: v.reshape(1, -1)

    yp, ys_tok = x_prompt, x_sample
    outs = {k: [] for k in ("kp", "vp", "pp", "ks", "vs", "ps")}
    for l in range(DEPTH):
        mod = mod_all[l]
        w_proj, b_proj = _proj_weights(w_in[l], b_in[l])
        wpool = _pool_weight(w_pool[l])
        pscale = row(pool_scale[l])
        wout = w_out[l].astype(BF16)
        wr_t = w_router[l].T
        wr_hi = wr_t.astype(BF16)
        wr = jnp.concatenate([wr_hi, (wr_t - wr_hi.astype(F32)).astype(BF16)], axis=0)
        br = b_router[l].reshape(N_EXPERTS, 1)

        q, kv, u = _inproj(yp, mod, w_proj, b_proj, INPROJ_TILE, 0)
        sinks2 = sinks[l] * LOG2_E
        mix_p = _mixer(sinks2, q, kv, u, None, None, bias_p, wpool, pscale,
                       tile=TOK_TILE, tq=Q_TILE, history_starts_empty=True, pos0=0)
        k_tail, v_tail = _split_kv(kv, nbp, WINDOW)
        outs["kp"].append(k_tail)
        outs["vp"].append(v_tail)
        outs["pp"].append(u[:, -POOL_PAD:])

        qs, kvs, us = _inproj(ys_tok, mod, w_proj, b_proj, tdec, nbp)
        cache_kv = jnp.concatenate([cache_k[l].reshape(nbs, hist, KV_DIM),
                                    cache_v[l].reshape(nbs, hist, KV_DIM)], axis=-1).astype(BF16)
        sp = jnp.pad(state_pool[l], ((0, 0), (POOL_HALO - POOL_PAD, 0), (0, 0)))
        mix_s = _mixer(sinks2, qs, kvs, us, cache_kv, sp, bias_s, wpool, pscale,
                       tile=tdec, tq=tdec, history_starts_empty=False, pos0=PAST_LEN)
        k_new, v_new = _split_kv(kvs, nbs, tdec)
        outs["ks"].append(jnp.concatenate([cache_k[l], k_new], axis=1)[:, -hist:])
        outs["vs"].append(jnp.concatenate([cache_v[l], v_new], axis=1)[:, -hist:])
        outs["ps"].append(us[:, -POOL_PAD:])

        post_w = (wout, row(b_out[l]), row(ln1_g[l]), row(ln1_b[l]), wr, br)
        zero_counts = jnp.zeros((N_EXPERTS, LANES), F32)
        x1p, h2tp, rlp, rgp, meta_p, cnt_p = _post(mix_p, yp, mod, *post_w, zero_counts, TOK_TILE, 0)
        x1s, h2ts, rls, rgs, meta_s, cnt = _post(mix_s, ys_tok, mod, *post_w, cnt_p, tdec, nbp)

        counts = cnt[:, 0].astype(jnp.int32)
        padded = (counts + MOE_ROWS - 1) // MOE_ROWS * MOE_ROWS
        pad_ends = jnp.cumsum(padded).astype(jnp.int32)
        pad_starts = pad_ends - padded
        block_start = jnp.arange(n_blocks, dtype=jnp.int32) * MOE_ROWS
        block_e = jnp.minimum(jnp.sum(pad_ends[None, :] <= block_start[:, None], axis=1),
                              N_EXPERTS - 1).astype(jnp.int32)
        n_valid = (pad_ends[-1:] // MOE_ROWS).astype(jnp.int32)
        ids = jnp.arange(N_EXPERTS, dtype=jnp.int32)
        present = padded > 0
        later = (ids[None, :] > ids[:, None]) & present[None, :]
        next_present = jnp.min(jnp.where(later, ids[None, :], N_EXPERTS), axis=1)
        ordinal = jnp.cumsum(present.astype(jnp.int32)) - 1
        is_block_e = block_e[:, None] == ids[None, :]
        next_e = jnp.sum(jnp.where(is_block_e, next_present[None, :], 0), axis=1).astype(jnp.int32)
        parity = jnp.sum(jnp.where(is_block_e, ordinal[None, :] % 2, 0), axis=1).astype(jnp.int32)

        def copy_plan(meta, tile):
            m = meta.reshape(-1, N_EXPERTS, LANES)
            return _copy_plan(m[:, :, 0], m[:, :, 1], pad_starts[None, :] + m[:, :, 2],
                              tile.bit_length())

        def per_assignment(r, n, first=0):
            return r[:, :, first:first + TOP_K, :].reshape(n * TOP_K)

        plan_p, plan_s = copy_plan(meta_p, TOK_TILE), copy_plan(meta_s, tdec)
        ls_p, ls_s = per_assignment(rlp, n_p), per_assignment(rls, n_s)

        xs = _dispatch(plan_p, pad_ends, per_assignment(rlp, n_p, TOP_K),
                       h2tp.reshape(n_p * PACKED_ROWS, LANES), None, n_rows, TOK_TILE,
                       later_rows=n_s)
        xs = _dispatch(plan_s, pad_ends, per_assignment(rls, n_s, TOP_K),
                       h2ts.reshape(n_s * PACKED_ROWS, LANES), xs, n_rows, tdec)
        ye = _moe(block_e, n_valid, next_e, parity, xs, w_gu, b_gu, w_down, b_down, l)

        g2, b2 = row(ln2_g[l]), row(ln2_b[l])
        gf = mod[:, 5].reshape(nb_pad, 1, D_MODEL)
        yp = _combine(plan_p, ls_p, per_assignment(rgp, n_p), x1p.reshape(n_p, D_MODEL),
                      gf[:nbp], g2, b2, ye, TOK_TILE, seq // TOK_TILE).reshape(nbp, seq, D_MODEL)
        ys_tok = _combine(plan_s, ls_s, per_assignment(rgs, n_s), x1s.reshape(n_s, D_MODEL),
                          gf[nbp:nb_all], g2, b2, ye, tdec, 1).reshape(nbs, tdec, D_MODEL)

    st = lambda k: jnp.stack(outs[k])
    return (yp, ys_tok, st("kp"), st("vp"), st("pp"), st("ks"), st("vs"), st("ps"))
```

```python
import functools

import jax
import jax.numpy as jnp
from jax import lax
from jax.experimental import pallas as pl
from jax.experimental.pallas import tpu as pltpu

F32 = jnp.float32
BF16 = jnp.bfloat16

D_MODEL = 1024
DEPTH = 2
CHUNK = 64
WINDOW = 128
WIN_CHUNKS = WINDOW // CHUNK
HEAD_DIM = 64
ATT_DIM = 768
N_HEADS = 12
N_KV = 4
GROUP = N_HEADS // N_KV
KV_DIM = N_KV * HEAD_DIM
ATTN_SCALE = HEAD_DIM ** -0.5
LOG2_E = 1.4426950408889634
ALIBI_MAX = 8.0
NEG_INF = -1e30
POOL_DIM = D_MODEL - ATT_DIM
POOL_WINDOWS = (2, 4, 8, 16)
POOL_CH = POOL_DIM // len(POOL_WINDOWS)
POOL_PAD = max(POOL_WINDOWS) - 1
N_EXPERTS = 32
TOP_K = 4
D_FF = D_MODEL
SWIGLU_LIMIT = 7.0
SWIGLU_ALPHA = 1.702
LN_EPS = 1e-5
DEEPNORM_ALPHA = (2.0 * DEPTH) ** 0.25
PAST_LEN = 1024

LANES = 128
SUBLANES = 8
VMEM_LIMIT = 52 * 1024 * 1024

TOK_TILE = 512
Q_TILE = 128
POOL_HALO = 16
KD_DIM = N_KV * LANES
PROJ_COLS = ATT_DIM + 2 * KV_DIM + POOL_DIM
INPROJ_TILE = 1024
MOE_ROWS = 512
TOKEN_UNROLL = 8
PLAN_BITS = TOK_TILE.bit_length()
PLAN_WORDS = 1024
assert 2 * PLAN_BITS * N_EXPERTS + PLAN_BITS <= PLAN_WORDS
ROW_TILE = D_MODEL // LANES
assert ROW_TILE == SUBLANES
PACKED_ROWS = ROW_TILE // 2


def _params(sem, vmem=VMEM_LIMIT):
    return pltpu.CompilerParams(dimension_semantics=sem, vmem_limit_bytes=vmem)


def _layer_norm(x):
    mu = jnp.mean(x, axis=-1, keepdims=True)
    xc = x - mu
    var = jnp.mean(xc * xc, axis=-1, keepdims=True)
    return xc * lax.rsqrt(var + LN_EPS)


def _adaln_kernel(c_ref, w_ref, b_ref, o_ref):
    c = c_ref[...]
    s = c * jax.nn.sigmoid(c)
    o_ref[...] = jnp.dot(s, w_ref[...], preferred_element_type=F32,
                         precision=lax.Precision.HIGHEST) + b_ref[...]


def _adaln(c_all, w_ada, b_ada):
    nb = c_all.shape[0]
    ncol = 6 * D_MODEL
    tn = D_MODEL
    return pl.pallas_call(
        _adaln_kernel,
        grid=(DEPTH, ncol // tn),
        in_specs=[
            pl.BlockSpec((nb, D_MODEL), lambda l, j: (0, 0)),
            pl.BlockSpec((None, D_MODEL, tn), lambda l, j: (l, 0, j)),
            pl.BlockSpec((None, 1, tn), lambda l, j: (l, 0, j)),
        ],
        out_specs=pl.BlockSpec((None, nb, tn), lambda l, j: (l, 0, j)),
        out_shape=jax.ShapeDtypeStruct((DEPTH, nb, ncol), F32),
        compiler_params=_params(("parallel", "parallel")),
        name="adaln",
    )(c_all, w_ada, b_ada.reshape(DEPTH, 1, ncol))


def _inproj_kernel(x_ref, mod_ref, w_ref, b_ref, q_ref, kv_ref, u_ref):
    h = _layer_norm(x_ref[...]) * (1.0 + mod_ref[1:2, :]) + mod_ref[0:1, :]
    p = jnp.dot(h.astype(BF16), w_ref[...], preferred_element_type=F32) + b_ref[...]
    q_ref[...] = p[:, :ATT_DIM].astype(BF16)
    kv_ref[...] = p[:, ATT_DIM:ATT_DIM + 2 * KV_DIM].astype(BF16)
    u_ref[...] = p[:, ATT_DIM + 2 * KV_DIM:]


def _inproj(x, mod, w, b, tile, mod_row0):
    nb, s, _ = x.shape
    nt = s // tile

    def tok(c):
        return pl.BlockSpec((None, tile, c), lambda bi, i: (bi, i, 0))

    return pl.pallas_call(
        _inproj_kernel,
        grid=(nb, nt),
        in_specs=[
            tok(D_MODEL),
            pl.BlockSpec((None, 6, D_MODEL), lambda bi, i: (bi + mod_row0, 0, 0)),
            pl.BlockSpec((D_MODEL, PROJ_COLS), lambda bi, i: (0, 0)),
            pl.BlockSpec((1, PROJ_COLS), lambda bi, i: (0, 0)),
        ],
        out_specs=[tok(ATT_DIM), tok(2 * KV_DIM), tok(POOL_DIM)],
        out_shape=[
            jax.ShapeDtypeStruct((nb, s, ATT_DIM), BF16),
            jax.ShapeDtypeStruct((nb, s, 2 * KV_DIM), BF16),
            jax.ShapeDtypeStruct((nb, s, POOL_DIM), F32),
        ],
        compiler_params=_params(("parallel", "parallel")),
        name="inproj",
    )(x, mod, w, b)


def _dup_heads(x):
    r = x.shape[0]
    low = lax.broadcasted_iota(jnp.int32, (r // 2, LANES), 1) < HEAD_DIM
    parts = []
    for g in range(KV_DIM // LANES):
        pair = pltpu.bitcast(x[:, g * LANES:(g + 1) * LANES], jnp.uint32)
        swapped = pltpu.roll(pair, HEAD_DIM, 1)
        parts += [jnp.where(low, pair, swapped), jnp.where(low, swapped, pair)]
    return pltpu.bitcast(jnp.concatenate(parts, axis=1), BF16)


def _mixer_kernel(sinks_ref, q_ref, kv_ref, kvh_ref, u_ref, uh_ref,
                  bias_ref, wpool_ref, pscale_ref, o_ref, kall_ref, vall_ref,
                  *, tq, history_starts_empty, pos0):
    i = pl.program_id(1)
    t = q_ref.shape[0]
    hk = kvh_ref.shape[0]
    nk = hk + tq
    @pl.when(i >= 0)
    def _():
        kall_ref[0:hk, :] = _dup_heads(kvh_ref[:, :KV_DIM])
        kall_ref[hk:, :] = _dup_heads(kv_ref[:, :KV_DIM])
        vall_ref[0:hk, :] = _dup_heads(kvh_ref[:, KV_DIM:])
        vall_ref[hk:, :] = _dup_heads(kv_ref[:, KV_DIM:])

    low_half = lax.broadcasted_iota(jnp.int32, (tq, LANES), 1) < HEAD_DIM
    for s in range(t // tq):
        rows = slice(s * tq, (s + 1) * tq)
        keys = slice(s * tq, s * tq + nk)
        table = (i == 0).astype(jnp.int32) if (history_starts_empty and s == 0) else 0
        for pair in range(N_HEADS // 2):
            q2 = q_ref[rows, pair * LANES:(pair + 1) * LANES]
            halves = []
            for half in range(2):
                head = 2 * pair + half
                kv = head // GROUP
                qm = jnp.where(low_half if half == 0 else ~low_half, q2, jnp.zeros_like(q2))
                kd = kall_ref[keys, kv * LANES:(kv + 1) * LANES]
                sc = lax.dot_general(qm, kd, (((1,), (1,)), ((), ())),
                                     preferred_element_type=F32)
                sc = sc + bias_ref[table, head]
                sink = sinks_ref[head]
                m = jnp.maximum(jnp.max(sc, axis=1, keepdims=True), sink)
                p = jnp.exp2(sc - m)
                den = jnp.sum(p, axis=1, keepdims=True) + jnp.exp2(sink - m)
                vd = vall_ref[keys, kv * LANES:(kv + 1) * LANES]
                o2 = jnp.dot(p.astype(BF16), vd, preferred_element_type=F32)
                halves.append(o2 / den)
            o_ref[rows, pair * LANES:(pair + 1) * LANES] = jnp.where(
                low_half, halves[0], halves[1]).astype(BF16)

    uh = uh_ref[...]
    if history_starts_empty:
        uh = jnp.where(i == 0, 0.0, uh)
    u = u_ref[...]
    ue = jnp.concatenate([uh, u], axis=0)
    s2 = ue + pltpu.roll(ue, 1, 0)
    s4 = s2 + pltpu.roll(s2, 2, 0)
    s8 = s4 + pltpu.roll(s4, 4, 0)
    s16 = s8 + pltpu.roll(s8, 8, 0)
    lane = lax.broadcasted_iota(jnp.int32, (1, POOL_DIM), 1)
    g0, g1, g2 = lane < POOL_CH, lane < 2 * POOL_CH, lane < 3 * POOL_CH
    wsum = jnp.where(g0, s2[POOL_HALO:], jnp.where(g1, s4[POOL_HALO:],
                     jnp.where(g2, s8[POOL_HALO:], s16[POOL_HALO:])))
    width = jnp.where(g0, 2.0, jnp.where(g1, 4.0, jnp.where(g2, 8.0, 16.0))).astype(F32)
    pos = (pos0 + i * t + lax.broadcasted_iota(jnp.int32, (t, 1), 0)).astype(F32)
    cnt = jnp.minimum(width, pos + 1.0)
    pooled = wsum / cnt - u
    mixed = jnp.dot(pooled.astype(BF16), wpool_ref[...], preferred_element_type=F32)
    o_ref[:, ATT_DIM:] = (mixed * pscale_ref[...]).astype(BF16)


def _mixer(sinks, q, kv, u, kv_hist, u_hist, bias, wpool, pscale, *,
           tile, tq, history_starts_empty, pos0):
    nb, s, _ = q.shape
    nt = s // tile
    hk = WINDOW
    own_history = kv_hist is None
    if own_history:
        kv_hist, u_hist = kv, u
        kh_map = lambda bi, i: (bi, jnp.maximum(i * (tile // hk) - 1, 0), 0)
        uh_map = lambda bi, i: (bi, jnp.maximum(i * (tile // POOL_HALO) - 1, 0), 0)
    else:
        kh_map = lambda bi, i: (bi, 0, 0)
        uh_map = lambda bi, i: (bi, 0, 0)

    def tok(c):
        return pl.BlockSpec((None, tile, c), lambda bi, i: (bi, i, 0))

    kern = functools.partial(_mixer_kernel, tq=tq,
                             history_starts_empty=history_starts_empty, pos0=pos0)
    return pl.pallas_call(
        kern,
        grid=(nb, nt),
        in_specs=[
            pl.BlockSpec(memory_space=pltpu.SMEM),
            tok(ATT_DIM),
            tok(2 * KV_DIM),
            pl.BlockSpec((None, hk, 2 * KV_DIM), kh_map),
            tok(POOL_DIM),
            pl.BlockSpec((None, POOL_HALO, POOL_DIM), uh_map),
            pl.BlockSpec(bias.shape, lambda bi, i: (0, 0, 0, 0)),
            pl.BlockSpec((POOL_DIM, POOL_DIM), lambda bi, i: (0, 0)),
            pl.BlockSpec((1, POOL_DIM), lambda bi, i: (0, 0)),
        ],
        out_specs=tok(D_MODEL),
        out_shape=jax.ShapeDtypeStruct((nb, s, D_MODEL), BF16),
        scratch_shapes=[pltpu.VMEM((hk + tile, KD_DIM), BF16),
                        pltpu.VMEM((hk + tile, KD_DIM), BF16)],
        compiler_params=_params(("parallel", "parallel")),
        name="mixer",
    )(sinks, q, kv, kv_hist, u, u_hist, bias, wpool, pscale)


def _to_row_tiles(ref, x):
    t = x.shape[0]
    for s in range(ROW_TILE):
        ref[pl.ds(s, t, stride=ROW_TILE), :] = x[:, s * LANES:(s + 1) * LANES]


def _from_row_tiles(ref, t):
    return jnp.concatenate([ref[pl.ds(s, t, stride=ROW_TILE), :] for s in range(ROW_TILE)], axis=1)


_HIGH_HALF = 0xFFFF0000


def _to_packed_rows(ref, x):
    t = x.shape[0]
    half = D_MODEL // 2
    as_bits = lambda v: pltpu.bitcast(v.astype(BF16).astype(F32), jnp.uint32)
    words = (as_bits(x[:, :half]) >> 16) | (as_bits(x[:, half:]) & jnp.uint32(_HIGH_HALF))
    for s in range(PACKED_ROWS):
        ref[pl.ds(s, t, stride=PACKED_ROWS), :] = words[:, s * LANES:(s + 1) * LANES]


def _from_packed_rows(ref, t):
    words = [ref[pl.ds(s, t, stride=PACKED_ROWS), :] for s in range(PACKED_ROWS)]
    low = [pltpu.bitcast(w << 16, F32) for w in words]
    high = [pltpu.bitcast(w & jnp.uint32(_HIGH_HALF), F32) for w in words]
    return jnp.concatenate(low + high, axis=1).astype(BF16)


def _post_kernel(mix_ref, x_ref, mod_ref, wout_ref, bout_ref, g1_ref, b1_ref, wr_ref, br_ref,
                 cin_ref, x1_ref, h2t_ref, rl_ref, rg_ref, meta_ref, cnt_ref):
    first = (pl.program_id(0) == 0) & (pl.program_id(1) == 0)
    t = x_ref.shape[0]

    @pl.when(first)
    def _():
        cnt_ref[...] = cin_ref[...]

    mix = jnp.dot(mix_ref[...], wout_ref[...], preferred_element_type=F32) + bout_ref[...]
    z = DEEPNORM_ALPHA * x_ref[...] + mod_ref[2:3, :] * mix
    x1 = _layer_norm(z) * g1_ref[...] + b1_ref[...]
    x1_ref[...] = x1
    h2 = _layer_norm(x1) * (1.0 + mod_ref[4:5, :]) + mod_ref[3:4, :]
    _to_packed_rows(h2t_ref, h2)

    h2_hi = h2.astype(BF16)
    h2_lo = (h2 - h2_hi.astype(F32)).astype(BF16)
    contract_last = (((1,), (1,)), ((), ()))
    d_hi = lax.dot_general(wr_ref[...], h2_hi, contract_last, preferred_element_type=F32)
    d_lo = lax.dot_general(wr_ref[0:N_EXPERTS, :], h2_lo, contract_last,
                           preferred_element_type=F32)
    logits = d_hi[:N_EXPERTS] + d_hi[N_EXPERTS:] + d_lo + br_ref[...]
    expert = lax.broadcasted_iota(jnp.int32, (N_EXPERTS, t), 0).astype(F32)
    vals, hots = [], []
    cur = logits
    for _ in range(TOP_K):
        mk = jnp.max(cur, axis=0, keepdims=True)
        ik = jnp.min(jnp.where(cur == mk, expert, float(N_EXPERTS)), axis=0, keepdims=True)
        hot = expert == ik
        cur = jnp.where(hot, -jnp.inf, cur)
        vals.append(mk)
        hots.append(hot)
    exps = [jnp.exp(v - vals[0]) for v in vals]
    den = exps[0] + exps[1] + exps[2] + exps[3]
    gates = [e / den for e in exps]

    onehot = jnp.zeros((N_EXPERTS, t), F32)
    for hot in hots:
        onehot = onehot + hot.astype(F32)
    onehot_bf = onehot.astype(BF16)
    tok_r = lax.broadcasted_iota(jnp.int32, (t, t), 0)
    tok_c = lax.broadcasted_iota(jnp.int32, (t, t), 1)
    earlier = jnp.where(tok_r < tok_c, 1.0, 0.0).astype(BF16)
    rank = jnp.dot(onehot_bf, earlier, preferred_element_type=F32)
    exp_r = lax.broadcasted_iota(jnp.int32, (N_EXPERTS, N_EXPERTS), 0)
    exp_c = lax.broadcasted_iota(jnp.int32, (N_EXPERTS, N_EXPERTS), 1)
    lower = jnp.where(exp_c < exp_r, 1.0, 0.0).astype(BF16)
    tile_off = jnp.sum(jnp.dot(lower, onehot_bf, preferred_element_type=F32),
                       axis=1, keepdims=True)
    tile_cnt = jnp.sum(onehot, axis=1, keepdims=True)
    local = rank + tile_off
    slots = [jnp.sum(jnp.where(hot, local, 0.0), axis=0, keepdims=True) for hot in hots]
    rl_ref[...] = jnp.concatenate([v * float(ROW_TILE) for v in slots] +
                                  [v * float(PACKED_ROWS) for v in slots],
                                  axis=0).astype(jnp.int32)
    rg_ref[...] = jnp.concatenate(gates + [jnp.zeros_like(g) for g in gates], axis=0)
    lane = lax.broadcasted_iota(jnp.int32, (N_EXPERTS, LANES), 1)
    before = cnt_ref[...]
    meta = jnp.where(lane == 0, tile_cnt, jnp.where(lane == 1, tile_off,
                     jnp.where(lane == 2, before, 0.0)))
    meta_ref[...] = meta.astype(jnp.int32)
    cnt_ref[...] = before + tile_cnt


def _post(mix, x, mod, wout, bout, g1, b1, wr, br, counts_in, tile, mod_row0):
    nb, s, _ = x.shape
    nt = s // tile

    def tok(c):
        return pl.BlockSpec((None, tile, c), lambda bi, i: (bi, i, 0))

    def whole(shape):
        return pl.BlockSpec(shape, lambda bi, i: tuple(0 for _ in shape))

    return pl.pallas_call(
        _post_kernel,
        grid=(nb, nt),
        in_specs=[
            tok(D_MODEL), tok(D_MODEL),
            pl.BlockSpec((None, 6, D_MODEL), lambda bi, i: (bi + mod_row0, 0, 0)),
            whole((D_MODEL, D_MODEL)), whole((1, D_MODEL)),
            whole((1, D_MODEL)), whole((1, D_MODEL)),
            whole((2 * N_EXPERTS, D_MODEL)), whole((N_EXPERTS, 1)),
            whole((N_EXPERTS, LANES)),
        ],
        out_specs=[tok(D_MODEL),
                   pl.BlockSpec((None, tile * PACKED_ROWS, LANES), lambda bi, i: (bi, i, 0)),
                   pl.BlockSpec((None, None, 2 * TOP_K, tile), lambda bi, i: (bi, i, 0, 0)),
                   pl.BlockSpec((None, None, 2 * TOP_K, tile), lambda bi, i: (bi, i, 0, 0)),
                   pl.BlockSpec((None, None, N_EXPERTS, LANES), lambda bi, i: (bi, i, 0, 0)),
                   whole((N_EXPERTS, LANES))],
        out_shape=[
            jax.ShapeDtypeStruct((nb, s, D_MODEL), F32),
            jax.ShapeDtypeStruct((nb, s * PACKED_ROWS, LANES), jnp.uint32),
            jax.ShapeDtypeStruct((nb, nt, 2 * TOP_K, tile), jnp.int32),
            jax.ShapeDtypeStruct((nb, nt, 2 * TOP_K, tile), F32),
            jax.ShapeDtypeStruct((nb, nt, N_EXPERTS, LANES), jnp.int32),
            jax.ShapeDtypeStruct((N_EXPERTS, LANES), F32),
        ],
        compiler_params=_params(("arbitrary", "arbitrary")),
        name="post",
    )(mix, x, mod, wout, bout, g1, b1, wr, br, counts_in)


def _tile_rows(row, n=1, per=ROW_TILE):
    return pl.ds(pl.multiple_of(row * per, per), n * per)


def _rows_at(first_row, per=ROW_TILE):
    return pl.ds(pl.multiple_of(first_row, per), per)


def _copy_plan(cnt, off, dst, n_bits):
    n_tiles = cnt.shape[0]
    bits = jnp.arange(PLAN_BITS, dtype=jnp.int32)[None, :, None]
    c = cnt[:, None, :]
    valid = ((c >> bits) & 1) * (bits < n_bits)
    done = c & ((1 << bits) - 1)
    ids = jnp.arange(N_EXPERTS, dtype=jnp.int32)
    pos = jnp.sum(jnp.where(ids[None, :] < ids[:, None], valid[..., None, :], 0), axis=-1)
    place = (pos[..., :, None] == ids) & (valid[..., :, None] == 1)
    pack = lambda v: jnp.sum(jnp.where(place, v[..., :, None], 0), axis=-2)
    local = pack(off[:, None, :] + done)
    glob = pack(dst[:, None, :] + done)
    n = jnp.sum(valid, axis=-1)
    used = 2 * PLAN_BITS * N_EXPERTS + PLAN_BITS
    plan = jnp.concatenate([local.reshape(n_tiles, -1), glob.reshape(n_tiles, -1), n,
                            jnp.zeros((n_tiles, PLAN_WORDS - used), jnp.int32)], axis=1)
    return plan.reshape(-1).astype(jnp.int32)


def _start_run_copies(plan_ref, t, make_copy):
    for b in range(t.bit_length()):
        def piece(r, carry, b=b):
            j = b * N_EXPERTS + r
            make_copy(plan_ref[j], plan_ref[PLAN_BITS * N_EXPERTS + j], 1 << b).start()
            return carry

        lax.fori_loop(0, plan_ref[2 * PLAN_BITS * N_EXPERTS + b], piece, 0)


def _dispatch_kernel(pe_ref, plan_ref, ls_ref, h_ref, *rest, clear_blocks, ls_blocked):
    xs_ref, stage, zbuf, sems, sem = rest[-5:]
    i = pl.program_id(0)
    t = h_ref.shape[0] // PACKED_ROWS
    rows = functools.partial(_tile_rows, per=PACKED_ROWS)

    if clear_blocks:
        @pl.when(i == 0)
        def _():
            zbuf[...] = jnp.zeros_like(zbuf)

            def clear_copy(row):
                return pltpu.make_async_copy(zbuf, xs_ref.at[rows(row, MOE_ROWS), :], sem)

            def clear_segment_end(e, n):
                end = pe_ref[e]
                start = jnp.where(e == 0, 0, pe_ref[jnp.maximum(e - 1, 0)])
                for back in range(1, clear_blocks + 1):
                    row = end - back * MOE_ROWS

                    @pl.when(row >= start)
                    def _():
                        clear_copy(row).start()
                    n = n + (row >= start).astype(jnp.int32)
                return n

            n_started = lax.fori_loop(0, N_EXPERTS, clear_segment_end, 0)
            n_blocks = xs_ref.shape[0] // (MOE_ROWS * PACKED_ROWS)
            n_valid = pe_ref[N_EXPERTS - 1] // MOE_ROWS

            def clear_past_end(b, carry):
                clear_copy(b * MOE_ROWS).start()
                return carry

            lax.fori_loop(n_valid, n_blocks, clear_past_end, 0)

            def wait_one(_, carry):
                clear_copy(0).wait()
                return carry

            lax.fori_loop(0, n_started + n_blocks - n_valid, wait_one, 0)

    base = 0 if ls_blocked else i * (t * TOP_K)
    slot = i % 2

    def permute_and_send(s):
        def scatter(g, carry):
            for u in range(TOKEN_UNROLL):
                tok = g * TOKEN_UNROLL + u
                token_row = h_ref[rows(tok), :]
                for k in range(TOP_K):
                    stage[s, _rows_at(ls_ref[base + k * t + tok], PACKED_ROWS), :] = token_row
            return carry

        lax.fori_loop(0, t // TOKEN_UNROLL, scatter, 0)
        _start_run_copies(
            plan_ref, t,
            lambda loc, glob, n: pltpu.make_async_copy(stage.at[s, rows(loc, n), :],
                                                       xs_ref.at[rows(glob, n), :],
                                                       sems.at[s]))

    for s in range(2):
        pl.when(slot == s)(functools.partial(permute_and_send, s))

    def wait_stage(s):
        pltpu.make_async_copy(stage.at[s], xs_ref.at[rows(0, t * TOP_K), :], sems.at[s]).wait()

    @pl.when(i > 0)
    def _():
        wait_stage(1 - slot)

    @pl.when(i == pl.num_programs(0) - 1)
    def _():
        wait_stage(slot)


def _dispatch(plan, pad_ends, ls, h2t, xs, n_rows, tile, later_rows=0):
    n_tiles = h2t.shape[0] // (tile * PACKED_ROWS)
    first = xs is None
    clear_blocks = -(-(later_rows + MOE_ROWS - 1) // MOE_ROWS) if first else 0
    ls_blocked = (tile * TOP_K) % 1024 == 0
    ls_spec = (pl.BlockSpec((tile * TOP_K,), lambda i, *_: (i,), memory_space=pltpu.SMEM)
               if ls_blocked else pl.BlockSpec(memory_space=pltpu.SMEM))
    in_specs = [pl.BlockSpec((PLAN_WORDS,), lambda i, *_: (i,), memory_space=pltpu.SMEM),
                ls_spec, pl.BlockSpec((tile * PACKED_ROWS, LANES), lambda i, *_: (i, 0))]
    args = [pad_ends, plan, ls, h2t]
    if not first:
        in_specs.append(pl.BlockSpec(memory_space=pl.ANY))
        args.append(xs)
    grid_spec = pltpu.PrefetchScalarGridSpec(
        num_scalar_prefetch=1,
        grid=(n_tiles,),
        in_specs=in_specs,
        out_specs=pl.BlockSpec(memory_space=pl.ANY),
        scratch_shapes=[pltpu.VMEM((2, tile * TOP_K * PACKED_ROWS, LANES), jnp.uint32),
                        pltpu.VMEM((MOE_ROWS * PACKED_ROWS, LANES), jnp.uint32),
                        pltpu.SemaphoreType.DMA((2,)),
                        pltpu.SemaphoreType.DMA],
    )
    return pl.pallas_call(
        functools.partial(_dispatch_kernel, clear_blocks=clear_blocks, ls_blocked=ls_blocked),
        grid_spec=grid_spec,
        out_shape=jax.ShapeDtypeStruct((n_rows * PACKED_ROWS, LANES), jnp.uint32),
        input_output_aliases={} if first else {4: 0},
        compiler_params=_params(("arbitrary",)),
        name="dispatch",
    )(*args)


def _moe_kernel(be_ref, nv_ref, nx_ref, par_ref, rv_ref, xs_ref, bgu_ref, bdn_ref, wgu_hbm, wdn_hbm,
                ys_ref, wgu_f32, wdn_f32, wgu_bf, wdn_bf, sems, *, layer):
    i = pl.program_id(0)
    e = be_ref[i]
    prev = be_ref[jnp.maximum(i - 1, 0)]
    valid = i < nv_ref[0]
    rows = 128

    def weight_copies(expert, slot):
        return (pltpu.make_async_copy(wgu_hbm.at[layer, expert], wgu_f32.at[slot], sems.at[slot]),
                pltpu.make_async_copy(wdn_hbm.at[layer, expert], wdn_f32.at[slot], sems.at[slot]))

    @pl.when(valid & ((i == 0) | (e != prev)))
    def _():
        slot = par_ref[i]

        @pl.when(i == 0)
        def _():
            for cp in weight_copies(e, slot):
                cp.start()

        for cp in weight_copies(e, slot):
            cp.wait()
        nxt = nx_ref[i]

        @pl.when(nxt < N_EXPERTS)
        def _():
            for cp in weight_copies(nxt, 1 - slot):
                cp.start()

        def cast(r, carry):
            sl = pl.ds(pl.multiple_of(r * rows, rows), rows)
            wgu_bf[sl, :] = wgu_f32[slot, sl, :].astype(BF16)
            wdn_bf[sl, :] = wdn_f32[slot, sl, :].astype(BF16)
            return carry
        lax.fori_loop(0, D_MODEL // rows, cast, 0)

    def expert_rows(n):
        x = _from_packed_rows(xs_ref.at[pl.ds(0, n * PACKED_ROWS), :], n)
        gu = jnp.dot(x, wgu_bf[...], preferred_element_type=F32) + bgu_ref[...]
        gate = jnp.minimum(gu[:, :D_FF], SWIGLU_LIMIT)
        up = jnp.clip(gu[:, D_FF:], -SWIGLU_LIMIT, SWIGLU_LIMIT)
        a = gate * jax.nn.sigmoid(SWIGLU_ALPHA * gate) * (up + 1.0)
        y = jnp.dot(a.astype(BF16), wdn_bf[...], preferred_element_type=F32) + bdn_ref[...]
        _to_row_tiles(ys_ref.at[pl.ds(0, n * ROW_TILE), :], y)

    half = MOE_ROWS // 2
    few_rows = rv_ref[i] <= half

    @pl.when(valid & jnp.logical_not(few_rows))
    def _():
        expert_rows(MOE_ROWS)

    @pl.when(valid & few_rows)
    def _():
        expert_rows(half)
        ys_ref[pl.ds(half * ROW_TILE, half * ROW_TILE), :] = jnp.zeros(
            (half * ROW_TILE, LANES), F32)

    @pl.when(jnp.logical_not(valid))
    def _():
        ys_ref[...] = jnp.zeros_like(ys_ref)


def _moe(block_e, n_valid, next_e, parity, rows_used, xs, w_gu, b_gu, w_down, b_down, layer):
    n_rows = xs.shape[0] // PACKED_ROWS
    n_blocks = n_rows // MOE_ROWS
    grid_spec = pltpu.PrefetchScalarGridSpec(
        num_scalar_prefetch=5,
        grid=(n_blocks,),
        in_specs=[
            pl.BlockSpec((MOE_ROWS * PACKED_ROWS, LANES),
                         lambda i, be, nv, *_: (jnp.minimum(i, nv[0] - 1), 0)),
            pl.BlockSpec((None, None, 1, 2 * D_FF), lambda i, be, *_: (layer, be[i], 0, 0)),
            pl.BlockSpec((None, None, 1, D_MODEL), lambda i, be, *_: (layer, be[i], 0, 0)),
            pl.BlockSpec(memory_space=pl.ANY),
            pl.BlockSpec(memory_space=pl.ANY),
        ],
        out_specs=pl.BlockSpec((MOE_ROWS * ROW_TILE, LANES), lambda i, *_: (i, 0)),
        scratch_shapes=[pltpu.VMEM((2, D_MODEL, 2 * D_FF), F32),
                        pltpu.VMEM((2, D_FF, D_MODEL), F32),
                        pltpu.VMEM((D_MODEL, 2 * D_FF), BF16),
                        pltpu.VMEM((D_FF, D_MODEL), BF16),
                        pltpu.SemaphoreType.DMA((2,))],
    )
    return pl.pallas_call(
        functools.partial(_moe_kernel, layer=layer),
        grid_spec=grid_spec,
        out_shape=jax.ShapeDtypeStruct((n_rows * ROW_TILE, LANES), F32),
        compiler_params=_params(("arbitrary",)),
        name="moe",
    )(block_e, n_valid, next_e, parity, rows_used, xs, b_gu.reshape(DEPTH, N_EXPERTS, 1, 2 * D_FF),
      b_down.reshape(DEPTH, N_EXPERTS, 1, D_MODEL), w_gu, w_down)


def _combine_kernel(plan_ref, next_plan_ref, ls_ref, gate_ref, x1_ref, gf_ref, g2_ref, b2_ref,
                    ys_ref, o_ref, stage, ft, sems, *, ls_blocked):
    i = pl.program_id(0)
    t = x1_ref.shape[0]
    slot = i % 2

    def start_fetch(tile_plan_ref, s):
        _start_run_copies(
            tile_plan_ref, t,
            lambda loc, glob, n: pltpu.make_async_copy(ys_ref.at[_tile_rows(glob, n), :],
                                                       stage.at[s, _tile_rows(loc, n), :],
                                                       sems.at[s]))

    @pl.when(i == 0)
    def _():
        start_fetch(plan_ref, 0)

    for s in range(2):
        @pl.when((i + 1 < pl.num_programs(0)) & (slot == 1 - s))
        def _(s=s):
            start_fetch(next_plan_ref, s)

    pltpu.make_async_copy(ys_ref.at[_tile_rows(0, t * TOP_K), :], stage.at[slot],
                          sems.at[slot]).wait()

    base = 0 if ls_blocked else i * (t * TOP_K)

    def gate_sum(s):
        def gather(g, carry):
            for u in range(TOKEN_UNROLL):
                tok = g * TOKEN_UNROLL + u
                acc = None
                for k in range(TOP_K):
                    j = base + k * t + tok
                    v = stage[s, _rows_at(ls_ref[j]), :] * gate_ref[j]
                    acc = v if acc is None else acc + v
                ft[_tile_rows(tok), :] = acc
            return carry

        lax.fori_loop(0, t // TOKEN_UNROLL, gather, 0)

    for s in range(2):
        pl.when(slot == s)(functools.partial(gate_sum, s))
    f = _from_row_tiles(ft, t)
    z = DEEPNORM_ALPHA * x1_ref[...] + gf_ref[...] * f
    o_ref[...] = _layer_norm(z) * g2_ref[...] + b2_ref[...]


def _combine(plan, ls, gates, x1, gf, g2, b2, ys, tile, tiles_per_seq):
    n = x1.shape[0]
    n_tiles = n // tile
    ls_blocked = (tile * TOP_K) % 1024 == 0

    def smem_vec():
        if ls_blocked:
            return pl.BlockSpec((tile * TOP_K,), lambda i, *_: (i,), memory_space=pltpu.SMEM)
        return pl.BlockSpec(memory_space=pltpu.SMEM)

    grid_spec = pltpu.PrefetchScalarGridSpec(
        num_scalar_prefetch=0,
        grid=(n_tiles,),
        in_specs=[
            pl.BlockSpec((PLAN_WORDS,), lambda i: (i,), memory_space=pltpu.SMEM),
            pl.BlockSpec((PLAN_WORDS,), lambda i: (jnp.minimum(i + 1, n_tiles - 1),),
                         memory_space=pltpu.SMEM),
            smem_vec(), smem_vec(),
            pl.BlockSpec((tile, D_MODEL), lambda i, *_: (i, 0)),
            pl.BlockSpec((None, 1, D_MODEL), lambda i, *_: (i // tiles_per_seq, 0, 0)),
            pl.BlockSpec((1, D_MODEL), lambda i, *_: (0, 0)),
            pl.BlockSpec((1, D_MODEL), lambda i, *_: (0, 0)),
            pl.BlockSpec(memory_space=pl.ANY),
        ],
        out_specs=pl.BlockSpec((tile, D_MODEL), lambda i, *_: (i, 0)),
        scratch_shapes=[pltpu.VMEM((2, tile * TOP_K * ROW_TILE, LANES), F32),
                        pltpu.VMEM((tile * ROW_TILE, LANES), F32),
                        pltpu.SemaphoreType.DMA((2,))],
    )
    return pl.pallas_call(
        functools.partial(_combine_kernel, ls_blocked=ls_blocked),
        grid_spec=grid_spec,
        out_shape=jax.ShapeDtypeStruct((n, D_MODEL), F32),
        compiler_params=_params(("arbitrary",)),
        name="combine",
    )(plan, plan, ls, gates, x1, gf, g2, b2, ys)


def _alibi_slopes():
    return jnp.exp2(-ALIBI_MAX * (jnp.arange(N_HEADS, dtype=F32) + 1.0) / N_HEADS)


def _prompt_bias(slopes):
    r = jnp.arange(Q_TILE)
    j = jnp.arange(WINDOW + Q_TILE)
    dc = (WINDOW + r)[:, None] // CHUNK - j[None, :] // CHUNK
    vis = (dc >= 0) & (dc <= WIN_CHUNKS)
    dist = jnp.abs(r[:, None] + WINDOW - j[None, :]).astype(F32)
    bias = jnp.where(vis[None], -slopes[:, None, None] * dist[None] * LOG2_E, NEG_INF)
    return jnp.stack([bias, jnp.where(j[None, None, :] < WINDOW, NEG_INF, bias)])


def _sample_bias(slopes, t, hist):
    qpos = PAST_LEN + jnp.arange(t)
    kpos = PAST_LEN - hist + jnp.arange(hist + t)
    dc = qpos[:, None] // CHUNK - kpos[None, :] // CHUNK
    vis = (dc >= 0) & (dc <= WIN_CHUNKS) & (kpos[None, :] >= 0)
    dist = jnp.abs(qpos[:, None] - kpos[None, :]).astype(F32)
    return jnp.where(vis[None], -slopes[:, None, None] * dist[None] * LOG2_E, NEG_INF)[None]


def _proj_weights(w_in, b_in):
    col_scale = jnp.where(jnp.arange(PROJ_COLS) < ATT_DIM, ATTN_SCALE * LOG2_E, 1.0).astype(F32)
    return (w_in * col_scale).astype(BF16), (b_in * col_scale).reshape(1, PROJ_COLS)


def _pool_weight(w_pool):
    n = len(POOL_WINDOWS)
    eye = jnp.eye(n, dtype=w_pool.dtype)
    return jnp.einsum('gcd,gh->gchd', w_pool, eye).reshape(POOL_DIM, POOL_DIM).astype(BF16)


def _split_kv(kv, nb, rows):
    tail = kv[:, -rows:].astype(F32)
    shape = (nb, rows, N_KV, HEAD_DIM)
    return tail[..., :KV_DIM].reshape(shape), tail[..., KV_DIM:].reshape(shape)


def kernel(x_prompt, x_sample, c_prompt, c_sample, cache_k, cache_v, state_pool, w_ada, b_ada,
           w_in, b_in, sinks, w_pool, pool_scale, w_out, b_out, ln1_g, ln1_b, ln2_g, ln2_b,
           w_router, b_router, w_gu, b_gu, w_down, b_down):
    nbp, seq, _ = x_prompt.shape
    nbs, tdec, _ = x_sample.shape
    hist = cache_k.shape[2]
    n_p, n_s = nbp * seq, nbs * tdec
    n_asg = (n_p + n_s) * TOP_K
    n_blocks = -(-(n_asg + N_EXPERTS * (MOE_ROWS - 1)) // MOE_ROWS)
    n_rows = n_blocks * MOE_ROWS
    assert hist == WINDOW and tdec >= POOL_PAD and seq % TOK_TILE == 0

    nb_all = nbp + nbs
    nb_pad = -(-nb_all // SUBLANES) * SUBLANES
    c_all = jnp.concatenate([c_prompt, c_sample, jnp.zeros((nb_pad - nb_all, D_MODEL), F32)], 0)
    mod_all = _adaln(c_all, w_ada, b_ada).reshape(DEPTH, nb_pad, 6, D_MODEL)

    slopes = _alibi_slopes()
    bias_p = _prompt_bias(slopes)
    bias_s = _sample_bias(slopes, tdec, hist)
    row = lambda v: v.reshape(1, -1)

    yp, ys_tok = x_prompt, x_sample
    outs = {k: [] for k in ("kp", "vp", "pp", "ks", "vs", "ps")}
    for l in range(DEPTH):
        mod = mod_all[l]
        w_proj, b_proj = _proj_weights(w_in[l], b_in[l])
        wpool = _pool_weight(w_pool[l])
        pscale = row(pool_scale[l])
        wout = w_out[l].astype(BF16)
        wr_t = w_router[l].T
        wr_hi = wr_t.astype(BF16)
        wr = jnp.concatenate([wr_hi, (wr_t - wr_hi.astype(F32)).astype(BF16)], axis=0)
        br = b_router[l].reshape(N_EXPERTS, 1)

        q, kv, u = _inproj(yp, mod, w_proj, b_proj, INPROJ_TILE, 0)
        sinks2 = sinks[l] * LOG2_E
        mix_p = _mixer(sinks2, q, kv, u, None, None, bias_p, wpool, pscale,
                       tile=TOK_TILE, tq=Q_TILE, history_starts_empty=True, pos0=0)
        k_tail, v_tail = _split_kv(kv, nbp, WINDOW)
        outs["kp"].append(k_tail)
        outs["vp"].append(v_tail)
        outs["pp"].append(u[:, -POOL_PAD:])

        qs, kvs, us = _inproj(ys_tok, mod, w_proj, b_proj, tdec, nbp)
        cache_kv = jnp.concatenate([cache_k[l].reshape(nbs, hist, KV_DIM),
                                    cache_v[l].reshape(nbs, hist, KV_DIM)], axis=-1).astype(BF16)
        sp = jnp.pad(state_pool[l], ((0, 0), (POOL_HALO - POOL_PAD, 0), (0, 0)))
        mix_s = _mixer(sinks2, qs, kvs, us, cache_kv, sp, bias_s, wpool, pscale,
                       tile=tdec, tq=tdec, history_starts_empty=False, pos0=PAST_LEN)
        k_new, v_new = _split_kv(kvs, nbs, tdec)
        outs["ks"].append(jnp.concatenate([cache_k[l], k_new], axis=1)[:, -hist:])
        outs["vs"].append(jnp.concatenate([cache_v[l], v_new], axis=1)[:, -hist:])
        outs["ps"].append(us[:, -POOL_PAD:])

        post_w = (wout, row(b_out[l]), row(ln1_g[l]), row(ln1_b[l]), wr, br)
        zero_counts = jnp.zeros((N_EXPERTS, LANES), F32)
        x1p, h2tp, rlp, rgp, meta_p, cnt_p = _post(mix_p, yp, mod, *post_w, zero_counts, TOK_TILE, 0)
        x1s, h2ts, rls, rgs, meta_s, cnt = _post(mix_s, ys_tok, mod, *post_w, cnt_p, tdec, nbp)

        counts = cnt[:, 0].astype(jnp.int32)
        padded = (counts + MOE_ROWS - 1) // MOE_ROWS * MOE_ROWS
        pad_ends = jnp.cumsum(padded).astype(jnp.int32)
        pad_starts = pad_ends - padded
        block_start = jnp.arange(n_blocks, dtype=jnp.int32) * MOE_ROWS
        block_e = jnp.minimum(jnp.sum(pad_ends[None, :] <= block_start[:, None], axis=1),
                              N_EXPERTS - 1).astype(jnp.int32)
        n_valid = (pad_ends[-1:] // MOE_ROWS).astype(jnp.int32)
        ids = jnp.arange(N_EXPERTS, dtype=jnp.int32)
        present = padded > 0
        later = (ids[None, :] > ids[:, None]) & present[None, :]
        next_present = jnp.min(jnp.where(later, ids[None, :], N_EXPERTS), axis=1)
        ordinal = jnp.cumsum(present.astype(jnp.int32)) - 1
        is_block_e = block_e[:, None] == ids[None, :]
        next_e = jnp.sum(jnp.where(is_block_e, next_present[None, :], 0), axis=1).astype(jnp.int32)
        parity = jnp.sum(jnp.where(is_block_e, ordinal[None, :] % 2, 0), axis=1).astype(jnp.int32)
        seg_used_end = jnp.sum(jnp.where(is_block_e, (pad_starts + counts)[None, :], 0), axis=1)
        rows_used = jnp.clip(seg_used_end - block_start, 0, MOE_ROWS).astype(jnp.int32)

        def copy_plan(meta, tile):
            m = meta.reshape(-1, N_EXPERTS, LANES)
            return _copy_plan(m[:, :, 0], m[:, :, 1], pad_starts[None, :] + m[:, :, 2],
                              tile.bit_length())

        def per_assignment(r, n, first=0):
            return r[:, :, first:first + TOP_K, :].reshape(n * TOP_K)

        plan_p, plan_s = copy_plan(meta_p, TOK_TILE), copy_plan(meta_s, tdec)
        ls_p, ls_s = per_assignment(rlp, n_p), per_assignment(rls, n_s)

        xs = _dispatch(plan_p, pad_ends, per_assignment(rlp, n_p, TOP_K),
                       h2tp.reshape(n_p * PACKED_ROWS, LANES), None, n_rows, TOK_TILE,
                       later_rows=n_s)
        xs = _dispatch(plan_s, pad_ends, per_assignment(rls, n_s, TOP_K),
                       h2ts.reshape(n_s * PACKED_ROWS, LANES), xs, n_rows, tdec)
        ye = _moe(block_e, n_valid, next_e, parity, rows_used, xs, w_gu, b_gu, w_down, b_down, l)

        g2, b2 = row(ln2_g[l]), row(ln2_b[l])
        gf = mod[:, 5].reshape(nb_pad, 1, D_MODEL)
        yp = _combine(plan_p, ls_p, per_assignment(rgp, n_p), x1p.reshape(n_p, D_MODEL),
                      gf[:nbp], g2, b2, ye, TOK_TILE, seq // TOK_TILE).reshape(nbp, seq, D_MODEL)
        ys_tok = _combine(plan_s, ls_s, per_assignment(rgs, n_s), x1s.reshape(n_s, D_MODEL),
                          gf[nbp:nb_all], g2, b2, ye, tdec, 1).reshape(nbs, tdec, D_MODEL)

    st = lambda k: jnp.stack(outs[k])
    return (yp, ys_tok, st("kp"), st("vp"), st("pp"), st("ks"), st("vs"), st("ps"))
```

```python
import functools

import jax
import jax.numpy as jnp
from jax import lax
from jax.experimental import pallas as pl
from jax.experimental.pallas import tpu as pltpu

F32 = jnp.float32
BF16 = jnp.bfloat16

D_MODEL = 1024
DEPTH = 2
CHUNK = 64
WINDOW = 128
WIN_CHUNKS = WINDOW // CHUNK
HEAD_DIM = 64
ATT_DIM = 768
N_HEADS = 12
N_KV = 4
GROUP = N_HEADS // N_KV
KV_DIM = N_KV * HEAD_DIM
ATTN_SCALE = HEAD_DIM ** -0.5
LOG2_E = 1.4426950408889634
ALIBI_MAX = 8.0
NEG_INF = -1e30
POOL_DIM = D_MODEL - ATT_DIM
POOL_WINDOWS = (2, 4, 8, 16)
POOL_CH = POOL_DIM // len(POOL_WINDOWS)
POOL_PAD = max(POOL_WINDOWS) - 1
N_EXPERTS = 32
TOP_K = 4
D_FF = D_MODEL
SWIGLU_LIMIT = 7.0
SWIGLU_ALPHA = 1.702
LN_EPS = 1e-5
DEEPNORM_ALPHA = (2.0 * DEPTH) ** 0.25
PAST_LEN = 1024

LANES = 128
SUBLANES = 8
VMEM_LIMIT = 52 * 1024 * 1024

TOK_TILE = 512
Q_TILE = 128
POOL_HALO = 16
KD_DIM = N_KV * LANES
PROJ_COLS = ATT_DIM + 2 * KV_DIM + POOL_DIM
INPROJ_TILE = 1024
MOE_ROWS = 512
TOKEN_UNROLL = 16
PLAN_BITS = TOK_TILE.bit_length()
PLAN_WORDS = 1024
assert 2 * PLAN_BITS * N_EXPERTS + PLAN_BITS <= PLAN_WORDS
ROW_TILE = D_MODEL // LANES
assert ROW_TILE == SUBLANES
PACKED_ROWS = ROW_TILE // 2


def _params(sem, vmem=VMEM_LIMIT):
    return pltpu.CompilerParams(dimension_semantics=sem, vmem_limit_bytes=vmem)


def _layer_norm(x):
    mu = jnp.mean(x, axis=-1, keepdims=True)
    xc = x - mu
    var = jnp.mean(xc * xc, axis=-1, keepdims=True)
    return xc * lax.rsqrt(var + LN_EPS)


def _adaln_kernel(c_ref, w_ref, b_ref, o_ref):
    c = c_ref[...]
    s = c * jax.nn.sigmoid(c)
    o_ref[...] = jnp.dot(s, w_ref[...], preferred_element_type=F32,
                         precision=lax.Precision.HIGHEST) + b_ref[...]


def _adaln(c_all, w_ada, b_ada):
    nb = c_all.shape[0]
    ncol = 6 * D_MODEL
    tn = D_MODEL
    return pl.pallas_call(
        _adaln_kernel,
        grid=(DEPTH, ncol // tn),
        in_specs=[
            pl.BlockSpec((nb, D_MODEL), lambda l, j: (0, 0)),
            pl.BlockSpec((None, D_MODEL, tn), lambda l, j: (l, 0, j)),
            pl.BlockSpec((None, 1, tn), lambda l, j: (l, 0, j)),
        ],
        out_specs=pl.BlockSpec((None, nb, tn), lambda l, j: (l, 0, j)),
        out_shape=jax.ShapeDtypeStruct((DEPTH, nb, ncol), F32),
        compiler_params=_params(("parallel", "parallel")),
        name="adaln",
    )(c_all, w_ada, b_ada.reshape(DEPTH, 1, ncol))


def _inproj_kernel(x_ref, mod_ref, w_ref, b_ref, q_ref, kv_ref, u_ref):
    h = _layer_norm(x_ref[...]) * (1.0 + mod_ref[1:2, :]) + mod_ref[0:1, :]
    p = jnp.dot(h.astype(BF16), w_ref[...], preferred_element_type=F32) + b_ref[...]
    q_ref[...] = p[:, :ATT_DIM].astype(BF16)
    kv_ref[...] = p[:, ATT_DIM:ATT_DIM + 2 * KV_DIM].astype(BF16)
    u_ref[...] = p[:, ATT_DIM + 2 * KV_DIM:]


def _inproj(x, mod, w, b, tile, mod_row0):
    nb, s, _ = x.shape
    nt = s // tile

    def tok(c):
        return pl.BlockSpec((None, tile, c), lambda bi, i: (bi, i, 0))

    return pl.pallas_call(
        _inproj_kernel,
        grid=(nb, nt),
        in_specs=[
            tok(D_MODEL),
            pl.BlockSpec((None, 6, D_MODEL), lambda bi, i: (bi + mod_row0, 0, 0)),
            pl.BlockSpec((D_MODEL, PROJ_COLS), lambda bi, i: (0, 0)),
            pl.BlockSpec((1, PROJ_COLS), lambda bi, i: (0, 0)),
        ],
        out_specs=[tok(ATT_DIM), tok(2 * KV_DIM), tok(POOL_DIM)],
        out_shape=[
            jax.ShapeDtypeStruct((nb, s, ATT_DIM), BF16),
            jax.ShapeDtypeStruct((nb, s, 2 * KV_DIM), BF16),
            jax.ShapeDtypeStruct((nb, s, POOL_DIM), F32),
        ],
        compiler_params=_params(("parallel", "parallel")),
        name="inproj",
    )(x, mod, w, b)


def _dup_heads(x):
    r = x.shape[0]
    low = lax.broadcasted_iota(jnp.int32, (r // 2, LANES), 1) < HEAD_DIM
    parts = []
    for g in range(KV_DIM // LANES):
        pair = pltpu.bitcast(x[:, g * LANES:(g + 1) * LANES], jnp.uint32)
        swapped = pltpu.roll(pair, HEAD_DIM, 1)
        parts += [jnp.where(low, pair, swapped), jnp.where(low, swapped, pair)]
    return pltpu.bitcast(jnp.concatenate(parts, axis=1), BF16)


def _mixer_kernel(sinks_ref, q_ref, kv_ref, kvh_ref, u_ref, uh_ref,
                  bias_ref, wpool_ref, pscale_ref, o_ref, kall_ref, vall_ref,
                  *, tq, history_starts_empty, pos0):
    i = pl.program_id(1)
    t = q_ref.shape[0]
    hk = kvh_ref.shape[0]
    nk = hk + tq
    @pl.when(i >= 0)
    def _():
        kall_ref[0:hk, :] = _dup_heads(kvh_ref[:, :KV_DIM])
        kall_ref[hk:, :] = _dup_heads(kv_ref[:, :KV_DIM])
        vall_ref[0:hk, :] = _dup_heads(kvh_ref[:, KV_DIM:])
        vall_ref[hk:, :] = _dup_heads(kv_ref[:, KV_DIM:])

    low_half = lax.broadcasted_iota(jnp.int32, (tq, LANES), 1) < HEAD_DIM
    for s in range(t // tq):
        rows = slice(s * tq, (s + 1) * tq)
        keys = slice(s * tq, s * tq + nk)
        table = (i == 0).astype(jnp.int32) if (history_starts_empty and s == 0) else 0
        for pair in range(N_HEADS // 2):
            q2 = q_ref[rows, pair * LANES:(pair + 1) * LANES]
            halves = []
            for half in range(2):
                head = 2 * pair + half
                kv = head // GROUP
                qm = jnp.where(low_half if half == 0 else ~low_half, q2, jnp.zeros_like(q2))
                kd = kall_ref[keys, kv * LANES:(kv + 1) * LANES]
                sc = lax.dot_general(qm, kd, (((1,), (1,)), ((), ())),
                                     preferred_element_type=F32)
                sc = sc + bias_ref[table, head]
                sink = sinks_ref[head]
                m = jnp.maximum(jnp.max(sc, axis=1, keepdims=True), sink)
                p = jnp.exp2(sc - m)
                den = jnp.sum(p, axis=1, keepdims=True) + jnp.exp2(sink - m)
                vd = vall_ref[keys, kv * LANES:(kv + 1) * LANES]
                o2 = jnp.dot(p.astype(BF16), vd, preferred_element_type=F32)
                halves.append(o2 / den)
            o_ref[rows, pair * LANES:(pair + 1) * LANES] = jnp.where(
                low_half, halves[0], halves[1]).astype(BF16)

    uh = uh_ref[...]
    if history_starts_empty:
        uh = jnp.where(i == 0, 0.0, uh)
    u = u_ref[...]
    ue = jnp.concatenate([uh, u], axis=0)
    s2 = ue + pltpu.roll(ue, 1, 0)
    s4 = s2 + pltpu.roll(s2, 2, 0)
    s8 = s4 + pltpu.roll(s4, 4, 0)
    s16 = s8 + pltpu.roll(s8, 8, 0)
    lane = lax.broadcasted_iota(jnp.int32, (1, POOL_DIM), 1)
    g0, g1, g2 = lane < POOL_CH, lane < 2 * POOL_CH, lane < 3 * POOL_CH
    wsum = jnp.where(g0, s2[POOL_HALO:], jnp.where(g1, s4[POOL_HALO:],
                     jnp.where(g2, s8[POOL_HALO:], s16[POOL_HALO:])))
    width = jnp.where(g0, 2.0, jnp.where(g1, 4.0, jnp.where(g2, 8.0, 16.0))).astype(F32)
    pos = (pos0 + i * t + lax.broadcasted_iota(jnp.int32, (t, 1), 0)).astype(F32)
    cnt = jnp.minimum(width, pos + 1.0)
    pooled = wsum / cnt - u
    mixed = jnp.dot(pooled.astype(BF16), wpool_ref[...], preferred_element_type=F32)
    o_ref[:, ATT_DIM:] = (mixed * pscale_ref[...]).astype(BF16)


def _mixer(sinks, q, kv, u, kv_hist, u_hist, bias, wpool, pscale, *,
           tile, tq, history_starts_empty, pos0):
    nb, s, _ = q.shape
    nt = s // tile
    hk = WINDOW
    own_history = kv_hist is None
    if own_history:
        kv_hist, u_hist = kv, u
        kh_map = lambda bi, i: (bi, jnp.maximum(i * (tile // hk) - 1, 0), 0)
        uh_map = lambda bi, i: (bi, jnp.maximum(i * (tile // POOL_HALO) - 1, 0), 0)
    else:
        kh_map = lambda bi, i: (bi, 0, 0)
        uh_map = lambda bi, i: (bi, 0, 0)

    def tok(c):
        return pl.BlockSpec((None, tile, c), lambda bi, i: (bi, i, 0))

    kern = functools.partial(_mixer_kernel, tq=tq,
                             history_starts_empty=history_starts_empty, pos0=pos0)
    return pl.pallas_call(
        kern,
        grid=(nb, nt),
        in_specs=[
            pl.BlockSpec(memory_space=pltpu.SMEM),
            tok(ATT_DIM),
            tok(2 * KV_DIM),
            pl.BlockSpec((None, hk, 2 * KV_DIM), kh_map),
            tok(POOL_DIM),
            pl.BlockSpec((None, POOL_HALO, POOL_DIM), uh_map),
            pl.BlockSpec(bias.shape, lambda bi, i: (0, 0, 0, 0)),
            pl.BlockSpec((POOL_DIM, POOL_DIM), lambda bi, i: (0, 0)),
            pl.BlockSpec((1, POOL_DIM), lambda bi, i: (0, 0)),
        ],
        out_specs=tok(D_MODEL),
        out_shape=jax.ShapeDtypeStruct((nb, s, D_MODEL), BF16),
        scratch_shapes=[pltpu.VMEM((hk + tile, KD_DIM), BF16),
                        pltpu.VMEM((hk + tile, KD_DIM), BF16)],
        compiler_params=_params(("parallel", "parallel")),
        name="mixer",
    )(sinks, q, kv, kv_hist, u, u_hist, bias, wpool, pscale)


def _to_row_tiles(ref, x):
    t = x.shape[0]
    for s in range(ROW_TILE):
        ref[pl.ds(s, t, stride=ROW_TILE), :] = x[:, s * LANES:(s + 1) * LANES]


def _from_row_tiles(ref, t):
    return jnp.concatenate([ref[pl.ds(s, t, stride=ROW_TILE), :] for s in range(ROW_TILE)], axis=1)


_HIGH_HALF = 0xFFFF0000


def _to_packed_rows(ref, x):
    t = x.shape[0]
    half = D_MODEL // 2
    as_bits = lambda v: pltpu.bitcast(v.astype(BF16).astype(F32), jnp.uint32)
    words = (as_bits(x[:, :half]) >> 16) | (as_bits(x[:, half:]) & jnp.uint32(_HIGH_HALF))
    for s in range(PACKED_ROWS):
        ref[pl.ds(s, t, stride=PACKED_ROWS), :] = words[:, s * LANES:(s + 1) * LANES]


def _from_packed_rows(ref, t):
    words = [ref[pl.ds(s, t, stride=PACKED_ROWS), :] for s in range(PACKED_ROWS)]
    low = [pltpu.bitcast(w << 16, F32) for w in words]
    high = [pltpu.bitcast(w & jnp.uint32(_HIGH_HALF), F32) for w in words]
    return jnp.concatenate(low + high, axis=1).astype(BF16)


def _post_kernel(mix_ref, x_ref, mod_ref, wout_ref, bout_ref, g1_ref, b1_ref, wr_ref, br_ref,
                 cin_ref, x1_ref, h2t_ref, rl_ref, rg_ref, meta_ref, cnt_ref):
    first = (pl.program_id(0) == 0) & (pl.program_id(1) == 0)
    t = x_ref.shape[0]

    @pl.when(first)
    def _():
        cnt_ref[...] = cin_ref[...]

    mix = jnp.dot(mix_ref[...], wout_ref[...], preferred_element_type=F32) + bout_ref[...]
    z = DEEPNORM_ALPHA * x_ref[...] + mod_ref[2:3, :] * mix
    x1 = _layer_norm(z) * g1_ref[...] + b1_ref[...]
    x1_ref[...] = x1
    h2 = _layer_norm(x1) * (1.0 + mod_ref[4:5, :]) + mod_ref[3:4, :]
    _to_packed_rows(h2t_ref, h2)

    h2_hi = h2.astype(BF16)
    h2_lo = (h2 - h2_hi.astype(F32)).astype(BF16)
    contract_last = (((1,), (1,)), ((), ()))
    d_hi = lax.dot_general(wr_ref[...], h2_hi, contract_last, preferred_element_type=F32)
    d_lo = lax.dot_general(wr_ref[0:N_EXPERTS, :], h2_lo, contract_last,
                           preferred_element_type=F32)
    logits = d_hi[:N_EXPERTS] + d_hi[N_EXPERTS:] + d_lo + br_ref[...]
    expert = lax.broadcasted_iota(jnp.int32, (N_EXPERTS, t), 0).astype(F32)
    vals, hots = [], []
    cur = logits
    for _ in range(TOP_K):
        mk = jnp.max(cur, axis=0, keepdims=True)
        ik = jnp.min(jnp.where(cur == mk, expert, float(N_EXPERTS)), axis=0, keepdims=True)
        hot = expert == ik
        cur = jnp.where(hot, -jnp.inf, cur)
        vals.append(mk)
        hots.append(hot)
    exps = [jnp.exp(v - vals[0]) for v in vals]
    den = exps[0] + exps[1] + exps[2] + exps[3]
    gates = [e / den for e in exps]

    onehot = jnp.zeros((N_EXPERTS, t), F32)
    for hot in hots:
        onehot = onehot + hot.astype(F32)
    onehot_bf = onehot.astype(BF16)
    tok_r = lax.broadcasted_iota(jnp.int32, (t, t), 0)
    tok_c = lax.broadcasted_iota(jnp.int32, (t, t), 1)
    earlier = jnp.where(tok_r < tok_c, 1.0, 0.0).astype(BF16)
    rank = jnp.dot(onehot_bf, earlier, preferred_element_type=F32)
    exp_r = lax.broadcasted_iota(jnp.int32, (N_EXPERTS, N_EXPERTS), 0)
    exp_c = lax.broadcasted_iota(jnp.int32, (N_EXPERTS, N_EXPERTS), 1)
    lower = jnp.where(exp_c < exp_r, 1.0, 0.0).astype(BF16)
    tile_off = jnp.sum(jnp.dot(lower, onehot_bf, preferred_element_type=F32),
                       axis=1, keepdims=True)
    tile_cnt = jnp.sum(onehot, axis=1, keepdims=True)
    local = rank + tile_off
    slots = [jnp.sum(jnp.where(hot, local, 0.0), axis=0, keepdims=True) for hot in hots]
    rl_ref[...] = jnp.concatenate([v * float(ROW_TILE) for v in slots] +
                                  [v * float(PACKED_ROWS) for v in slots],
                                  axis=0).astype(jnp.int32)
    rg_ref[...] = jnp.concatenate(gates + [jnp.zeros_like(g) for g in gates], axis=0)
    lane = lax.broadcasted_iota(jnp.int32, (N_EXPERTS, LANES), 1)
    before = cnt_ref[...]
    meta = jnp.where(lane == 0, tile_cnt, jnp.where(lane == 1, tile_off,
                     jnp.where(lane == 2, before, 0.0)))
    meta_ref[...] = meta.astype(jnp.int32)
    cnt_ref[...] = before + tile_cnt


def _post(mix, x, mod, wout, bout, g1, b1, wr, br, counts_in, tile, mod_row0):
    nb, s, _ = x.shape
    nt = s // tile

    def tok(c):
        return pl.BlockSpec((None, tile, c), lambda bi, i: (bi, i, 0))

    def whole(shape):
        return pl.BlockSpec(shape, lambda bi, i: tuple(0 for _ in shape))

    return pl.pallas_call(
        _post_kernel,
        grid=(nb, nt),
        in_specs=[
            tok(D_MODEL), tok(D_MODEL),
            pl.BlockSpec((None, 6, D_MODEL), lambda bi, i: (bi + mod_row0, 0, 0)),
            whole((D_MODEL, D_MODEL)), whole((1, D_MODEL)),
            whole((1, D_MODEL)), whole((1, D_MODEL)),
            whole((2 * N_EXPERTS, D_MODEL)), whole((N_EXPERTS, 1)),
            whole((N_EXPERTS, LANES)),
        ],
        out_specs=[tok(D_MODEL),
                   pl.BlockSpec((None, tile * PACKED_ROWS, LANES), lambda bi, i: (bi, i, 0)),
                   pl.BlockSpec((None, None, 2 * TOP_K, tile), lambda bi, i: (bi, i, 0, 0)),
                   pl.BlockSpec((None, None, 2 * TOP_K, tile), lambda bi, i: (bi, i, 0, 0)),
                   pl.BlockSpec((None, None, N_EXPERTS, LANES), lambda bi, i: (bi, i, 0, 0)),
                   whole((N_EXPERTS, LANES))],
        out_shape=[
            jax.ShapeDtypeStruct((nb, s, D_MODEL), F32),
            jax.ShapeDtypeStruct((nb, s * PACKED_ROWS, LANES), jnp.uint32),
            jax.ShapeDtypeStruct((nb, nt, 2 * TOP_K, tile), jnp.int32),
            jax.ShapeDtypeStruct((nb, nt, 2 * TOP_K, tile), F32),
            jax.ShapeDtypeStruct((nb, nt, N_EXPERTS, LANES), jnp.int32),
            jax.ShapeDtypeStruct((N_EXPERTS, LANES), F32),
        ],
        compiler_params=_params(("arbitrary", "arbitrary")),
        name="post",
    )(mix, x, mod, wout, bout, g1, b1, wr, br, counts_in)


def _tile_rows(row, n=1, per=ROW_TILE):
    return pl.ds(pl.multiple_of(row * per, per), n * per)


def _rows_at(first_row, per=ROW_TILE):
    return pl.ds(pl.multiple_of(first_row, per), per)


def _copy_plan(cnt, off, dst, n_bits):
    n_tiles = cnt.shape[0]
    bits = jnp.arange(PLAN_BITS, dtype=jnp.int32)[None, :, None]
    c = cnt[:, None, :]
    valid = ((c >> bits) & 1) * (bits < n_bits)
    done = c & ((1 << bits) - 1)
    ids = jnp.arange(N_EXPERTS, dtype=jnp.int32)
    pos = jnp.sum(jnp.where(ids[None, :] < ids[:, None], valid[..., None, :], 0), axis=-1)
    place = (pos[..., :, None] == ids) & (valid[..., :, None] == 1)
    pack = lambda v: jnp.sum(jnp.where(place, v[..., :, None], 0), axis=-2)
    local = pack(off[:, None, :] + done)
    glob = pack(dst[:, None, :] + done)
    n = jnp.sum(valid, axis=-1)
    used = 2 * PLAN_BITS * N_EXPERTS + PLAN_BITS
    plan = jnp.concatenate([local.reshape(n_tiles, -1), glob.reshape(n_tiles, -1), n,
                            jnp.zeros((n_tiles, PLAN_WORDS - used), jnp.int32)], axis=1)
    return plan.reshape(-1).astype(jnp.int32)


def _start_run_copies(plan_ref, t, make_copy):
    for b in range(t.bit_length()):
        def piece(r, carry, b=b):
            j = b * N_EXPERTS + r
            make_copy(plan_ref[j], plan_ref[PLAN_BITS * N_EXPERTS + j], 1 << b).start()
            return carry

        lax.fori_loop(0, plan_ref[2 * PLAN_BITS * N_EXPERTS + b], piece, 0)


def _dispatch_kernel(pe_ref, plan_ref, ls_ref, h_ref, *rest, clear_blocks, ls_blocked):
    xs_ref, stage, zbuf, sems, sem = rest[-5:]
    i = pl.program_id(0)
    t = h_ref.shape[0] // PACKED_ROWS
    rows = functools.partial(_tile_rows, per=PACKED_ROWS)

    if clear_blocks:
        @pl.when(i == 0)
        def _():
            zbuf[...] = jnp.zeros_like(zbuf)

            def clear_copy(row):
                return pltpu.make_async_copy(zbuf, xs_ref.at[rows(row, MOE_ROWS), :], sem)

            def clear_segment_end(e, n):
                end = pe_ref[e]
                start = jnp.where(e == 0, 0, pe_ref[jnp.maximum(e - 1, 0)])
                for back in range(1, clear_blocks + 1):
                    row = end - back * MOE_ROWS

                    @pl.when(row >= start)
                    def _():
                        clear_copy(row).start()
                    n = n + (row >= start).astype(jnp.int32)
                return n

            n_started = lax.fori_loop(0, N_EXPERTS, clear_segment_end, 0)
            n_blocks = xs_ref.shape[0] // (MOE_ROWS * PACKED_ROWS)
            n_valid = pe_ref[N_EXPERTS - 1] // MOE_ROWS

            def clear_past_end(b, carry):
                clear_copy(b * MOE_ROWS).start()
                return carry

            lax.fori_loop(n_valid, n_blocks, clear_past_end, 0)

            def wait_one(_, carry):
                clear_copy(0).wait()
                return carry

            lax.fori_loop(0, n_started + n_blocks - n_valid, wait_one, 0)

    base = 0 if ls_blocked else i * (t * TOP_K)
    slot = i % 2

    def permute_and_send(s):
        def scatter(g, carry):
            for u in range(TOKEN_UNROLL):
                tok = g * TOKEN_UNROLL + u
                token_row = h_ref[rows(tok), :]
                for k in range(TOP_K):
                    stage[s, _rows_at(ls_ref[base + k * t + tok], PACKED_ROWS), :] = token_row
            return carry

        lax.fori_loop(0, t // TOKEN_UNROLL, scatter, 0)
        _start_run_copies(
            plan_ref, t,
            lambda loc, glob, n: pltpu.make_async_copy(stage.at[s, rows(loc, n), :],
                                                       xs_ref.at[rows(glob, n), :],
                                                       sems.at[s]))

    for s in range(2):
        pl.when(slot == s)(functools.partial(permute_and_send, s))

    def wait_stage(s):
        pltpu.make_async_copy(stage.at[s], xs_ref.at[rows(0, t * TOP_K), :], sems.at[s]).wait()

    @pl.when(i > 0)
    def _():
        wait_stage(1 - slot)

    @pl.when(i == pl.num_programs(0) - 1)
    def _():
        wait_stage(slot)


def _dispatch(plan, pad_ends, ls, h2t, xs, n_rows, tile, later_rows=0):
    n_tiles = h2t.shape[0] // (tile * PACKED_ROWS)
    first = xs is None
    clear_blocks = -(-(later_rows + MOE_ROWS - 1) // MOE_ROWS) if first else 0
    ls_blocked = (tile * TOP_K) % 1024 == 0
    ls_spec = (pl.BlockSpec((tile * TOP_K,), lambda i, *_: (i,), memory_space=pltpu.SMEM)
               if ls_blocked else pl.BlockSpec(memory_space=pltpu.SMEM))
    in_specs = [pl.BlockSpec((PLAN_WORDS,), lambda i, *_: (i,), memory_space=pltpu.SMEM),
                ls_spec, pl.BlockSpec((tile * PACKED_ROWS, LANES), lambda i, *_: (i, 0))]
    args = [pad_ends, plan, ls, h2t]
    if not first:
        in_specs.append(pl.BlockSpec(memory_space=pl.ANY))
        args.append(xs)
    grid_spec = pltpu.PrefetchScalarGridSpec(
        num_scalar_prefetch=1,
        grid=(n_tiles,),
        in_specs=in_specs,
        out_specs=pl.BlockSpec(memory_space=pl.ANY),
        scratch_shapes=[pltpu.VMEM((2, tile * TOP_K * PACKED_ROWS, LANES), jnp.uint32),
                        pltpu.VMEM((MOE_ROWS * PACKED_ROWS, LANES), jnp.uint32),
                        pltpu.SemaphoreType.DMA((2,)),
                        pltpu.SemaphoreType.DMA],
    )
    return pl.pallas_call(
        functools.partial(_dispatch_kernel, clear_blocks=clear_blocks, ls_blocked=ls_blocked),
        grid_spec=grid_spec,
        out_shape=jax.ShapeDtypeStruct((n_rows * PACKED_ROWS, LANES), jnp.uint32),
        input_output_aliases={} if first else {4: 0},
        compiler_params=_params(("arbitrary",)),
        name="dispatch",
    )(*args)


def _moe_kernel(be_ref, nv_ref, nx_ref, par_ref, rv_ref, xs_ref, bgu_ref, bdn_ref, wgu_hbm, wdn_hbm,
                ys_ref, wgu_f32, wdn_f32, wgu_bf, wdn_bf, sems, *, layer):
    i = pl.program_id(0)
    e = be_ref[i]
    prev = be_ref[jnp.maximum(i - 1, 0)]
    valid = i < nv_ref[0]
    rows = 128

    def weight_copies(expert, slot):
        return (pltpu.make_async_copy(wgu_hbm.at[layer, expert], wgu_f32.at[slot], sems.at[slot]),
                pltpu.make_async_copy(wdn_hbm.at[layer, expert], wdn_f32.at[slot], sems.at[slot]))

    @pl.when(valid & ((i == 0) | (e != prev)))
    def _():
        slot = par_ref[i]

        @pl.when(i == 0)
        def _():
            for cp in weight_copies(e, slot):
                cp.start()

        for cp in weight_copies(e, slot):
            cp.wait()
        nxt = nx_ref[i]

        @pl.when(nxt < N_EXPERTS)
        def _():
            for cp in weight_copies(nxt, 1 - slot):
                cp.start()

        def cast(r, carry):
            sl = pl.ds(pl.multiple_of(r * rows, rows), rows)
            wgu_bf[sl, :] = wgu_f32[slot, sl, :].astype(BF16)
            wdn_bf[sl, :] = wdn_f32[slot, sl, :].astype(BF16)
            return carry
        lax.fori_loop(0, D_MODEL // rows, cast, 0)

    def expert_rows(n):
        x = _from_packed_rows(xs_ref.at[pl.ds(0, n * PACKED_ROWS), :], n)
        gu = jnp.dot(x, wgu_bf[...], preferred_element_type=F32) + bgu_ref[...]
        gate = jnp.minimum(gu[:, :D_FF], SWIGLU_LIMIT)
        up = jnp.clip(gu[:, D_FF:], -SWIGLU_LIMIT, SWIGLU_LIMIT)
        a = gate * jax.nn.sigmoid(SWIGLU_ALPHA * gate) * (up + 1.0)
        y = jnp.dot(a.astype(BF16), wdn_bf[...], preferred_element_type=F32) + bdn_ref[...]
        _to_row_tiles(ys_ref.at[pl.ds(0, n * ROW_TILE), :], y)

    half = MOE_ROWS // 2
    few_rows = rv_ref[i] <= half

    @pl.when(valid & jnp.logical_not(few_rows))
    def _():
        expert_rows(MOE_ROWS)

    @pl.when(valid & few_rows)
    def _():
        expert_rows(half)
        ys_ref[pl.ds(half * ROW_TILE, half * ROW_TILE), :] = jnp.zeros(
            (half * ROW_TILE, LANES), F32)

    @pl.when(jnp.logical_not(valid))
    def _():
        ys_ref[...] = jnp.zeros_like(ys_ref)


def _moe(block_e, n_valid, next_e, parity, rows_used, xs, w_gu, b_gu, w_down, b_down, layer):
    n_rows = xs.shape[0] // PACKED_ROWS
    n_blocks = n_rows // MOE_ROWS
    grid_spec = pltpu.PrefetchScalarGridSpec(
        num_scalar_prefetch=5,
        grid=(n_blocks,),
        in_specs=[
            pl.BlockSpec((MOE_ROWS * PACKED_ROWS, LANES),
                         lambda i, be, nv, *_: (jnp.minimum(i, nv[0] - 1), 0)),
            pl.BlockSpec((None, None, 1, 2 * D_FF), lambda i, be, *_: (layer, be[i], 0, 0)),
            pl.BlockSpec((None, None, 1, D_MODEL), lambda i, be, *_: (layer, be[i], 0, 0)),
            pl.BlockSpec(memory_space=pl.ANY),
            pl.BlockSpec(memory_space=pl.ANY),
        ],
        out_specs=pl.BlockSpec((MOE_ROWS * ROW_TILE, LANES), lambda i, *_: (i, 0)),
        scratch_shapes=[pltpu.VMEM((2, D_MODEL, 2 * D_FF), F32),
                        pltpu.VMEM((2, D_FF, D_MODEL), F32),
                        pltpu.VMEM((D_MODEL, 2 * D_FF), BF16),
                        pltpu.VMEM((D_FF, D_MODEL), BF16),
                        pltpu.SemaphoreType.DMA((2,))],
    )
    return pl.pallas_call(
        functools.partial(_moe_kernel, layer=layer),
        grid_spec=grid_spec,
        out_shape=jax.ShapeDtypeStruct((n_rows * ROW_TILE, LANES), F32),
        compiler_params=_params(("arbitrary",)),
        name="moe",
    )(block_e, n_valid, next_e, parity, rows_used, xs, b_gu.reshape(DEPTH, N_EXPERTS, 1, 2 * D_FF),
      b_down.reshape(DEPTH, N_EXPERTS, 1, D_MODEL), w_gu, w_down)


def _combine_kernel(plan_ref, next_plan_ref, ls_ref, gate_ref, x1_ref, gf_ref, g2_ref, b2_ref,
                    ys_ref, o_ref, stage, ft, sems, *, ls_blocked):
    i = pl.program_id(0)
    t = x1_ref.shape[0]
    slot = i % 2

    def start_fetch(tile_plan_ref, s):
        _start_run_copies(
            tile_plan_ref, t,
            lambda loc, glob, n: pltpu.make_async_copy(ys_ref.at[_tile_rows(glob, n), :],
                                                       stage.at[s, _tile_rows(loc, n), :],
                                                       sems.at[s]))

    @pl.when(i == 0)
    def _():
        start_fetch(plan_ref, 0)

    for s in range(2):
        @pl.when((i + 1 < pl.num_programs(0)) & (slot == 1 - s))
        def _(s=s):
            start_fetch(next_plan_ref, s)

    pltpu.make_async_copy(ys_ref.at[_tile_rows(0, t * TOP_K), :], stage.at[slot],
                          sems.at[slot]).wait()

    base = 0 if ls_blocked else i * (t * TOP_K)

    def gate_sum(s):
        def gather(g, carry):
            for u in range(TOKEN_UNROLL):
                tok = g * TOKEN_UNROLL + u
                acc = None
                for k in range(TOP_K):
                    j = base + k * t + tok
                    v = stage[s, _rows_at(ls_ref[j]), :] * gate_ref[j]
                    acc = v if acc is None else acc + v
                ft[_tile_rows(tok), :] = acc
            return carry

        lax.fori_loop(0, t // TOKEN_UNROLL, gather, 0)

    for s in range(2):
        pl.when(slot == s)(functools.partial(gate_sum, s))
    f = _from_row_tiles(ft, t)
    z = DEEPNORM_ALPHA * x1_ref[...] + gf_ref[...] * f
    o_ref[...] = _layer_norm(z) * g2_ref[...] + b2_ref[...]


def _combine(plan, ls, gates, x1, gf, g2, b2, ys, tile, tiles_per_seq):
    n = x1.shape[0]
    n_tiles = n // tile
    ls_blocked = (tile * TOP_K) % 1024 == 0

    def smem_vec():
        if ls_blocked:
            return pl.BlockSpec((tile * TOP_K,), lambda i, *_: (i,), memory_space=pltpu.SMEM)
        return pl.BlockSpec(memory_space=pltpu.SMEM)

    grid_spec = pltpu.PrefetchScalarGridSpec(
        num_scalar_prefetch=0,
        grid=(n_tiles,),
        in_specs=[
            pl.BlockSpec((PLAN_WORDS,), lambda i: (i,), memory_space=pltpu.SMEM),
            pl.BlockSpec((PLAN_WORDS,), lambda i: (jnp.minimum(i + 1, n_tiles - 1),),
                         memory_space=pltpu.SMEM),
            smem_vec(), smem_vec(),
            pl.BlockSpec((tile, D_MODEL), lambda i, *_: (i, 0)),
            pl.BlockSpec((None, 1, D_MODEL), lambda i, *_: (i // tiles_per_seq, 0, 0)),
            pl.BlockSpec((1, D_MODEL), lambda i, *_: (0, 0)),
            pl.BlockSpec((1, D_MODEL), lambda i, *_: (0, 0)),
            pl.BlockSpec(memory_space=pl.ANY),
        ],
        out_specs=pl.BlockSpec((tile, D_MODEL), lambda i, *_: (i, 0)),
        scratch_shapes=[pltpu.VMEM((2, tile * TOP_K * ROW_TILE, LANES), F32),
                        pltpu.VMEM((tile * ROW_TILE, LANES), F32),
                        pltpu.SemaphoreType.DMA((2,))],
    )
    return pl.pallas_call(
        functools.partial(_combine_kernel, ls_blocked=ls_blocked),
        grid_spec=grid_spec,
        out_shape=jax.ShapeDtypeStruct((n, D_MODEL), F32),
        compiler_params=_params(("arbitrary",)),
        name="combine",
    )(plan, plan, ls, gates, x1, gf, g2, b2, ys)


def _alibi_slopes():
    return jnp.exp2(-ALIBI_MAX * (jnp.arange(N_HEADS, dtype=F32) + 1.0) / N_HEADS)


def _prompt_bias(slopes):
    r = jnp.arange(Q_TILE)
    j = jnp.arange(WINDOW + Q_TILE)
    dc = (WINDOW + r)[:, None] // CHUNK - j[None, :] // CHUNK
    vis = (dc >= 0) & (dc <= WIN_CHUNKS)
    dist = jnp.abs(r[:, None] + WINDOW - j[None, :]).astype(F32)
    bias = jnp.where(vis[None], -slopes[:, None, None] * dist[None] * LOG2_E, NEG_INF)
    return jnp.stack([bias, jnp.where(j[None, None, :] < WINDOW, NEG_INF, bias)])


def _sample_bias(slopes, t, hist):
    qpos = PAST_LEN + jnp.arange(t)
    kpos = PAST_LEN - hist + jnp.arange(hist + t)
    dc = qpos[:, None] // CHUNK - kpos[None, :] // CHUNK
    vis = (dc >= 0) & (dc <= WIN_CHUNKS) & (kpos[None, :] >= 0)
    dist = jnp.abs(qpos[:, None] - kpos[None, :]).astype(F32)
    return jnp.where(vis[None], -slopes[:, None, None] * dist[None] * LOG2_E, NEG_INF)[None]


def _proj_weights(w_in, b_in):
    col_scale = jnp.where(jnp.arange(PROJ_COLS) < ATT_DIM, ATTN_SCALE * LOG2_E, 1.0).astype(F32)
    return (w_in * col_scale).astype(BF16), (b_in * col_scale).reshape(1, PROJ_COLS)


def _pool_weight(w_pool):
    n = len(POOL_WINDOWS)
    eye = jnp.eye(n, dtype=w_pool.dtype)
    return jnp.einsum('gcd,gh->gchd', w_pool, eye).reshape(POOL_DIM, POOL_DIM).astype(BF16)


def _split_kv(kv, nb, rows):
    tail = kv[:, -rows:].astype(F32)
    shape = (nb, rows, N_KV, HEAD_DIM)
    return tail[..., :KV_DIM].reshape(shape), tail[..., KV_DIM:].reshape(shape)


def kernel(x_prompt, x_sample, c_prompt, c_sample, cache_k, cache_v, state_pool, w_ada, b_ada,
           w_in, b_in, sinks, w_pool, pool_scale, w_out, b_out, ln1_g, ln1_b, ln2_g, ln2_b,
           w_router, b_router, w_gu, b_gu, w_down, b_down):
    nbp, seq, _ = x_prompt.shape
    nbs, tdec, _ = x_sample.shape
    hist = cache_k.shape[2]
    n_p, n_s = nbp * seq, nbs * tdec
    n_asg = (n_p + n_s) * TOP_K
    n_blocks = -(-(n_asg + N_EXPERTS * (MOE_ROWS - 1)) // MOE_ROWS)
    n_rows = n_blocks * MOE_ROWS
    assert hist == WINDOW and tdec >= POOL_PAD and seq % TOK_TILE == 0

    nb_all = nbp + nbs
    nb_pad = -(-nb_all // SUBLANES) * SUBLANES
    c_all = jnp.concatenate([c_prompt, c_sample, jnp.zeros((nb_pad - nb_all, D_MODEL), F32)], 0)
    mod_all = _adaln(c_all, w_ada, b_ada).reshape(DEPTH, nb_pad, 6, D_MODEL)

    slopes = _alibi_slopes()
    bias_p = _prompt_bias(slopes)
    bias_s = _sample_bias(slopes, tdec, hist)
    row = lambda v: v.reshape(1, -1)

    yp, ys_tok = x_prompt, x_sample
    outs = {k: [] for k in ("kp", "vp", "pp", "ks", "vs", "ps")}
    for l in range(DEPTH):
        mod = mod_all[l]
        w_proj, b_proj = _proj_weights(w_in[l], b_in[l])
        wpool = _pool_weight(w_pool[l])
        pscale = row(pool_scale[l])
        wout = w_out[l].astype(BF16)
        wr_t = w_router[l].T
        wr_hi = wr_t.astype(BF16)
        wr = jnp.concatenate([wr_hi, (wr_t - wr_hi.astype(F32)).astype(BF16)], axis=0)
        br = b_router[l].reshape(N_EXPERTS, 1)

        q, kv, u = _inproj(yp, mod, w_proj, b_proj, INPROJ_TILE, 0)
        sinks2 = sinks[l] * LOG2_E
        mix_p = _mixer(sinks2, q, kv, u, None, None, bias_p, wpool, pscale,
                       tile=TOK_TILE, tq=Q_TILE, history_starts_empty=True, pos0=0)
        k_tail, v_tail = _split_kv(kv, nbp, WINDOW)
        outs["kp"].append(k_tail)
        outs["vp"].append(v_tail)
        outs["pp"].append(u[:, -POOL_PAD:])

        qs, kvs, us = _inproj(ys_tok, mod, w_proj, b_proj, tdec, nbp)
        cache_kv = jnp.concatenate([cache_k[l].reshape(nbs, hist, KV_DIM),
                                    cache_v[l].reshape(nbs, hist, KV_DIM)], axis=-1).astype(BF16)
        sp = jnp.pad(state_pool[l], ((0, 0), (POOL_HALO - POOL_PAD, 0), (0, 0)))
        mix_s = _mixer(sinks2, qs, kvs, us, cache_kv, sp, bias_s, wpool, pscale,
                       tile=tdec, tq=tdec, history_starts_empty=False, pos0=PAST_LEN)
        k_new, v_new = _split_kv(kvs, nbs, tdec)
        outs["ks"].append(jnp.concatenate([cache_k[l], k_new], axis=1)[:, -hist:])
        outs["vs"].append(jnp.concatenate([cache_v[l], v_new], axis=1)[:, -hist:])
        outs["ps"].append(us[:, -POOL_PAD:])

        post_w = (wout, row(b_out[l]), row(ln1_g[l]), row(ln1_b[l]), wr, br)
        zero_counts = jnp.zeros((N_EXPERTS, LANES), F32)
        x1p, h2tp, rlp, rgp, meta_p, cnt_p = _post(mix_p, yp, mod, *post_w, zero_counts, TOK_TILE, 0)
        x1s, h2ts, rls, rgs, meta_s, cnt = _post(mix_s, ys_tok, mod, *post_w, cnt_p, tdec, nbp)

        counts = cnt[:, 0].astype(jnp.int32)
        padded = (counts + MOE_ROWS - 1) // MOE_ROWS * MOE_ROWS
        pad_ends = jnp.cumsum(padded).astype(jnp.int32)
        pad_starts = pad_ends - padded
        block_start = jnp.arange(n_blocks, dtype=jnp.int32) * MOE_ROWS
        block_e = jnp.minimum(jnp.sum(pad_ends[None, :] <= block_start[:, None], axis=1),
                              N_EXPERTS - 1).astype(jnp.int32)
        n_valid = (pad_ends[-1:] // MOE_ROWS).astype(jnp.int32)
        ids = jnp.arange(N_EXPERTS, dtype=jnp.int32)
        present = padded > 0
        later = (ids[None, :] > ids[:, None]) & present[None, :]
        next_present = jnp.min(jnp.where(later, ids[None, :], N_EXPERTS), axis=1)
        ordinal = jnp.cumsum(present.astype(jnp.int32)) - 1
        is_block_e = block_e[:, None] == ids[None, :]
        next_e = jnp.sum(jnp.where(is_block_e, next_present[None, :], 0), axis=1).astype(jnp.int32)
        parity = jnp.sum(jnp.where(is_block_e, ordinal[None, :] % 2, 0), axis=1).astype(jnp.int32)
        seg_used_end = jnp.sum(jnp.where(is_block_e, (pad_starts + counts)[None, :], 0), axis=1)
        rows_used = jnp.clip(seg_used_end - block_start, 0, MOE_ROWS).astype(jnp.int32)

        def copy_plan(meta, tile):
            m = meta.reshape(-1, N_EXPERTS, LANES)
            return _copy_plan(m[:, :, 0], m[:, :, 1], pad_starts[None, :] + m[:, :, 2],
                              tile.bit_length())

        def per_assignment(r, n, first=0):
            return r[:, :, first:first + TOP_K, :].reshape(n * TOP_K)

        plan_p, plan_s = copy_plan(meta_p, TOK_TILE), copy_plan(meta_s, tdec)
        ls_p, ls_s = per_assignment(rlp, n_p), per_assignment(rls, n_s)

        xs = _dispatch(plan_p, pad_ends, per_assignment(rlp, n_p, TOP_K),
                       h2tp.reshape(n_p * PACKED_ROWS, LANES), None, n_rows, TOK_TILE,
                       later_rows=n_s)
        xs = _dispatch(plan_s, pad_ends, per_assignment(rls, n_s, TOP_K),
                       h2ts.reshape(n_s * PACKED_ROWS, LANES), xs, n_rows, tdec)
        ye = _moe(block_e, n_valid, next_e, parity, rows_used, xs, w_gu, b_gu, w_down, b_down, l)

        g2, b2 = row(ln2_g[l]), row(ln2_b[l])
        gf = mod[:, 5].reshape(nb_pad, 1, D_MODEL)
        yp = _combine(plan_p, ls_p, per_assignment(rgp, n_p), x1p.reshape(n_p, D_MODEL),
                      gf[:nbp], g2, b2, ye, TOK_TILE, seq // TOK_TILE).reshape(nbp, seq, D_MODEL)
        ys_tok = _combine(plan_s, ls_s, per_assignment(rgs, n_s), x1s.reshape(n_s, D_MODEL),
                          gf[nbp:nb_all], g2, b2, ye, tdec, 1).reshape(nbs, tdec, D_MODEL)

    st = lambda k: jnp.stack(outs[k])
    return (yp, ys_tok, st("kp"), st("vp"), st("pp"), st("ks"), st("vs"), st("ps"))
```

```python
import functools

import jax
import jax.numpy as jnp
from jax import lax
from jax.experimental import pallas as pl
from jax.experimental.pallas import tpu as pltpu

F32 = jnp.float32
BF16 = jnp.bfloat16

D_MODEL = 1024
DEPTH = 2
CHUNK = 64
WINDOW = 128
WIN_CHUNKS = WINDOW // CHUNK
HEAD_DIM = 64
ATT_DIM = 768
N_HEADS = 12
N_KV = 4
GROUP = N_HEADS // N_KV
KV_DIM = N_KV * HEAD_DIM
ATTN_SCALE = HEAD_DIM ** -0.5
LOG2_E = 1.4426950408889634
ALIBI_MAX = 8.0
NEG_INF = -1e30
POOL_DIM = D_MODEL - ATT_DIM
POOL_WINDOWS = (2, 4, 8, 16)
POOL_CH = POOL_DIM // len(POOL_WINDOWS)
POOL_PAD = max(POOL_WINDOWS) - 1
N_EXPERTS = 32
TOP_K = 4
D_FF = D_MODEL
SWIGLU_LIMIT = 7.0
SWIGLU_ALPHA = 1.702
LN_EPS = 1e-5
DEEPNORM_ALPHA = (2.0 * DEPTH) ** 0.25
PAST_LEN = 1024

LANES = 128
SUBLANES = 8
VMEM_LIMIT = 52 * 1024 * 1024

TOK_TILE = 512
Q_TILE = 128
POOL_HALO = 16
KD_DIM = N_KV * LANES
PROJ_COLS = ATT_DIM + 2 * KV_DIM + POOL_DIM
INPROJ_TILE = 1024
MOE_ROWS = 512
TOKEN_UNROLL = 32
PLAN_BITS = TOK_TILE.bit_length()
PLAN_WORDS = 1024
assert 2 * PLAN_BITS * N_EXPERTS + PLAN_BITS <= PLAN_WORDS
ROW_TILE = D_MODEL // LANES
assert ROW_TILE == SUBLANES
PACKED_ROWS = ROW_TILE // 2


def _params(sem, vmem=VMEM_LIMIT):
    return pltpu.CompilerParams(dimension_semantics=sem, vmem_limit_bytes=vmem)


def _layer_norm(x):
    mu = jnp.mean(x, axis=-1, keepdims=True)
    xc = x - mu
    var = jnp.mean(xc * xc, axis=-1, keepdims=True)
    return xc * lax.rsqrt(var + LN_EPS)


def _adaln_kernel(c_ref, w_ref, b_ref, o_ref):
    c = c_ref[...]
    s = c * jax.nn.sigmoid(c)
    o_ref[...] = jnp.dot(s, w_ref[...], preferred_element_type=F32,
                         precision=lax.Precision.HIGHEST) + b_ref[...]


def _adaln(c_all, w_ada, b_ada):
    nb = c_all.shape[0]
    ncol = 6 * D_MODEL
    tn = D_MODEL
    return pl.pallas_call(
        _adaln_kernel,
        grid=(DEPTH, ncol // tn),
        in_specs=[
            pl.BlockSpec((nb, D_MODEL), lambda l, j: (0, 0)),
            pl.BlockSpec((None, D_MODEL, tn), lambda l, j: (l, 0, j)),
            pl.BlockSpec((None, 1, tn), lambda l, j: (l, 0, j)),
        ],
        out_specs=pl.BlockSpec((None, nb, tn), lambda l, j: (l, 0, j)),
        out_shape=jax.ShapeDtypeStruct((DEPTH, nb, ncol), F32),
        compiler_params=_params(("parallel", "parallel")),
        name="adaln",
    )(c_all, w_ada, b_ada.reshape(DEPTH, 1, ncol))


def _inproj_kernel(x_ref, mod_ref, w_ref, b_ref, q_ref, kv_ref, u_ref):
    h = _layer_norm(x_ref[...]) * (1.0 + mod_ref[1:2, :]) + mod_ref[0:1, :]
    p = jnp.dot(h.astype(BF16), w_ref[...], preferred_element_type=F32) + b_ref[...]
    q_ref[...] = p[:, :ATT_DIM].astype(BF16)
    kv_ref[...] = p[:, ATT_DIM:ATT_DIM + 2 * KV_DIM].astype(BF16)
    u_ref[...] = p[:, ATT_DIM + 2 * KV_DIM:]


def _inproj(x, mod, w, b, tile, mod_row0):
    nb, s, _ = x.shape
    nt = s // tile

    def tok(c):
        return pl.BlockSpec((None, tile, c), lambda bi, i: (bi, i, 0))

    return pl.pallas_call(
        _inproj_kernel,
        grid=(nb, nt),
        in_specs=[
            tok(D_MODEL),
            pl.BlockSpec((None, 6, D_MODEL), lambda bi, i: (bi + mod_row0, 0, 0)),
            pl.BlockSpec((D_MODEL, PROJ_COLS), lambda bi, i: (0, 0)),
            pl.BlockSpec((1, PROJ_COLS), lambda bi, i: (0, 0)),
        ],
        out_specs=[tok(ATT_DIM), tok(2 * KV_DIM), tok(POOL_DIM)],
        out_shape=[
            jax.ShapeDtypeStruct((nb, s, ATT_DIM), BF16),
            jax.ShapeDtypeStruct((nb, s, 2 * KV_DIM), BF16),
            jax.ShapeDtypeStruct((nb, s, POOL_DIM), F32),
        ],
        compiler_params=_params(("parallel", "parallel")),
        name="inproj",
    )(x, mod, w, b)


def _dup_heads(x):
    r = x.shape[0]
    low = lax.broadcasted_iota(jnp.int32, (r // 2, LANES), 1) < HEAD_DIM
    parts = []
    for g in range(KV_DIM // LANES):
        pair = pltpu.bitcast(x[:, g * LANES:(g + 1) * LANES], jnp.uint32)
        swapped = pltpu.roll(pair, HEAD_DIM, 1)
        parts += [jnp.where(low, pair, swapped), jnp.where(low, swapped, pair)]
    return pltpu.bitcast(jnp.concatenate(parts, axis=1), BF16)


def _mixer_kernel(sinks_ref, q_ref, kv_ref, kvh_ref, u_ref, uh_ref,
                  bias_ref, wpool_ref, pscale_ref, o_ref, kall_ref, vall_ref,
                  *, tq, history_starts_empty, pos0):
    i = pl.program_id(1)
    t = q_ref.shape[0]
    hk = kvh_ref.shape[0]
    nk = hk + tq
    @pl.when(i >= 0)
    def _():
        kall_ref[0:hk, :] = _dup_heads(kvh_ref[:, :KV_DIM])
        kall_ref[hk:, :] = _dup_heads(kv_ref[:, :KV_DIM])
        vall_ref[0:hk, :] = _dup_heads(kvh_ref[:, KV_DIM:])
        vall_ref[hk:, :] = _dup_heads(kv_ref[:, KV_DIM:])

    low_half = lax.broadcasted_iota(jnp.int32, (tq, LANES), 1) < HEAD_DIM
    for s in range(t // tq):
        rows = slice(s * tq, (s + 1) * tq)
        keys = slice(s * tq, s * tq + nk)
        table = (i == 0).astype(jnp.int32) if (history_starts_empty and s == 0) else 0
        for pair in range(N_HEADS // 2):
            q2 = q_ref[rows, pair * LANES:(pair + 1) * LANES]
            halves = []
            for half in range(2):
                head = 2 * pair + half
                kv = head // GROUP
                qm = jnp.where(low_half if half == 0 else ~low_half, q2, jnp.zeros_like(q2))
                kd = kall_ref[keys, kv * LANES:(kv + 1) * LANES]
                sc = lax.dot_general(qm, kd, (((1,), (1,)), ((), ())),
                                     preferred_element_type=F32)
                sc = sc + bias_ref[table, head]
                sink = sinks_ref[head]
                m = jnp.maximum(jnp.max(sc, axis=1, keepdims=True), sink)
                p = jnp.exp2(sc - m)
                den = jnp.sum(p, axis=1, keepdims=True) + jnp.exp2(sink - m)
                vd = vall_ref[keys, kv * LANES:(kv + 1) * LANES]
                o2 = jnp.dot(p.astype(BF16), vd, preferred_element_type=F32)
                halves.append(o2 / den)
            o_ref[rows, pair * LANES:(pair + 1) * LANES] = jnp.where(
                low_half, halves[0], halves[1]).astype(BF16)

    uh = uh_ref[...]
    if history_starts_empty:
        uh = jnp.where(i == 0, 0.0, uh)
    u = u_ref[...]
    ue = jnp.concatenate([uh, u], axis=0)
    s2 = ue + pltpu.roll(ue, 1, 0)
    s4 = s2 + pltpu.roll(s2, 2, 0)
    s8 = s4 + pltpu.roll(s4, 4, 0)
    s16 = s8 + pltpu.roll(s8, 8, 0)
    lane = lax.broadcasted_iota(jnp.int32, (1, POOL_DIM), 1)
    g0, g1, g2 = lane < POOL_CH, lane < 2 * POOL_CH, lane < 3 * POOL_CH
    wsum = jnp.where(g0, s2[POOL_HALO:], jnp.where(g1, s4[POOL_HALO:],
                     jnp.where(g2, s8[POOL_HALO:], s16[POOL_HALO:])))
    width = jnp.where(g0, 2.0, jnp.where(g1, 4.0, jnp.where(g2, 8.0, 16.0))).astype(F32)
    pos = (pos0 + i * t + lax.broadcasted_iota(jnp.int32, (t, 1), 0)).astype(F32)
    cnt = jnp.minimum(width, pos + 1.0)
    pooled = wsum / cnt - u
    mixed = jnp.dot(pooled.astype(BF16), wpool_ref[...], preferred_element_type=F32)
    o_ref[:, ATT_DIM:] = (mixed * pscale_ref[...]).astype(BF16)


def _mixer(sinks, q, kv, u, kv_hist, u_hist, bias, wpool, pscale, *,
           tile, tq, history_starts_empty, pos0):
    nb, s, _ = q.shape
    nt = s // tile
    hk = WINDOW
    own_history = kv_hist is None
    if own_history:
        kv_hist, u_hist = kv, u
        kh_map = lambda bi, i: (bi, jnp.maximum(i * (tile // hk) - 1, 0), 0)
        uh_map = lambda bi, i: (bi, jnp.maximum(i * (tile // POOL_HALO) - 1, 0), 0)
    else:
        kh_map = lambda bi, i: (bi, 0, 0)
        uh_map = lambda bi, i: (bi, 0, 0)

    def tok(c):
        return pl.BlockSpec((None, tile, c), lambda bi, i: (bi, i, 0))

    kern = functools.partial(_mixer_kernel, tq=tq,
                             history_starts_empty=history_starts_empty, pos0=pos0)
    return pl.pallas_call(
        kern,
        grid=(nb, nt),
        in_specs=[
            pl.BlockSpec(memory_space=pltpu.SMEM),
            tok(ATT_DIM),
            tok(2 * KV_DIM),
            pl.BlockSpec((None, hk, 2 * KV_DIM), kh_map),
            tok(POOL_DIM),
            pl.BlockSpec((None, POOL_HALO, POOL_DIM), uh_map),
            pl.BlockSpec(bias.shape, lambda bi, i: (0, 0, 0, 0)),
            pl.BlockSpec((POOL_DIM, POOL_DIM), lambda bi, i: (0, 0)),
            pl.BlockSpec((1, POOL_DIM), lambda bi, i: (0, 0)),
        ],
        out_specs=tok(D_MODEL),
        out_shape=jax.ShapeDtypeStruct((nb, s, D_MODEL), BF16),
        scratch_shapes=[pltpu.VMEM((hk + tile, KD_DIM), BF16),
                        pltpu.VMEM((hk + tile, KD_DIM), BF16)],
        compiler_params=_params(("parallel", "parallel")),
        name="mixer",
    )(sinks, q, kv, kv_hist, u, u_hist, bias, wpool, pscale)


def _to_row_tiles(ref, x):
    t = x.shape[0]
    for s in range(ROW_TILE):
        ref[pl.ds(s, t, stride=ROW_TILE), :] = x[:, s * LANES:(s + 1) * LANES]


def _from_row_tiles(ref, t):
    return jnp.concatenate([ref[pl.ds(s, t, stride=ROW_TILE), :] for s in range(ROW_TILE)], axis=1)


_HIGH_HALF = 0xFFFF0000


def _to_packed_rows(ref, x):
    t = x.shape[0]
    half = D_MODEL // 2
    as_bits = lambda v: pltpu.bitcast(v.astype(BF16).astype(F32), jnp.uint32)
    words = (as_bits(x[:, :half]) >> 16) | (as_bits(x[:, half:]) & jnp.uint32(_HIGH_HALF))
    for s in range(PACKED_ROWS):
        ref[pl.ds(s, t, stride=PACKED_ROWS), :] = words[:, s * LANES:(s + 1) * LANES]


def _from_packed_rows(ref, t):
    words = [ref[pl.ds(s, t, stride=PACKED_ROWS), :] for s in range(PACKED_ROWS)]
    low = [pltpu.bitcast(w << 16, F32) for w in words]
    high = [pltpu.bitcast(w & jnp.uint32(_HIGH_HALF), F32) for w in words]
    return jnp.concatenate(low + high, axis=1).astype(BF16)


def _post_kernel(mix_ref, x_ref, mod_ref, wout_ref, bout_ref, g1_ref, b1_ref, wr_ref, br_ref,
                 cin_ref, x1_ref, h2t_ref, rl_ref, rg_ref, meta_ref, cnt_ref):
    first = (pl.program_id(0) == 0) & (pl.program_id(1) == 0)
    t = x_ref.shape[0]

    @pl.when(first)
    def _():
        cnt_ref[...] = cin_ref[...]

    mix = jnp.dot(mix_ref[...], wout_ref[...], preferred_element_type=F32) + bout_ref[...]
    z = DEEPNORM_ALPHA * x_ref[...] + mod_ref[2:3, :] * mix
    x1 = _layer_norm(z) * g1_ref[...] + b1_ref[...]
    x1_ref[...] = x1
    h2 = _layer_norm(x1) * (1.0 + mod_ref[4:5, :]) + mod_ref[3:4, :]
    _to_packed_rows(h2t_ref, h2)

    h2_hi = h2.astype(BF16)
    h2_lo = (h2 - h2_hi.astype(F32)).astype(BF16)
    contract_last = (((1,), (1,)), ((), ()))
    d_hi = lax.dot_general(wr_ref[...], h2_hi, contract_last, preferred_element_type=F32)
    d_lo = lax.dot_general(wr_ref[0:N_EXPERTS, :], h2_lo, contract_last,
                           preferred_element_type=F32)
    logits = d_hi[:N_EXPERTS] + d_hi[N_EXPERTS:] + d_lo + br_ref[...]
    expert = lax.broadcasted_iota(jnp.int32, (N_EXPERTS, t), 0).astype(F32)
    vals, hots = [], []
    cur = logits
    for _ in range(TOP_K):
        mk = jnp.max(cur, axis=0, keepdims=True)
        ik = jnp.min(jnp.where(cur == mk, expert, float(N_EXPERTS)), axis=0, keepdims=True)
        hot = expert == ik
        cur = jnp.where(hot, -jnp.inf, cur)
        vals.append(mk)
        hots.append(hot)
    exps = [jnp.exp(v - vals[0]) for v in vals]
    den = exps[0] + exps[1] + exps[2] + exps[3]
    gates = [e / den for e in exps]

    onehot = jnp.zeros((N_EXPERTS, t), F32)
    for hot in hots:
        onehot = onehot + hot.astype(F32)
    onehot_bf = onehot.astype(BF16)
    tok_r = lax.broadcasted_iota(jnp.int32, (t, t), 0)
    tok_c = lax.broadcasted_iota(jnp.int32, (t, t), 1)
    earlier = jnp.where(tok_r < tok_c, 1.0, 0.0).astype(BF16)
    rank = jnp.dot(onehot_bf, earlier, preferred_element_type=F32)
    exp_r = lax.broadcasted_iota(jnp.int32, (N_EXPERTS, N_EXPERTS), 0)
    exp_c = lax.broadcasted_iota(jnp.int32, (N_EXPERTS, N_EXPERTS), 1)
    lower = jnp.where(exp_c < exp_r, 1.0, 0.0).astype(BF16)
    tile_off = jnp.sum(jnp.dot(lower, onehot_bf, preferred_element_type=F32),
                       axis=1, keepdims=True)
    tile_cnt = jnp.sum(onehot, axis=1, keepdims=True)
    local = rank + tile_off
    slots = [jnp.sum(jnp.where(hot, local, 0.0), axis=0, keepdims=True) for hot in hots]
    rl_ref[...] = jnp.concatenate([v * float(ROW_TILE) for v in slots] +
                                  [v * float(PACKED_ROWS) for v in slots],
                                  axis=0).astype(jnp.int32)
    rg_ref[...] = jnp.concatenate(gates + [jnp.zeros_like(g) for g in gates], axis=0)
    lane = lax.broadcasted_iota(jnp.int32, (N_EXPERTS, LANES), 1)
    before = cnt_ref[...]
    meta = jnp.where(lane == 0, tile_cnt, jnp.where(lane == 1, tile_off,
                     jnp.where(lane == 2, before, 0.0)))
    meta_ref[...] = meta.astype(jnp.int32)
    cnt_ref[...] = before + tile_cnt


def _post(mix, x, mod, wout, bout, g1, b1, wr, br, counts_in, tile, mod_row0):
    nb, s, _ = x.shape
    nt = s // tile

    def tok(c):
        return pl.BlockSpec((None, tile, c), lambda bi, i: (bi, i, 0))

    def whole(shape):
        return pl.BlockSpec(shape, lambda bi, i: tuple(0 for _ in shape))

    return pl.pallas_call(
        _post_kernel,
        grid=(nb, nt),
        in_specs=[
            tok(D_MODEL), tok(D_MODEL),
            pl.BlockSpec((None, 6, D_MODEL), lambda bi, i: (bi + mod_row0, 0, 0)),
            whole((D_MODEL, D_MODEL)), whole((1, D_MODEL)),
            whole((1, D_MODEL)), whole((1, D_MODEL)),
            whole((2 * N_EXPERTS, D_MODEL)), whole((N_EXPERTS, 1)),
            whole((N_EXPERTS, LANES)),
        ],
        out_specs=[tok(D_MODEL),
                   pl.BlockSpec((None, tile * PACKED_ROWS, LANES), lambda bi, i: (bi, i, 0)),
                   pl.BlockSpec((None, None, 2 * TOP_K, tile), lambda bi, i: (bi, i, 0, 0)),
                   pl.BlockSpec((None, None, 2 * TOP_K, tile), lambda bi, i: (bi, i, 0, 0)),
                   pl.BlockSpec((None, None, N_EXPERTS, LANES), lambda bi, i: (bi, i, 0, 0)),
                   whole((N_EXPERTS, LANES))],
        out_shape=[
            jax.ShapeDtypeStruct((nb, s, D_MODEL), F32),
            jax.ShapeDtypeStruct((nb, s * PACKED_ROWS, LANES), jnp.uint32),
            jax.ShapeDtypeStruct((nb, nt, 2 * TOP_K, tile), jnp.int32),
            jax.ShapeDtypeStruct((nb, nt, 2 * TOP_K, tile), F32),
            jax.ShapeDtypeStruct((nb, nt, N_EXPERTS, LANES), jnp.int32),
            jax.ShapeDtypeStruct((N_EXPERTS, LANES), F32),
        ],
        compiler_params=_params(("arbitrary", "arbitrary")),
        name="post",
    )(mix, x, mod, wout, bout, g1, b1, wr, br, counts_in)


def _tile_rows(row, n=1, per=ROW_TILE):
    return pl.ds(pl.multiple_of(row * per, per), n * per)


def _rows_at(first_row, per=ROW_TILE):
    return pl.ds(pl.multiple_of(first_row, per), per)


def _copy_plan(cnt, off, dst, n_bits):
    n_tiles = cnt.shape[0]
    bits = jnp.arange(PLAN_BITS, dtype=jnp.int32)[None, :, None]
    c = cnt[:, None, :]
    valid = ((c >> bits) & 1) * (bits < n_bits)
    done = c & ((1 << bits) - 1)
    ids = jnp.arange(N_EXPERTS, dtype=jnp.int32)
    pos = jnp.sum(jnp.where(ids[None, :] < ids[:, None], valid[..., None, :], 0), axis=-1)
    place = (pos[..., :, None] == ids) & (valid[..., :, None] == 1)
    pack = lambda v: jnp.sum(jnp.where(place, v[..., :, None], 0), axis=-2)
    local = pack(off[:, None, :] + done)
    glob = pack(dst[:, None, :] + done)
    n = jnp.sum(valid, axis=-1)
    used = 2 * PLAN_BITS * N_EXPERTS + PLAN_BITS
    plan = jnp.concatenate([local.reshape(n_tiles, -1), glob.reshape(n_tiles, -1), n,
                            jnp.zeros((n_tiles, PLAN_WORDS - used), jnp.int32)], axis=1)
    return plan.reshape(-1).astype(jnp.int32)


def _start_run_copies(plan_ref, t, make_copy):
    for b in range(t.bit_length()):
        def piece(r, carry, b=b):
            j = b * N_EXPERTS + r
            make_copy(plan_ref[j], plan_ref[PLAN_BITS * N_EXPERTS + j], 1 << b).start()
            return carry

        lax.fori_loop(0, plan_ref[2 * PLAN_BITS * N_EXPERTS + b], piece, 0)


def _dispatch_kernel(pe_ref, plan_ref, ls_ref, h_ref, *rest, clear_blocks, ls_blocked):
    xs_ref, stage, zbuf, sems, sem = rest[-5:]
    i = pl.program_id(0)
    t = h_ref.shape[0] // PACKED_ROWS
    rows = functools.partial(_tile_rows, per=PACKED_ROWS)

    if clear_blocks:
        @pl.when(i == 0)
        def _():
            zbuf[...] = jnp.zeros_like(zbuf)

            def clear_copy(row):
                return pltpu.make_async_copy(zbuf, xs_ref.at[rows(row, MOE_ROWS), :], sem)

            def clear_segment_end(e, n):
                end = pe_ref[e]
                start = jnp.where(e == 0, 0, pe_ref[jnp.maximum(e - 1, 0)])
                for back in range(1, clear_blocks + 1):
                    row = end - back * MOE_ROWS

                    @pl.when(row >= start)
                    def _():
                        clear_copy(row).start()
                    n = n + (row >= start).astype(jnp.int32)
                return n

            n_started = lax.fori_loop(0, N_EXPERTS, clear_segment_end, 0)
            n_blocks = xs_ref.shape[0] // (MOE_ROWS * PACKED_ROWS)
            n_valid = pe_ref[N_EXPERTS - 1] // MOE_ROWS

            def clear_past_end(b, carry):
                clear_copy(b * MOE_ROWS).start()
                return carry

            lax.fori_loop(n_valid, n_blocks, clear_past_end, 0)

            def wait_one(_, carry):
                clear_copy(0).wait()
                return carry

            lax.fori_loop(0, n_started + n_blocks - n_valid, wait_one, 0)

    base = 0 if ls_blocked else i * (t * TOP_K)
    slot = i % 2

    def permute_and_send(s):
        def scatter(g, carry):
            for u in range(TOKEN_UNROLL):
                tok = g * TOKEN_UNROLL + u
                token_row = h_ref[rows(tok), :]
                for k in range(TOP_K):
                    stage[s, _rows_at(ls_ref[base + k * t + tok], PACKED_ROWS), :] = token_row
            return carry

        lax.fori_loop(0, t // TOKEN_UNROLL, scatter, 0)
        _start_run_copies(
            plan_ref, t,
            lambda loc, glob, n: pltpu.make_async_copy(stage.at[s, rows(loc, n), :],
                                                       xs_ref.at[rows(glob, n), :],
                                                       sems.at[s]))

    for s in range(2):
        pl.when(slot == s)(functools.partial(permute_and_send, s))

    def wait_stage(s):
        pltpu.make_async_copy(stage.at[s], xs_ref.at[rows(0, t * TOP_K), :], sems.at[s]).wait()

    @pl.when(i > 0)
    def _():
        wait_stage(1 - slot)

    @pl.when(i == pl.num_programs(0) - 1)
    def _():
        wait_stage(slot)


def _dispatch(plan, pad_ends, ls, h2t, xs, n_rows, tile, later_rows=0):
    n_tiles = h2t.shape[0] // (tile * PACKED_ROWS)
    first = xs is None
    clear_blocks = -(-(later_rows + MOE_ROWS - 1) // MOE_ROWS) if first else 0
    ls_blocked = (tile * TOP_K) % 1024 == 0
    ls_spec = (pl.BlockSpec((tile * TOP_K,), lambda i, *_: (i,), memory_space=pltpu.SMEM)
               if ls_blocked else pl.BlockSpec(memory_space=pltpu.SMEM))
    in_specs = [pl.BlockSpec((PLAN_WORDS,), lambda i, *_: (i,), memory_space=pltpu.SMEM),
                ls_spec, pl.BlockSpec((tile * PACKED_ROWS, LANES), lambda i, *_: (i, 0))]
    args = [pad_ends, plan, ls, h2t]
    if not first:
        in_specs.append(pl.BlockSpec(memory_space=pl.ANY))
        args.append(xs)
    grid_spec = pltpu.PrefetchScalarGridSpec(
        num_scalar_prefetch=1,
        grid=(n_tiles,),
        in_specs=in_specs,
        out_specs=pl.BlockSpec(memory_space=pl.ANY),
        scratch_shapes=[pltpu.VMEM((2, tile * TOP_K * PACKED_ROWS, LANES), jnp.uint32),
                        pltpu.VMEM((MOE_ROWS * PACKED_ROWS, LANES), jnp.uint32),
                        pltpu.SemaphoreType.DMA((2,)),
                        pltpu.SemaphoreType.DMA],
    )
    return pl.pallas_call(
        functools.partial(_dispatch_kernel, clear_blocks=clear_blocks, ls_blocked=ls_blocked),
        grid_spec=grid_spec,
        out_shape=jax.ShapeDtypeStruct((n_rows * PACKED_ROWS, LANES), jnp.uint32),
        input_output_aliases={} if first else {4: 0},
        compiler_params=_params(("arbitrary",)),
        name="dispatch",
    )(*args)


def _moe_kernel(be_ref, nv_ref, nx_ref, par_ref, rv_ref, xs_ref, bgu_ref, bdn_ref, wgu_hbm, wdn_hbm,
                ys_ref, wgu_f32, wdn_f32, wgu_bf, wdn_bf, sems, *, layer):
    i = pl.program_id(0)
    e = be_ref[i]
    prev = be_ref[jnp.maximum(i - 1, 0)]
    valid = i < nv_ref[0]
    rows = 128

    def weight_copies(expert, slot):
        return (pltpu.make_async_copy(wgu_hbm.at[layer, expert], wgu_f32.at[slot], sems.at[slot]),
                pltpu.make_async_copy(wdn_hbm.at[layer, expert], wdn_f32.at[slot], sems.at[slot]))

    @pl.when(valid & ((i == 0) | (e != prev)))
    def _():
        slot = par_ref[i]

        @pl.when(i == 0)
        def _():
            for cp in weight_copies(e, slot):
                cp.start()

        for cp in weight_copies(e, slot):
            cp.wait()
        nxt = nx_ref[i]

        @pl.when(nxt < N_EXPERTS)
        def _():
            for cp in weight_copies(nxt, 1 - slot):
                cp.start()

        def cast(r, carry):
            sl = pl.ds(pl.multiple_of(r * rows, rows), rows)
            wgu_bf[sl, :] = wgu_f32[slot, sl, :].astype(BF16)
            wdn_bf[sl, :] = wdn_f32[slot, sl, :].astype(BF16)
            return carry
        lax.fori_loop(0, D_MODEL // rows, cast, 0)

    def expert_rows(n):
        x = _from_packed_rows(xs_ref.at[pl.ds(0, n * PACKED_ROWS), :], n)
        gu = jnp.dot(x, wgu_bf[...], preferred_element_type=F32) + bgu_ref[...]
        gate = jnp.minimum(gu[:, :D_FF], SWIGLU_LIMIT)
        up = jnp.clip(gu[:, D_FF:], -SWIGLU_LIMIT, SWIGLU_LIMIT)
        a = gate * jax.nn.sigmoid(SWIGLU_ALPHA * gate) * (up + 1.0)
        y = jnp.dot(a.astype(BF16), wdn_bf[...], preferred_element_type=F32) + bdn_ref[...]
        _to_row_tiles(ys_ref.at[pl.ds(0, n * ROW_TILE), :], y)

    used = rv_ref[i]
    lower = -1
    for n in (MOE_ROWS // 4, MOE_ROWS // 2, MOE_ROWS):
        @pl.when(valid & (used > lower) & (used <= n))
        def _(n=n):
            expert_rows(n)
            if n < MOE_ROWS:
                rest = (MOE_ROWS - n) * ROW_TILE
                ys_ref[pl.ds(n * ROW_TILE, rest), :] = jnp.zeros((rest, LANES), F32)
        lower = n

    @pl.when(jnp.logical_not(valid))
    def _():
        ys_ref[...] = jnp.zeros_like(ys_ref)


def _moe(block_e, n_valid, next_e, parity, rows_used, xs, w_gu, b_gu, w_down, b_down, layer):
    n_rows = xs.shape[0] // PACKED_ROWS
    n_blocks = n_rows // MOE_ROWS
    grid_spec = pltpu.PrefetchScalarGridSpec(
        num_scalar_prefetch=5,
        grid=(n_blocks,),
        in_specs=[
            pl.BlockSpec((MOE_ROWS * PACKED_ROWS, LANES),
                         lambda i, be, nv, *_: (jnp.minimum(i, nv[0] - 1), 0)),
            pl.BlockSpec((None, None, 1, 2 * D_FF), lambda i, be, *_: (layer, be[i], 0, 0)),
            pl.BlockSpec((None, None, 1, D_MODEL), lambda i, be, *_: (layer, be[i], 0, 0)),
            pl.BlockSpec(memory_space=pl.ANY),
            pl.BlockSpec(memory_space=pl.ANY),
        ],
        out_specs=pl.BlockSpec((MOE_ROWS * ROW_TILE, LANES), lambda i, *_: (i, 0)),
        scratch_shapes=[pltpu.VMEM((2, D_MODEL, 2 * D_FF), F32),
                        pltpu.VMEM((2, D_FF, D_MODEL), F32),
                        pltpu.VMEM((D_MODEL, 2 * D_FF), BF16),
                        pltpu.VMEM((D_FF, D_MODEL), BF16),
                        pltpu.SemaphoreType.DMA((2,))],
    )
    return pl.pallas_call(
        functools.partial(_moe_kernel, layer=layer),
        grid_spec=grid_spec,
        out_shape=jax.ShapeDtypeStruct((n_rows * ROW_TILE, LANES), F32),
        compiler_params=_params(("arbitrary",)),
        name="moe",
    )(block_e, n_valid, next_e, parity, rows_used, xs, b_gu.reshape(DEPTH, N_EXPERTS, 1, 2 * D_FF),
      b_down.reshape(DEPTH, N_EXPERTS, 1, D_MODEL), w_gu, w_down)


def _combine_kernel(plan_ref, next_plan_ref, ls_ref, gate_ref, x1_ref, gf_ref, g2_ref, b2_ref,
                    ys_ref, o_ref, stage, ft, sems, *, ls_blocked):
    i = pl.program_id(0)
    t = x1_ref.shape[0]
    slot = i % 2

    def start_fetch(tile_plan_ref, s):
        _start_run_copies(
            tile_plan_ref, t,
            lambda loc, glob, n: pltpu.make_async_copy(ys_ref.at[_tile_rows(glob, n), :],
                                                       stage.at[s, _tile_rows(loc, n), :],
                                                       sems.at[s]))

    @pl.when(i == 0)
    def _():
        start_fetch(plan_ref, 0)

    for s in range(2):
        @pl.when((i + 1 < pl.num_programs(0)) & (slot == 1 - s))
        def _(s=s):
            start_fetch(next_plan_ref, s)

    pltpu.make_async_copy(ys_ref.at[_tile_rows(0, t * TOP_K), :], stage.at[slot],
                          sems.at[slot]).wait()

    base = 0 if ls_blocked else i * (t * TOP_K)

    def gate_sum(s):
        def gather(g, carry):
            for u in range(TOKEN_UNROLL):
                tok = g * TOKEN_UNROLL + u
                acc = None
                for k in range(TOP_K):
                    j = base + k * t + tok
                    v = stage[s, _rows_at(ls_ref[j]), :] * gate_ref[j]
                    acc = v if acc is None else acc + v
                ft[_tile_rows(tok), :] = acc
            return carry

        lax.fori_loop(0, t // TOKEN_UNROLL, gather, 0)

    for s in range(2):
        pl.when(slot == s)(functools.partial(gate_sum, s))
    f = _from_row_tiles(ft, t)
    z = DEEPNORM_ALPHA * x1_ref[...] + gf_ref[...] * f
    o_ref[...] = _layer_norm(z) * g2_ref[...] + b2_ref[...]


def _combine(plan, ls, gates, x1, gf, g2, b2, ys, tile, tiles_per_seq):
    n = x1.shape[0]
    n_tiles = n // tile
    ls_blocked = (tile * TOP_K) % 1024 == 0

    def smem_vec():
        if ls_blocked:
            return pl.BlockSpec((tile * TOP_K,), lambda i, *_: (i,), memory_space=pltpu.SMEM)
        return pl.BlockSpec(memory_space=pltpu.SMEM)

    grid_spec = pltpu.PrefetchScalarGridSpec(
        num_scalar_prefetch=0,
        grid=(n_tiles,),
        in_specs=[
            pl.BlockSpec((PLAN_WORDS,), lambda i: (i,), memory_space=pltpu.SMEM),
            pl.BlockSpec((PLAN_WORDS,), lambda i: (jnp.minimum(i + 1, n_tiles - 1),),
                         memory_space=pltpu.SMEM),
            smem_vec(), smem_vec(),
            pl.BlockSpec((tile, D_MODEL), lambda i, *_: (i, 0)),
            pl.BlockSpec((None, 1, D_MODEL), lambda i, *_: (i // tiles_per_seq, 0, 0)),
            pl.BlockSpec((1, D_MODEL), lambda i, *_: (0, 0)),
            pl.BlockSpec((1, D_MODEL), lambda i, *_: (0, 0)),
            pl.BlockSpec(memory_space=pl.ANY),
        ],
        out_specs=pl.BlockSpec((tile, D_MODEL), lambda i, *_: (i, 0)),
        scratch_shapes=[pltpu.VMEM((2, tile * TOP_K * ROW_TILE, LANES), F32),
                        pltpu.VMEM((tile * ROW_TILE, LANES), F32),
                        pltpu.SemaphoreType.DMA((2,))],
    )
    return pl.pallas_call(
        functools.partial(_combine_kernel, ls_blocked=ls_blocked),
        grid_spec=grid_spec,
        out_shape=jax.ShapeDtypeStruct((n, D_MODEL), F32),
        compiler_params=_params(("arbitrary",)),
        name="combine",
    )(plan, plan, ls, gates, x1, gf, g2, b2, ys)


def _alibi_slopes():
    return jnp.exp2(-ALIBI_MAX * (jnp.arange(N_HEADS, dtype=F32) + 1.0) / N_HEADS)


def _prompt_bias(slopes):
    r = jnp.arange(Q_TILE)
    j = jnp.arange(WINDOW + Q_TILE)
    dc = (WINDOW + r)[:, None] // CHUNK - j[None, :] // CHUNK
    vis = (dc >= 0) & (dc <= WIN_CHUNKS)
    dist = jnp.abs(r[:, None] + WINDOW - j[None, :]).astype(F32)
    bias = jnp.where(vis[None], -slopes[:, None, None] * dist[None] * LOG2_E, NEG_INF)
    return jnp.stack([bias, jnp.where(j[None, None, :] < WINDOW, NEG_INF, bias)])


def _sample_bias(slopes, t, hist):
    qpos = PAST_LEN + jnp.arange(t)
    kpos = PAST_LEN - hist + jnp.arange(hist + t)
    dc = qpos[:, None] // CHUNK - kpos[None, :] // CHUNK
    vis = (dc >= 0) & (dc <= WIN_CHUNKS) & (kpos[None, :] >= 0)
    dist = jnp.abs(qpos[:, None] - kpos[None, :]).astype(F32)
    return jnp.where(vis[None], -slopes[:, None, None] * dist[None] * LOG2_E, NEG_INF)[None]


def _proj_weights(w_in, b_in):
    col_scale = jnp.where(jnp.arange(PROJ_COLS) < ATT_DIM, ATTN_SCALE * LOG2_E, 1.0).astype(F32)
    return (w_in * col_scale).astype(BF16), (b_in * col_scale).reshape(1, PROJ_COLS)


def _pool_weight(w_pool):
    n = len(POOL_WINDOWS)
    eye = jnp.eye(n, dtype=w_pool.dtype)
    return jnp.einsum('gcd,gh->gchd', w_pool, eye).reshape(POOL_DIM, POOL_DIM).astype(BF16)


def _split_kv(kv, nb, rows):
    tail = kv[:, -rows:].astype(F32)
    shape = (nb, rows, N_KV, HEAD_DIM)
    return tail[..., :KV_DIM].reshape(shape), tail[..., KV_DIM:].reshape(shape)


def kernel(x_prompt, x_sample, c_prompt, c_sample, cache_k, cache_v, state_pool, w_ada, b_ada,
           w_in, b_in, sinks, w_pool, pool_scale, w_out, b_out, ln1_g, ln1_b, ln2_g, ln2_b,
           w_router, b_router, w_gu, b_gu, w_down, b_down):
    nbp, seq, _ = x_prompt.shape
    nbs, tdec, _ = x_sample.shape
    hist = cache_k.shape[2]
    n_p, n_s = nbp * seq, nbs * tdec
    n_asg = (n_p + n_s) * TOP_K
    n_blocks = -(-(n_asg + N_EXPERTS * (MOE_ROWS - 1)) // MOE_ROWS)
    n_rows = n_blocks * MOE_ROWS
    assert hist == WINDOW and tdec >= POOL_PAD and seq % TOK_TILE == 0

    nb_all = nbp + nbs
    nb_pad = -(-nb_all // SUBLANES) * SUBLANES
    c_all = jnp.concatenate([c_prompt, c_sample, jnp.zeros((nb_pad - nb_all, D_MODEL), F32)], 0)
    mod_all = _adaln(c_all, w_ada, b_ada).reshape(DEPTH, nb_pad, 6, D_MODEL)

    slopes = _alibi_slopes()
    bias_p = _prompt_bias(slopes)
    bias_s = _sample_bias(slopes, tdec, hist)
    row = lambda v: v.reshape(1, -1)

    yp, ys_tok = x_prompt, x_sample
    outs = {k: [] for k in ("kp", "vp", "pp", "ks", "vs", "ps")}
    for l in range(DEPTH):
        mod = mod_all[l]
        w_proj, b_proj = _proj_weights(w_in[l], b_in[l])
        wpool = _pool_weight(w_pool[l])
        pscale = row(pool_scale[l])
        wout = w_out[l].astype(BF16)
        wr_t = w_router[l].T
        wr_hi = wr_t.astype(BF16)
        wr = jnp.concatenate([wr_hi, (wr_t - wr_hi.astype(F32)).astype(BF16)], axis=0)
        br = b_router[l].reshape(N_EXPERTS, 1)

        q, kv, u = _inproj(yp, mod, w_proj, b_proj, INPROJ_TILE, 0)
        sinks2 = sinks[l] * LOG2_E
        mix_p = _mixer(sinks2, q, kv, u, None, None, bias_p, wpool, pscale,
                       tile=TOK_TILE, tq=Q_TILE, history_starts_empty=True, pos0=0)
        k_tail, v_tail = _split_kv(kv, nbp, WINDOW)
        outs["kp"].append(k_tail)
        outs["vp"].append(v_tail)
        outs["pp"].append(u[:, -POOL_PAD:])

        qs, kvs, us = _inproj(ys_tok, mod, w_proj, b_proj, tdec, nbp)
        cache_kv = jnp.concatenate([cache_k[l].reshape(nbs, hist, KV_DIM),
                                    cache_v[l].reshape(nbs, hist, KV_DIM)], axis=-1).astype(BF16)
        sp = jnp.pad(state_pool[l], ((0, 0), (POOL_HALO - POOL_PAD, 0), (0, 0)))
        mix_s = _mixer(sinks2, qs, kvs, us, cache_kv, sp, bias_s, wpool, pscale,
                       tile=tdec, tq=tdec, history_starts_empty=False, pos0=PAST_LEN)
        k_new, v_new = _split_kv(kvs, nbs, tdec)
        outs["ks"].append(jnp.concatenate([cache_k[l], k_new], axis=1)[:, -hist:])
        outs["vs"].append(jnp.concatenate([cache_v[l], v_new], axis=1)[:, -hist:])
        outs["ps"].append(us[:, -POOL_PAD:])

        post_w = (wout, row(b_out[l]), row(ln1_g[l]), row(ln1_b[l]), wr, br)
        zero_counts = jnp.zeros((N_EXPERTS, LANES), F32)
        x1p, h2tp, rlp, rgp, meta_p, cnt_p = _post(mix_p, yp, mod, *post_w, zero_counts, TOK_TILE, 0)
        x1s, h2ts, rls, rgs, meta_s, cnt = _post(mix_s, ys_tok, mod, *post_w, cnt_p, tdec, nbp)

        counts = cnt[:, 0].astype(jnp.int32)
        padded = (counts + MOE_ROWS - 1) // MOE_ROWS * MOE_ROWS
        pad_ends = jnp.cumsum(padded).astype(jnp.int32)
        pad_starts = pad_ends - padded
        block_start = jnp.arange(n_blocks, dtype=jnp.int32) * MOE_ROWS
        block_e = jnp.minimum(jnp.sum(pad_ends[None, :] <= block_start[:, None], axis=1),
                              N_EXPERTS - 1).astype(jnp.int32)
        n_valid = (pad_ends[-1:] // MOE_ROWS).astype(jnp.int32)
        ids = jnp.arange(N_EXPERTS, dtype=jnp.int32)
        present = padded > 0
        later = (ids[None, :] > ids[:, None]) & present[None, :]
        next_present = jnp.min(jnp.where(later, ids[None, :], N_EXPERTS), axis=1)
        ordinal = jnp.cumsum(present.astype(jnp.int32)) - 1
        is_block_e = block_e[:, None] == ids[None, :]
        next_e = jnp.sum(jnp.where(is_block_e, next_present[None, :], 0), axis=1).astype(jnp.int32)
        parity = jnp.sum(jnp.where(is_block_e, ordinal[None, :] % 2, 0), axis=1).astype(jnp.int32)
        seg_used_end = jnp.sum(jnp.where(is_block_e, (pad_starts + counts)[None, :], 0), axis=1)
        rows_used = jnp.clip(seg_used_end - block_start, 0, MOE_ROWS).astype(jnp.int32)

        def copy_plan(meta, tile):
            m = meta.reshape(-1, N_EXPERTS, LANES)
            return _copy_plan(m[:, :, 0], m[:, :, 1], pad_starts[None, :] + m[:, :, 2],
                              tile.bit_length())

        def per_assignment(r, n, first=0):
            return r[:, :, first:first + TOP_K, :].reshape(n * TOP_K)

        plan_p, plan_s = copy_plan(meta_p, TOK_TILE), copy_plan(meta_s, tdec)
        ls_p, ls_s = per_assignment(rlp, n_p), per_assignment(rls, n_s)

        xs = _dispatch(plan_p, pad_ends, per_assignment(rlp, n_p, TOP_K),
                       h2tp.reshape(n_p * PACKED_ROWS, LANES), None, n_rows, TOK_TILE,
                       later_rows=n_s)
        xs = _dispatch(plan_s, pad_ends, per_assignment(rls, n_s, TOP_K),
                       h2ts.reshape(n_s * PACKED_ROWS, LANES), xs, n_rows, tdec)
        ye = _moe(block_e, n_valid, next_e, parity, rows_used, xs, w_gu, b_gu, w_down, b_down, l)

        g2, b2 = row(ln2_g[l]), row(ln2_b[l])
        gf = mod[:, 5].reshape(nb_pad, 1, D_MODEL)
        yp = _combine(plan_p, ls_p, per_assignment(rgp, n_p), x1p.reshape(n_p, D_MODEL),
                      gf[:nbp], g2, b2, ye, TOK_TILE, seq // TOK_TILE).reshape(nbp, seq, D_MODEL)
        ys_tok = _combine(plan_s, ls_s, per_assignment(rgs, n_s), x1s.reshape(n_s, D_MODEL),
                          gf[nbp:nb_all], g2, b2, ye, tdec, 1).reshape(nbs, tdec, D_MODEL)

    st = lambda k: jnp.stack(outs[k])
    return (yp, ys_tok, st("kp"), st("vp"), st("pp"), st("ks"), st("vs"), st("ps"))
```
